```python
import math
import jax
import jax.numpy as jnp
from jax import lax
import numpy as np

D_MODEL = 1024
BATCH = 4
SEQ = 4096
DEPTH = 2

PLE_DIM = 256
D_FF = 2816
LN_EPS = 1e-5
DEEPNORM_ALPHA = (2 * DEPTH) ** 0.25
DEEPNORM_BETA = (8 * DEPTH) ** -0.25
MACARON_W = 0.5

RW_HEADS = 8
RW_HEAD = 64
RW_WIDTH = RW_HEADS * RW_HEAD
RW_DECAY_LORA = 64
RW_AAA_LORA = 64
RW_GATE_LORA = 160
RW_GN_EPS = 1e-5 * RW_HEAD

GLA_HEADS = 4
GLA_DK = 64
GLA_DV = 128
GLA_GATE_LORA = 16
GLA_GATE_NORM = 16.0
GLA_CHUNK = 64

GDN_HEADS = 4
GDN_DK = 128
GDN_DV = 128
GDN_CONV = 4
GDN_CHUNK = 64

HEAD_NORM_EPS = 1e-6
L2_EPS = 1e-6
N_BRANCH = 3
BRANCH_WIDTH = 512

RW_SPLITS = (RW_WIDTH, RW_WIDTH, RW_WIDTH, RW_DECAY_LORA, RW_AAA_LORA, RW_GATE_LORA)
GLA_SPLITS = (GLA_HEADS * GLA_DK, GLA_HEADS * GLA_DK, GLA_HEADS * GLA_DV, GLA_GATE_LORA, GLA_HEADS * GLA_DV)
GDN_QKV = GDN_HEADS * (2 * GDN_DK + GDN_DV)
GDN_SPLITS = (GDN_QKV, GDN_HEADS, GDN_HEADS, GDN_HEADS * GDN_DV)
RW_IN = sum(RW_SPLITS)
GROUP_SPLITS = (RW_IN, sum(GLA_SPLITS), sum(GDN_SPLITS), N_BRANCH * D_MODEL)
D_IN = sum(GROUP_SPLITS)

kernel_name = "hybrid_rwkv7_gla_gdn_macaron_deepnorm"


def _split(h, sizes):
    idx, s = [], 0
    for w in sizes[:-1]:
        s += w
        idx.append(s)
    return jnp.split(h, idx, axis=-1)


def _heads(t, h, d):
    return t.reshape(t.shape[:-1] + (h, d))


def _layer_norm(x, g, b):
    xf = x.astype(jnp.float32)
    m = jnp.mean(xf, -1, keepdims=True)
    var = jnp.mean(jnp.square(xf - m), -1, keepdims=True)
    return ((xf - m) * lax.rsqrt(var + LN_EPS) * g + b).astype(x.dtype)


def _rms_norm(x, g):
    return x * lax.rsqrt(jnp.mean(jnp.square(x), -1, keepdims=True) + HEAD_NORM_EPS) * g


def _l2norm(x):
    return x * lax.rsqrt(jnp.sum(jnp.square(x), -1, keepdims=True) + L2_EPS)


def _swiglu(x, w1, w3, w2):
    return (jax.nn.silu(x @ w1) * (x @ w3)) @ w2


def rwkv7_time_mix(h_rw, mu, w0, w2, a0, a2, g2, k_k, k_a, r_k, gn_g, gn_b):
    B, T, _ = h_rw.shape
    prev = jnp.pad(h_rw, ((0, 0), (1, 0), (0, 0)))[:, :-1]
    h_rw = h_rw + (prev - h_rw) * mu
    r, k, v, w_lo, a_lo, g_lo = _split(h_rw, RW_SPLITS)
    w = -jax.nn.softplus(-(w0 + jnp.tanh(w_lo) @ w2)) - 0.5
    decay = jnp.exp(-jnp.exp(w))
    a = jax.nn.sigmoid(a0 + a_lo @ a2)
    g = jax.nn.sigmoid(g_lo) @ g2
    kk = _l2norm(_heads(k * k_k, RW_HEADS, RW_HEAD))
    k = k * (1.0 + (a - 1.0) * k_a)
    r, k, v, decay, a = (_heads(t, RW_HEADS, RW_HEAD) for t in (r, k, v, decay, a))

    def step(S, inp):
        r_t, k_t, v_t, d_t, a_t, b_t = inp
        sa = jnp.einsum('bhvk,bhk->bhv', S, a_t)
        S = S * d_t[:, :, None, :] + sa[..., None] * b_t[:, :, None, :] + v_t[..., None] * k_t[:, :, None, :]
        return S, jnp.einsum('bhvk,bhk->bhv', S, r_t)

    xs = tuple(jnp.moveaxis(t, 1, 0) for t in (r, k, v, decay, -kk, kk * a))
    S0 = jnp.zeros((B, RW_HEADS, RW_HEAD, RW_HEAD), jnp.float32)
    _, y = lax.scan(step, S0, xs)
    y = jnp.moveaxis(y, 0, 1)
    ym = jnp.mean(y, -1, keepdims=True)
    yv = jnp.mean(jnp.square(y - ym), -1, keepdims=True)
    y = ((y - ym) * lax.rsqrt(yv + RW_GN_EPS)).reshape(B, T, RW_WIDTH) * gn_g + gn_b
    bonus = jnp.sum(r * k * r_k, -1, keepdims=True) * v
    return (y + bonus.reshape(B, T, RW_WIDTH)) * g


def gla_mix(q, k, v, gk_lo, gate, gk_w2, gk_b, norm_g):
    B, T, _ = q.shape
    C, n, H = GLA_CHUNK, T // GLA_CHUNK, GLA_HEADS
    log_a = jax.nn.log_sigmoid(gk_lo @ gk_w2 + gk_b) / GLA_GATE_NORM

    def chunked(t, d):
        return t.reshape(B, n, C, H, d).transpose(1, 0, 3, 2, 4)

    q = chunked(q * GLA_DK ** -0.5, GLA_DK)
    k = chunked(k, GLA_DK)
    v = chunked(v, GLA_DV)
    b = jnp.cumsum(chunked(log_a, GLA_DK), axis=3)
    causal = jnp.tril(jnp.ones((C, C), dtype=bool))[:, :, None]

    def step(S, inp):
        q_c, k_c, v_c, b_c = inp
        rel = jnp.where(causal, b_c[:, :, :, None, :] - b_c[:, :, None, :, :], -jnp.inf)
        att = jnp.sum(q_c[:, :, :, None, :] * k_c[:, :, None, :, :] * jnp.exp(rel), axis=-1)
        o = att @ v_c + (q_c * jnp.exp(b_c)) @ S
        b_end = b_c[:, :, -1:, :]
        S = S * jnp.exp(b_end[:, :, 0, :, None]) + jnp.einsum('bhjd,bhjv->bhdv', k_c * jnp.exp(b_end - b_c), v_c)
        return S, o

    S0 = jnp.zeros((B, H, GLA_DK, GLA_DV), jnp.float32)
    _, o = lax.scan(step, S0, (q, k, v, b))
    o = o.transpose(1, 0, 3, 2, 4).reshape(B, T, H, GLA_DV)
    return _rms_norm(o, norm_g).reshape(B, T, H * GLA_DV) * jax.nn.silu(gate)


def gated_deltanet_mix(qkv, a_in, b_in, gate, conv_w, a_log, dt_bias, norm_g):
    B, T, _ = qkv.shape
    C, n, H = GDN_CHUNK, T // GDN_CHUNK, GDN_HEADS
    qkv = jax.nn.silu(lax.conv_general_dilated(
        qkv, conv_w[:, None, :].astype(qkv.dtype), (1,), [(GDN_CONV - 1, 0)],
        dimension_numbers=('NWC', 'WIO', 'NWC'), feature_group_count=qkv.shape[-1]))
    q, k, v = _split(qkv, (H * GDN_DK, H * GDN_DK, H * GDN_DV))
    q = _l2norm(_heads(q, H, GDN_DK)) * GDN_DK ** -0.5
    k = _l2norm(_heads(k, H, GDN_DK))
    v = _heads(v, H, GDN_DV)
    beta = jax.nn.sigmoid(b_in)
    g = -jnp.exp(a_log) * jax.nn.softplus(a_in + dt_bias)

    def chunked(t):
        t = jnp.moveaxis(t, 2, 1)
        return t.reshape((B, H, n, C) + t.shape[3:])

    q, k, v, beta, g = (chunked(t) for t in (q, k, v, beta, g))
    g = jnp.cumsum(g, axis=-1)
    causal = jnp.tril(jnp.ones((C, C), dtype=bool))
    strict = jnp.tril(jnp.ones((C, C), dtype=bool), -1)
    decay = jnp.exp(jnp.where(causal, g[..., :, None] - g[..., None, :], -jnp.inf))
    kb = k * beta[..., None]
    lower = jnp.where(strict, jnp.einsum('bhnid,bhnjd->bhnij', kb, k) * decay, 0.0)
    eye = jnp.eye(C, dtype=lower.dtype)
    t_inv = lax.linalg.triangular_solve(eye + lower, jnp.broadcast_to(eye, lower.shape),
                                        left_side=True, lower=True)
    u = t_inv @ (v * beta[..., None])
    w = t_inv @ (kb * jnp.exp(g)[..., None])
    att = jnp.einsum('bhnid,bhnjd->bhnij', q, k) * decay
    xs = tuple(jnp.moveaxis(t, 2, 0) for t in (q, k, u, w, g, att))

    def step(S, inp):
        q_c, k_c, u_c, w_c, g_c, att_c = inp
        v_new = u_c - w_c @ S
        o = (q_c * jnp.exp(g_c)[..., None]) @ S + att_c @ v_new
        g_end = g_c[..., -1:]
        S = S * jnp.exp(g_end)[..., None] + jnp.einsum('bhcd,bhcv->bhdv', k_c * jnp.exp(g_end - g_c)[..., None], v_new)
        return S, o

    S0 = jnp.zeros((B, H, GDN_DK, GDN_DV), jnp.float32)
    _, o = lax.scan(step, S0, xs)
    o = o.transpose(1, 0, 3, 2, 4).reshape(B, T, H, GDN_DV)
    return _rms_norm(o, norm_g).reshape(B, T, H * GDN_DV) * jax.nn.silu(gate)


def token_mixing(x, w_in, rw_mu, rw_w0, rw_w2, rw_a0, rw_a2, rw_g2, rw_k_k, rw_k_a, rw_r_k, rw_gn_g, rw_gn_b,
                 gla_gk_w2, gla_gk_b, gla_norm_g, gdn_conv_w, gdn_a_log, gdn_dt_bias, gdn_norm_g, w_branch, w_o):
    B, T, _ = x.shape
    h = (x @ w_in).astype(jnp.float32)
    h_rw, h_gla, h_gdn, h_gate = _split(h, GROUP_SPLITS)
    o_rw = rwkv7_time_mix(h_rw, rw_mu, rw_w0, rw_w2, rw_a0, rw_a2, rw_g2, rw_k_k, rw_k_a, rw_r_k, rw_gn_g, rw_gn_b)
    o_gla = gla_mix(*_split(h_gla, GLA_SPLITS), gla_gk_w2, gla_gk_b, gla_norm_g)
    o_gdn = gated_deltanet_mix(*_split(h_gdn, GDN_SPLITS), gdn_conv_w, gdn_a_log, gdn_dt_bias, gdn_norm_g)
    branches = jnp.stack([o_rw, o_gla, o_gdn], axis=2)
    proj = jnp.einsum('btnc,ncd->btnd', branches, w_branch)
    gates = jax.nn.sigmoid(h_gate).reshape(B, T, N_BRANCH, D_MODEL)
    merged = jnp.sum(gates * proj, axis=2)
    return merged.astype(x.dtype) @ w_o


def setup_inputs(seed: int = 0) -> dict:
    key = jax.random.key(seed)
    ks = iter(jax.random.split(key, 40))
    L, D = DEPTH, D_MODEL
    bt = DEEPNORM_BETA

    def nrm(shape, scale):
        return jax.random.normal(next(ks), shape, jnp.float32) * scale

    def uni(shape, lo, hi):
        return jax.random.uniform(next(ks), shape, jnp.float32, lo, hi)

    dt = jnp.exp(uni((L, GDN_HEADS), math.log(1e-3), math.log(1e-1)))
    return {
        "x": nrm((BATCH, SEQ, D), 1.0),
        "p": nrm((L, BATCH, SEQ, PLE_DIM), 1.0),
        "ln_g": 1.0 + nrm((L, 4, D), 0.02),
        "ln_b": nrm((L, 4, D), 0.02),
        "ffn_w1": nrm((L, 2, D, D_FF), D ** -0.5),
        "ffn_w3": nrm((L, 2, D, D_FF), D ** -0.5),
        "ffn_w2": nrm((L, 2, D_FF, D), bt * D_FF ** -0.5),
        "w_in": nrm((L, D, D_IN), D ** -0.5),
        "rw_mu": uni((L, RW_IN), 0.0, 1.0),
        "rw_w0": uni((L, RW_WIDTH), -6.0, -1.0),
        "rw_w2": nrm((L, RW_DECAY_LORA, RW_WIDTH), 0.5 * RW_DECAY_LORA ** -0.5),
        "rw_a0": nrm((L, RW_WIDTH), 0.1),
        "rw_a2": nrm((L, RW_AAA_LORA, RW_WIDTH), 0.5 * RW_AAA_LORA ** -0.5),
        "rw_g2": nrm((L, RW_GATE_LORA, RW_WIDTH), RW_GATE_LORA ** -0.5),
        "rw_k_k": 0.85 + nrm((L, RW_WIDTH), 0.05),
        "rw_k_a": 1.0 + nrm((L, RW_WIDTH), 0.05),
        "rw_r_k": nrm((L, RW_HEADS, RW_HEAD), 0.1),
        "rw_gn_g": 1.0 + nrm((L, RW_WIDTH), 0.02),
        "rw_gn_b": nrm((L, RW_WIDTH), 0.02),
        "gla_gk_w2": nrm((L, GLA_GATE_LORA, GLA_HEADS * GLA_DK), GLA_GATE_LORA ** -0.5),
        "gla_gk_b": uni((L, GLA_HEADS * GLA_DK), 0.0, 3.0),
        "gla_norm_g": 1.0 + nrm((L, GLA_DV), 0.02),
        "gdn_conv_w": nrm((L, GDN_CONV, GDN_QKV), GDN_CONV ** -0.5),
        "gdn_a_log": jnp.log(uni((L, GDN_HEADS), 1.0, 16.0)),
        "gdn_dt_bias": dt + jnp.log(-jnp.expm1(-dt)),
        "gdn_norm_g": 1.0 + nrm((L, GDN_DV), 0.02),
        "w_branch": nrm((L, N_BRANCH, BRANCH_WIDTH, D), bt * BRANCH_WIDTH ** -0.5),
        "w_o": nrm((L, D, D), bt * D ** -0.5),
        "ple_w_gate": nrm((L, D, D), D ** -0.5),
        "ple_w_proj": nrm((L, PLE_DIM, D), bt * PLE_DIM ** -0.5),
    }


def reference(x, p, ln_g, ln_b, ffn_w1, ffn_w3, ffn_w2, w_in, rw_mu, rw_w0, rw_w2, rw_a0, rw_a2, rw_g2,
              rw_k_k, rw_k_a, rw_r_k, rw_gn_g, rw_gn_b, gla_gk_w2, gla_gk_b, gla_norm_g, gdn_conv_w, gdn_a_log,
              gdn_dt_bias, gdn_norm_g, w_branch, w_o, ple_w_gate, ple_w_proj):
    a = DEEPNORM_ALPHA
    for i in range(DEPTH):
        x = _layer_norm(a * x + MACARON_W * _swiglu(x, ffn_w1[i, 0], ffn_w3[i, 0], ffn_w2[i, 0]), ln_g[i, 0], ln_b[i, 0])
        mix = token_mixing(x, w_in[i], rw_mu[i], rw_w0[i], rw_w2[i], rw_a0[i], rw_a2[i], rw_g2[i], rw_k_k[i],
                           rw_k_a[i], rw_r_k[i], rw_gn_g[i], rw_gn_b[i], gla_gk_w2[i], gla_gk_b[i], gla_norm_g[i],
                           gdn_conv_w[i], gdn_a_log[i], gdn_dt_bias[i], gdn_norm_g[i], w_branch[i], w_o[i])
        x = _layer_norm(a * x + mix, ln_g[i, 1], ln_b[i, 1])
        x = _layer_norm(a * x + MACARON_W * _swiglu(x, ffn_w1[i, 1], ffn_w3[i, 1], ffn_w2[i, 1]), ln_g[i, 2], ln_b[i, 2])
        ple = jax.nn.sigmoid(x @ ple_w_gate[i]) * (p[i] @ ple_w_proj[i])
        x = _layer_norm(a * x + ple, ln_g[i, 3], ln_b[i, 3])
    return x
```

```python
import functools

import jax
import jax.numpy as jnp
from jax import lax
from jax.experimental import pallas as pl
from jax.experimental.pallas import tpu as pltpu

F32 = jnp.float32
BF16 = jnp.bfloat16

D_MODEL = 1024
DEPTH = 2
PLE_DIM = 256
D_FF = 2816
LN_EPS = 1e-5
DEEPNORM_ALPHA = (2 * DEPTH) ** 0.25
MACARON_W = 0.5

RW_HEADS = 8
RW_HEAD = 64
RW_WIDTH = RW_HEADS * RW_HEAD
RW_DECAY_LORA = 64
RW_AAA_LORA = 64
RW_GATE_LORA = 160
RW_GN_EPS = 1e-5 * RW_HEAD
RW_IN = 3 * RW_WIDTH + RW_DECAY_LORA + RW_AAA_LORA + RW_GATE_LORA

GLA_HEADS = 4
GLA_DK = 64
GLA_DV = 128
GLA_GATE_LORA = 16
GLA_GATE_NORM = 16.0
GLA_IN = 2 * GLA_HEADS * GLA_DK + 2 * GLA_HEADS * GLA_DV + GLA_GATE_LORA

GDN_HEADS = 4
GDN_DK = 128
GDN_DV = 128
GDN_CONV = 4
GDN_QKV = GDN_HEADS * (2 * GDN_DK + GDN_DV)
GDN_IN = GDN_QKV + 2 * GDN_HEADS + GDN_HEADS * GDN_DV

HEAD_NORM_EPS = 1e-6
L2_EPS = 1e-6
N_BRANCH = 3
BRANCH_WIDTH = 512

LANES = 128
CHUNK = 64
RW_PAD = 1920
GLA_PAD = 1664
GDN_PAD = 2176
MIX_PAD = RW_PAD + GLA_PAD + GDN_PAD
VMEM_LIMIT = 56 * 1024 * 1024


def _cparams(sem):
    return pltpu.CompilerParams(dimension_semantics=sem, vmem_limit_bytes=VMEM_LIMIT)


def _const_spec(shape):
    nd = len(shape)
    return pl.BlockSpec(shape, lambda *_: (0,) * nd, pipeline_mode=pl.Buffered(1))


def _mm(a, b):
    return jnp.dot(a.astype(BF16), b.astype(BF16), preferred_element_type=F32)


def _mm_nt(a, b):
    return lax.dot_general(a.astype(BF16), b.astype(BF16), (((1,), (1,)), ((), ())),
                           preferred_element_type=F32)


def _mm_f32(a, b):
    return jnp.dot(a, b, preferred_element_type=F32, precision=lax.Precision.HIGHEST)


def _layer_norm(y, g, b):
    m = jnp.mean(y, -1, keepdims=True)
    d = y - m
    var = jnp.mean(d * d, -1, keepdims=True)
    return d * lax.rsqrt(var + LN_EPS) * g + b


def _sigmoid(x):
    return 1.0 / (1.0 + jnp.exp(-x))


def _softplus(x):
    return jnp.maximum(x, 0.0) + jnp.log(1.0 + jnp.exp(-jnp.abs(x)))


def _iota2(shape):
    return (lax.broadcasted_iota(jnp.int32, shape, 0), lax.broadcasted_iota(jnp.int32, shape, 1))


def _tril_f32(n):
    r, c = _iota2((n, n))
    return jnp.where(r >= c, 1.0, 0.0).astype(F32)


def _sibling_mask(r, c, s):
    k = s.bit_length() - 1
    return ((r >> (k + 1)) == (c >> (k + 1))) & (((r >> k) & 1) == 1) & (((c >> k) & 1) == 0)


def _unit_lower_inverse(low, n):
    r, c = _iota2((n, n))
    eye = jnp.where(r == c, 1.0, 0.0).astype(F32)
    x = eye - jnp.where(_sibling_mask(r, c, 1), low, 0.0)
    s = 2
    while s < CHUNK:
        l_off = jnp.where(_sibling_mask(r, c, s), low, 0.0)
        x = x - _mm(x, _mm(l_off, x))
        s *= 2
    return x


def _stack_heads(z):
    lane = lax.broadcasted_iota(jnp.int32, z.shape, 1)
    first = lane < (LANES // 2)
    return jnp.concatenate([jnp.where(first, z, 0.0), jnp.where(first, 0.0, z)], axis=0)


def _block_causal(n, strict):
    r, c = _iota2((n, n))
    same = (r >> 6) == (c >> 6)
    return same & ((r > c) if strict else (r >= c))


FF_CHUNK = 1408


def _swiglu_ln(x, w1_ref, w3_ref, w2_ref, g, b):
    xb = x.astype(BF16)
    acc = None
    for c0 in range(0, D_FF, FF_CHUNK):
        h1 = jnp.dot(xb, w1_ref[:, c0:c0 + FF_CHUNK], preferred_element_type=F32)
        h3 = jnp.dot(xb, w3_ref[:, c0:c0 + FF_CHUNK], preferred_element_type=F32)
        act = (h1 * _sigmoid(h1) * h3).astype(BF16)
        part = jnp.dot(act, w2_ref[c0:c0 + FF_CHUNK, :], preferred_element_type=F32)
        acc = part if acc is None else acc + part
    return _layer_norm(DEEPNORM_ALPHA * x + MACARON_W * acc, g, b)


def _ffn_body(x_ref, w1_ref, w3_ref, w2_ref, g_ref, b_ref, o_ref):
    o_ref[...] = _swiglu_ln(x_ref[...], w1_ref, w3_ref, w2_ref, g_ref[...], b_ref[...])


def _ffn_ple_body(x_ref, p_ref, w1_ref, w3_ref, w2_ref, g_ref, b_ref, wg_ref, wp_ref, g2_ref, b2_ref, o_ref):
    x = _swiglu_ln(x_ref[...], w1_ref, w3_ref, w2_ref, g_ref[...], b_ref[...])
    gate = _sigmoid(jnp.dot(x.astype(BF16), wg_ref[...], preferred_element_type=F32))
    emb = jnp.dot(p_ref[...].astype(BF16), wp_ref[...], preferred_element_type=F32)
    o_ref[...] = _layer_norm(DEEPNORM_ALPHA * x + gate * emb, g2_ref[...], b2_ref[...])


def _ffn_call(x, w1, w3, w2, g, b, tm):
    n = x.shape[0]
    row = pl.BlockSpec((tm, D_MODEL), lambda i: (i, 0))
    return pl.pallas_call(
        _ffn_body,
        grid=(n // tm,),
        in_specs=[row, _const_spec((D_MODEL, D_FF)), _const_spec((D_MODEL, D_FF)), _const_spec((D_FF, D_MODEL)),
                  _const_spec((1, D_MODEL)), _const_spec((1, D_MODEL))],
        out_specs=row,
        out_shape=jax.ShapeDtypeStruct((n, D_MODEL), F32),
        compiler_params=_cparams(("arbitrary",)),
        name="ffn_ln",
    )(x, w1, w3, w2, g, b)


def _ffn_ple_call(x, p, w1, w3, w2, g, b, wg, wp, g2, b2, tm):
    n = x.shape[0]
    row = pl.BlockSpec((tm, D_MODEL), lambda i: (i, 0))
    return pl.pallas_call(
        _ffn_ple_body,
        grid=(n // tm,),
        in_specs=[row, pl.BlockSpec((tm, PLE_DIM), lambda i: (i, 0)),
                  _const_spec((D_MODEL, D_FF)), _const_spec((D_MODEL, D_FF)), _const_spec((D_FF, D_MODEL)),
                  _const_spec((1, D_MODEL)), _const_spec((1, D_MODEL)),
                  _const_spec((D_MODEL, D_MODEL)), _const_spec((PLE_DIM, D_MODEL)),
                  _const_spec((1, D_MODEL)), _const_spec((1, D_MODEL))],
        out_specs=row,
        out_shape=jax.ShapeDtypeStruct((n, D_MODEL), F32),
        compiler_params=_cparams(("arbitrary",)),
        name="ffn_ple_ln",
    )(x, p, w1, w3, w2, g, b, wg, wp, g2, b2)


def _proj_body(x_ref, w_ref, rw_ref, gla_ref, gdn_ref):
    xb = x_ref[...].astype(BF16)
    rw_ref[...] = jnp.dot(xb, w_ref[:, 0:RW_PAD], preferred_element_type=F32)
    gla_ref[...] = jnp.dot(xb, w_ref[:, RW_PAD:RW_PAD + GLA_PAD], preferred_element_type=F32)
    gdn_ref[...] = jnp.dot(xb, w_ref[:, RW_PAD + GLA_PAD:MIX_PAD], preferred_element_type=F32)


def _proj_call(x, w, tm):
    n = x.shape[0]
    return pl.pallas_call(
        _proj_body,
        grid=(n // tm,),
        in_specs=[pl.BlockSpec((tm, D_MODEL), lambda i: (i, 0)), _const_spec((D_MODEL, MIX_PAD))],
        out_specs=[pl.BlockSpec((tm, RW_PAD), lambda i: (i, 0)),
                   pl.BlockSpec((tm, GLA_PAD), lambda i: (i, 0)),
                   pl.BlockSpec((tm, GDN_PAD), lambda i: (i, 0))],
        out_shape=[jax.ShapeDtypeStruct((n, RW_PAD), F32),
                   jax.ShapeDtypeStruct((n, GLA_PAD), F32),
                   jax.ShapeDtypeStruct((n, GDN_PAD), F32)],
        compiler_params=_cparams(("arbitrary",)),
        name="mixer_in_proj",
    )(x, w)


def _merge_body(x_ref, orw_ref, ogla_ref, ogdn_ref, wgate_ref, wbr_ref, wo_ref, g_ref, b_ref, o_ref):
    x = x_ref[...]
    xb = x.astype(BF16)
    merged = None
    for n, br_ref in enumerate((orw_ref, ogla_ref, ogdn_ref)):
        gate = _sigmoid(jnp.dot(xb, wgate_ref[:, n * D_MODEL:(n + 1) * D_MODEL], preferred_element_type=F32))
        term = gate * jnp.dot(br_ref[...], wbr_ref[n], preferred_element_type=F32)
        merged = term if merged is None else merged + term
    mix = jnp.dot(merged.astype(BF16), wo_ref[...], preferred_element_type=F32)
    o_ref[...] = _layer_norm(DEEPNORM_ALPHA * x + mix, g_ref[...], b_ref[...])


def _merge_call(x, o_rw, o_gla, o_gdn, wgate, wbr, wo, g, b, tm):
    n = x.shape[0]
    row = pl.BlockSpec((tm, D_MODEL), lambda i: (i, 0))
    br = pl.BlockSpec((tm, BRANCH_WIDTH), lambda i: (i, 0))
    return pl.pallas_call(
        _merge_body,
        grid=(n // tm,),
        in_specs=[row, br, br, br, _const_spec((D_MODEL, N_BRANCH * D_MODEL)),
                  _const_spec((N_BRANCH, BRANCH_WIDTH, D_MODEL)), _const_spec((D_MODEL, D_MODEL)),
                  _const_spec((1, D_MODEL)), _const_spec((1, D_MODEL))],
        out_specs=row,
        out_shape=jax.ShapeDtypeStruct((n, D_MODEL), F32),
        compiler_params=_cparams(("arbitrary",)),
        name="merge_ln",
    )(x, o_rw, o_gla, o_gdn, wgate, wbr, wo, g, b)


def _gdn_body(h_ref, cw_ref, alog_ref, dtb_ref, ng_ref, o_ref,
              ext_ref, q_s, k_s, v_s, g_s, beta_s, st_ref, *, tb):
    t_idx = pl.program_id(1)

    @pl.when(t_idx == 0)
    def _():
        ext_ref[0:8, :] = jnp.zeros((8, GDN_QKV), F32)
        st_ref[...] = jnp.zeros_like(st_ref)

    @pl.when(t_idx != 0)
    def _():
        ext_ref[0:8, :] = ext_ref[tb:tb + 8, :]

    ext_ref[8:tb + 8, :] = h_ref[:, 0:GDN_QKV]

    hw = GDN_HEADS * GDN_DK
    for grp, dst in enumerate((q_s, k_s, v_s)):
        cols = slice(grp * hw, (grp + 1) * hw)
        acc = ext_ref[8:tb + 8, cols] * cw_ref[GDN_CONV - 1:GDN_CONV, cols]
        for d in range(1, GDN_CONV):
            acc = acc + ext_ref[8 - d:8 - d + tb, cols] * cw_ref[GDN_CONV - 1 - d:GDN_CONV - d, cols]
        act = acc * _sigmoid(acc)
        if grp == 2:
            dst[...] = act
        else:
            scale = GDN_DK ** -0.5 if grp == 0 else 1.0
            for h in range(GDN_HEADS):
                t = act[:, h * GDN_DK:(h + 1) * GDN_DK]
                dst[:, h * GDN_DK:(h + 1) * GDN_DK] = t * (lax.rsqrt(jnp.sum(t * t, -1, keepdims=True) + L2_EPS) * scale)

    ab = h_ref[:, GDN_QKV + GDN_HEADS * GDN_DV:GDN_PAD]
    g_s[...] = -jnp.exp(alog_ref[...]) * _softplus(ab + dtb_ref[...])
    beta_s[...] = _sigmoid(ab)

    n2 = 2 * CHUNK
    ng = ng_ref[...]

    def chunk(c, carry):
        r0 = pl.multiple_of(c * CHUNK, CHUNK)
        rows = pl.ds(r0, CHUNK)
        gc = _mm_f32(_tril_f32(CHUNK), g_s[rows, :])
        beta = beta_s[rows, :]
        causal = _block_causal(n2, strict=False)
        strict = _block_causal(n2, strict=True)
        for pair in range(GDN_HEADS // 2):
            heads = (2 * pair, 2 * pair + 1)
            gi = jnp.concatenate([gc[:, h:h + 1] for h in heads], axis=0)
            bcol = jnp.concatenate([beta[:, GDN_HEADS + h:GDN_HEADS + h + 1] for h in heads], axis=0)
            qs = jnp.concatenate([q_s[rows, h * GDN_DK:(h + 1) * GDN_DK] for h in heads], axis=0)
            ks = jnp.concatenate([k_s[rows, h * GDN_DK:(h + 1) * GDN_DK] for h in heads], axis=0)
            vs = jnp.concatenate([v_s[rows, h * GDN_DV:(h + 1) * GDN_DV] for h in heads], axis=0)
            gi_b = jnp.broadcast_to(gi, (n2, n2))
            diff = gi_b - gi_b.T
            dec = jnp.where(causal, jnp.exp(jnp.where(causal, diff, 0.0)), 0.0)
            kb = ks * bcol
            gram = _mm_nt(jnp.concatenate([kb, qs], axis=0), ks)
            low = jnp.where(strict, gram[0:n2] * dec, 0.0)
            att = gram[n2:2 * n2] * dec
            t_inv = _unit_lower_inverse(low, n2)
            eg = jnp.exp(gi)
            wu = _mm(t_inv, jnp.concatenate([kb * eg, vs * bcol], axis=1))
            aw = _mm(att, wu)
            for j, h in enumerate(heads):
                sl = slice(j * CHUNK, (j + 1) * CHUNK)
                g_end = gc[CHUNK - 1:CHUNK, h:h + 1]
                k_hat = ks[sl] * jnp.exp(g_end - gi[sl])
                kwu = _mm(k_hat.T, wu[sl])
                st = st_ref[h]
                q_eff = qs[sl] * eg[sl] - aw[sl, 0:GDN_DV]
                o = _mm(q_eff, st) + aw[sl, GDN_DV:2 * GDN_DV]
                st_ref[h] = jnp.exp(g_end) * st - _mm(kwu[:, 0:GDN_DV], st) + kwu[:, GDN_DV:2 * GDN_DV]
                o = o * lax.rsqrt(jnp.mean(o * o, -1, keepdims=True) + HEAD_NORM_EPS) * ng
                gate = h_ref[rows, GDN_QKV + h * GDN_DV:GDN_QKV + (h + 1) * GDN_DV]
                o_ref[rows, h * GDN_DV:(h + 1) * GDN_DV] = (o * gate * _sigmoid(gate)).astype(o_ref.dtype)
        return carry

    lax.fori_loop(0, tb // CHUNK, chunk, 0)


def _gdn_call(h, conv_w, alog, dtb, norm_g, bsz, seq, tb):
    nt = seq // tb
    return pl.pallas_call(
        functools.partial(_gdn_body, tb=tb),
        grid=(bsz, nt),
        in_specs=[pl.BlockSpec((tb, GDN_PAD), lambda b, t: (b * nt + t, 0)),
                  _const_spec((GDN_CONV, GDN_QKV)), _const_spec((1, LANES)), _const_spec((1, LANES)),
                  _const_spec((1, GDN_DV))],
        out_specs=pl.BlockSpec((tb, GDN_HEADS * GDN_DV), lambda b, t: (b * nt + t, 0)),
        out_shape=jax.ShapeDtypeStruct((bsz * seq, GDN_HEADS * GDN_DV), BF16),
        scratch_shapes=[pltpu.VMEM((tb + 8, GDN_QKV), F32),
                        pltpu.VMEM((tb, GDN_HEADS * GDN_DK), F32),
                        pltpu.VMEM((tb, GDN_HEADS * GDN_DK), F32),
                        pltpu.VMEM((tb, GDN_HEADS * GDN_DV), F32),
                        pltpu.VMEM((tb, LANES), F32),
                        pltpu.VMEM((tb, LANES), F32),
                        pltpu.VMEM((GDN_HEADS, GDN_DK, GDN_DV), F32)],
        compiler_params=_cparams(("arbitrary", "arbitrary")),
        name="gated_deltanet",
    )(h, conv_w, alog, dtb, norm_g)


def _gla_levels(q, k, la, b):
    c = CHUNK
    ri = lax.broadcasted_iota(jnp.int32, (c, LANES), 0)
    r2, c2 = _iota2((2 * c, c))
    r2 = r2 & (c - 1)
    scores = jnp.where(r2 == c2, _mm_nt(_stack_heads(q), k), 0.0)
    s = c // 2
    while s >= 1:
        row_side = ((ri >> (s.bit_length() - 1)) & 1) == 1
        if s >= 4:
            ref = jnp.concatenate(
                [jnp.broadcast_to(b[g * 2 * s + s - 1:g * 2 * s + s, :], (2 * s, LANES)) for g in range(c // (2 * s))],
                axis=0)
            arg = -jnp.abs(b - ref)
        elif s == 2:
            m = ri & 3
            up = pltpu.roll(la, c - 1, 0)
            dn = pltpu.roll(la, 1, 0)
            arg = jnp.where(m == 0, up, jnp.where(m == 1, 0.0, jnp.where(m == 2, la, la + dn)))
        else:
            arg = jnp.where(row_side, la, 0.0)
        e = jnp.exp(arg)
        qt = jnp.where(row_side, q * e, 0.0)
        kt = jnp.where(row_side, 0.0, k * e)
        scores = scores + jnp.where(_sibling_mask(r2, c2, s), _mm_nt(_stack_heads(qt), kt), 0.0)
        s //= 2
    return scores


def _gla_body(h_ref, w2_ref, gkb_ref, ng_ref, o_ref, la_s, st_ref, *, tb):
    t_idx = pl.program_id(1)

    @pl.when(t_idx == 0)
    def _():
        st_ref[...] = jnp.zeros_like(st_ref)

    hk = GLA_HEADS * GLA_DK
    hv = GLA_HEADS * GLA_DV
    z = _mm_f32(h_ref[:, 2 * hk + 2 * hv:GLA_PAD], w2_ref[...]) + gkb_ref[...]
    la_s[...] = (jnp.minimum(z, 0.0) - jnp.log(1.0 + jnp.exp(-jnp.abs(z)))) * (1.0 / GLA_GATE_NORM)
    ng = ng_ref[...]

    def chunk(c, carry):
        r0 = pl.multiple_of(c * CHUNK, CHUNK)
        rows = pl.ds(r0, CHUNK)
        tril = _tril_f32(CHUNK)
        row_first = lax.broadcasted_iota(jnp.int32, (LANES, GLA_DV), 0) < GLA_DK
        for pair in range(GLA_HEADS // 2):
            lanes = slice(pair * LANES, (pair + 1) * LANES)
            q = h_ref[rows, lanes] * (GLA_DK ** -0.5)
            k = h_ref[rows, hk + pair * LANES:hk + (pair + 1) * LANES]
            la = la_s[rows, lanes]
            b = _mm_f32(tril, la)
            scores = _gla_levels(q, k, la, b)
            st = st_ref[pair]
            inter = _mm(_stack_heads(q * jnp.exp(b)), st)
            bt = b.T
            b_end = bt[:, CHUNK - 1:CHUNK]
            k_hat_t = k.T * jnp.exp(b_end - bt)
            vpair = h_ref[rows, 2 * hk + 2 * pair * GLA_DV:2 * hk + (2 * pair + 2) * GLA_DV]
            upd = _mm(k_hat_t, vpair)
            st_ref[pair] = st * jnp.exp(b_end) + jnp.where(row_first, upd[:, 0:GLA_DV], upd[:, GLA_DV:2 * GLA_DV])
            for j in range(2):
                h = 2 * pair + j
                sl = slice(j * CHUNK, (j + 1) * CHUNK)
                o = _mm(scores[sl], vpair[:, j * GLA_DV:(j + 1) * GLA_DV]) + inter[sl]
                o = o * lax.rsqrt(jnp.mean(o * o, -1, keepdims=True) + HEAD_NORM_EPS) * ng
                gate = h_ref[rows, 2 * hk + hv + h * GLA_DV:2 * hk + hv + (h + 1) * GLA_DV]
                o_ref[rows, h * GLA_DV:(h + 1) * GLA_DV] = (o * gate * _sigmoid(gate)).astype(o_ref.dtype)
        return carry

    lax.fori_loop(0, tb // CHUNK, chunk, 0)


def _gla_call(h, w2, gkb, norm_g, bsz, seq, tb):
    nt = seq // tb
    return pl.pallas_call(
        functools.partial(_gla_body, tb=tb),
        grid=(bsz, nt),
        in_specs=[pl.BlockSpec((tb, GLA_PAD), lambda b, t: (b * nt + t, 0)),
                  _const_spec((LANES, GLA_HEADS * GLA_DK)), _const_spec((1, GLA_HEADS * GLA_DK)),
                  _const_spec((1, GLA_DV))],
        out_specs=pl.BlockSpec((tb, GLA_HEADS * GLA_DV), lambda b, t: (b * nt + t, 0)),
        out_shape=jax.ShapeDtypeStruct((bsz * seq, GLA_HEADS * GLA_DV), BF16),
        scratch_shapes=[pltpu.VMEM((tb, GLA_HEADS * GLA_DK), F32),
                        pltpu.VMEM((GLA_HEADS // 2, 2 * GLA_DK, GLA_DV), F32)],
        compiler_params=_cparams(("arbitrary", "arbitrary")),
        name="gated_linear_attention",
    )(h, w2, gkb, norm_g)


def _seg_sum(x, ones_bd):
    hi = x.astype(BF16)
    lo = (x - hi.astype(F32)).astype(BF16)
    return (jnp.dot(hi, ones_bd, preferred_element_type=F32) + jnp.dot(lo, ones_bd, preferred_element_type=F32))


def _rw_body(h_ref, mu_ref, w0_ref, w2_ref, a0_ref, a2_ref, g2_ref, kk_ref, ka_ref, rk_ref, gng_ref, gnb_ref,
             o_ref, ext_ref, r_s, k_s, v_s, ld_s, an_s, bn_s, g_s, st_ref, *, tb):
    t_idx = pl.program_id(1)

    @pl.when(t_idx == 0)
    def _():
        ext_ref[0:8, :] = jnp.zeros((8, RW_PAD), F32)
        st_ref[...] = jnp.zeros_like(st_ref)

    @pl.when(t_idx != 0)
    def _():
        ext_ref[0:8, :] = ext_ref[tb:tb + 8, :]

    ext_ref[8:tb + 8, :] = h_ref[...]

    def shifted(cols):
        cur = ext_ref[8:tb + 8, cols]
        prev = ext_ref[7:tb + 7, cols]
        return cur + (prev - cur) * mu_ref[:, cols]

    w = RW_WIDTH
    r2, c2 = _iota2((LANES, LANES))
    ones_bd = jnp.where((r2 >> 6) == (c2 >> 6), 1.0, 0.0).astype(BF16)

    lo = shifted(slice(3 * w, 3 * w + LANES))
    glo = shifted(slice(3 * w + LANES, RW_PAD))
    wv = -_softplus(-(w0_ref[...] + _mm_f32(jnp.tanh(lo), w2_ref[...]))) - 0.5
    ld_s[...] = -jnp.exp(wv)
    a = _sigmoid(a0_ref[...] + _mm_f32(lo, a2_ref[...]))
    g_s[...] = _mm_f32(_sigmoid(glo), g2_ref[...])
    r_s[...] = shifted(slice(0, w))
    v_s[...] = shifted(slice(2 * w, 3 * w))
    k = shifted(slice(w, 2 * w))
    kk = k * kk_ref[...]
    for p in range(RW_HEADS // 2):
        lanes = slice(p * LANES, (p + 1) * LANES)
        t = kk[:, lanes]
        kkn = t * lax.rsqrt(_seg_sum(t * t, ones_bd) + L2_EPS)
        an_s[:, lanes] = -kkn
        bn_s[:, lanes] = kkn * a[:, lanes]
    k_s[...] = k * (1.0 + (a - 1.0) * ka_ref[...])

    n2 = 2 * CHUNK

    def chunk(c, carry):
        r0 = pl.multiple_of(c * CHUNK, CHUNK)
        rows = pl.ds(r0, CHUNK)
        tril = _tril_f32(CHUNK)
        strict = _block_causal(n2, strict=True)
        incl = _block_causal(n2, strict=False)
        for p in range(RW_HEADS // 2):
            lanes = slice(p * LANES, (p + 1) * LANES)
            r = r_s[rows, lanes]
            kx = k_s[rows, lanes]
            v = v_s[rows, lanes]
            ld = ld_s[rows, lanes]
            an = an_s[rows, lanes]
            bn = bn_s[rows, lanes]
            ci = _mm_f32(tril, ld)
            c_end = ci[CHUNK - 1:CHUNK, :]
            e_in = jnp.exp(-ci)
            e_out = jnp.exp(c_end - ci)
            a_s = _stack_heads(an * jnp.exp(ci - ld))
            r_t = _stack_heads(r * jnp.exp(ci))
            b_hat = _stack_heads(bn * e_out)
            k_hat = _stack_heads(kx * e_out)
            v_st = _stack_heads(v)
            gram = _mm_nt(jnp.concatenate([a_s, r_t], axis=0),
                          jnp.concatenate([_stack_heads(bn * e_in), _stack_heads(kx * e_in)], axis=0))
            a_ab = jnp.where(strict, gram[0:n2, 0:n2], 0.0)
            a_ak = jnp.where(strict, gram[0:n2, n2:2 * n2], 0.0)
            a_rb = jnp.where(incl, gram[n2:2 * n2, 0:n2], 0.0)
            a_rk = jnp.where(incl, gram[n2:2 * n2, n2:2 * n2], 0.0)
            t_inv = _unit_lower_inverse(-a_ab, n2)
            x = _mm(t_inv, jnp.concatenate([a_s, _mm(a_ak, v_st)], axis=1))
            a_p, u0 = x[:, 0:LANES], x[:, LANES:2 * LANES]
            zero = jnp.zeros((n2, LANES), F32)
            z = _mm(jnp.concatenate([a_rb, a_rk], axis=1),
                    jnp.concatenate([x, jnp.concatenate([zero, v_st], axis=1)], axis=0))
            r_p = r_t + z[:, 0:LANES]
            bt = b_hat.T
            wmat = _mm(bt, a_p)
            n0 = _mm(jnp.concatenate([bt, k_hat.T], axis=1), jnp.concatenate([u0, v_st], axis=0))
            st = st_ref[p]
            ys = _mm(r_p, st) + z[:, LANES:2 * LANES]
            y = ys[0:CHUNK] + ys[CHUNK:n2]
            g_col = jnp.exp(ci.T[:, CHUNK - 1:CHUNK])
            st_ref[p] = g_col * st + _mm(wmat, st) + n0
            mean = _seg_sum(y, ones_bd) * (1.0 / RW_HEAD)
            d = y - mean
            var = _seg_sum(d * d, ones_bd) * (1.0 / RW_HEAD)
            yn = d * lax.rsqrt(var + RW_GN_EPS) * gng_ref[:, lanes] + gnb_ref[:, lanes]
            bonus = _seg_sum(r * kx * rk_ref[:, lanes], ones_bd) * v
            o_ref[rows, lanes] = ((yn + bonus) * g_s[rows, lanes]).astype(o_ref.dtype)
        return carry

    lax.fori_loop(0, tb // CHUNK, chunk, 0)


def _rw_call(h, mu, w0, w2, a0, a2, g2, k_k, k_a, r_k, gn_g, gn_b, bsz, seq, tb):
    nt = seq // tb
    vec = _const_spec((1, RW_WIDTH))
    return pl.pallas_call(
        functools.partial(_rw_body, tb=tb),
        grid=(bsz, nt),
        in_specs=[pl.BlockSpec((tb, RW_PAD), lambda b, t: (b * nt + t, 0)),
                  _const_spec((1, RW_PAD)), vec, _const_spec((LANES, RW_WIDTH)), vec,
                  _const_spec((LANES, RW_WIDTH)), _const_spec((2 * LANES, RW_WIDTH)),
                  vec, vec, vec, vec, vec],
        out_specs=pl.BlockSpec((tb, RW_WIDTH), lambda b, t: (b * nt + t, 0)),
        out_shape=jax.ShapeDtypeStruct((bsz * seq, RW_WIDTH), BF16),
        scratch_shapes=[pltpu.VMEM((tb + 8, RW_PAD), F32)] + [pltpu.VMEM((tb, RW_WIDTH), F32)] * 7
        + [pltpu.VMEM((RW_HEADS // 2, LANES, LANES), F32)],
        compiler_params=_cparams(("arbitrary", "arbitrary")),
        name="rwkv7_time_mix",
    )(h, mu, w0, w2, a0, a2, g2, k_k, k_a, r_k, gn_g, gn_b)


def _pad_to(a, size, axis):
    pad = [(0, 0)] * a.ndim
    pad[axis] = (0, size - a.shape[axis])
    return jnp.pad(a, pad)


def _mixer_in_weights(w_in):
    o = 0
    rw = w_in[:, o:o + RW_IN]
    o += RW_IN
    gla = w_in[:, o:o + GLA_IN]
    o += GLA_IN
    gdn = w_in[:, o:o + GDN_IN]
    o += GDN_IN
    gate = w_in[:, o:]
    qkv_w = 2 * GLA_HEADS * GLA_DK + GLA_HEADS * GLA_DV
    gla_p = jnp.concatenate([gla[:, 0:qkv_w], gla[:, qkv_w + GLA_GATE_LORA:], gla[:, qkv_w:qkv_w + GLA_GATE_LORA]], axis=1)
    gdn_p = jnp.concatenate([gdn[:, 0:GDN_QKV], gdn[:, GDN_QKV + 2 * GDN_HEADS:], gdn[:, GDN_QKV:GDN_QKV + 2 * GDN_HEADS]], axis=1)
    mix = jnp.concatenate([_pad_to(rw, RW_PAD, 1), _pad_to(gla_p, GLA_PAD, 1), _pad_to(gdn_p, GDN_PAD, 1)], axis=1)
    return mix.astype(BF16), gate.astype(BF16)


def _row(v):
    return v.reshape(1, -1)


def kernel(x, p, ln_g, ln_b, ffn_w1, ffn_w3, ffn_w2, w_in, rw_mu, rw_w0, rw_w2, rw_a0, rw_a2, rw_g2, rw_k_k, rw_k_a, rw_r_k, rw_gn_g, rw_gn_b, gla_gk_w2, gla_gk_b, gla_norm_g, gdn_conv_w, gdn_a_log, gdn_dt_bias, gdn_norm_g, w_branch, w_o, ple_w_gate, ple_w_proj):
    bsz, seq, d = x.shape
    depth = p.shape[0]
    n = bsz * seq
    tm = min(512, n)
    tb = min(512, seq)
    x = x.reshape(n, d)
    for i in range(depth):
        w1 = ffn_w1[i].astype(BF16)
        w3 = ffn_w3[i].astype(BF16)
        w2 = ffn_w2[i].astype(BF16)
        x = _ffn_call(x, w1[0], w3[0], w2[0], _row(ln_g[i, 0]), _row(ln_b[i, 0]), tm)

        w_mix, w_gate = _mixer_in_weights(w_in[i])
        h_rw, h_gla, h_gdn = _proj_call(x, w_mix, min(256, n))
        lora = jnp.zeros((LANES, RW_WIDTH), F32)
        o_rw = _rw_call(
            h_rw, _row(_pad_to(rw_mu[i], RW_PAD, 0)), _row(rw_w0[i]),
            lora.at[0:RW_DECAY_LORA].set(rw_w2[i]), _row(rw_a0[i]),
            lora.at[RW_DECAY_LORA:RW_DECAY_LORA + RW_AAA_LORA].set(rw_a2[i]),
            _pad_to(rw_g2[i], 2 * LANES, 0), _row(rw_k_k[i]), _row(rw_k_a[i]), _row(rw_r_k[i]),
            _row(rw_gn_g[i]), _row(rw_gn_b[i]), bsz, seq, tb)
        o_gla = _gla_call(h_gla, _pad_to(gla_gk_w2[i], LANES, 0), _row(gla_gk_b[i]), _row(gla_norm_g[i]), bsz, seq, tb)
        o_gdn = _gdn_call(h_gdn, gdn_conv_w[i], _row(_pad_to(gdn_a_log[i], LANES, 0)),
                          _row(_pad_to(gdn_dt_bias[i], LANES, 0)), _row(gdn_norm_g[i]), bsz, seq, tb)
        x = _merge_call(x, o_rw, o_gla, o_gdn, w_gate, w_branch[i].astype(BF16), w_o[i].astype(BF16),
                        _row(ln_g[i, 1]), _row(ln_b[i, 1]), tm)

        x = _ffn_ple_call(x, p[i].reshape(n, PLE_DIM), w1[1], w3[1], w2[1], _row(ln_g[i, 2]), _row(ln_b[i, 2]),
                          ple_w_gate[i].astype(BF16), ple_w_proj[i].astype(BF16),
                          _row(ln_g[i, 3]), _row(ln_b[i, 3]), tm)
    return x.reshape(bsz, seq, d)
```

```python
import functools

import jax
import jax.numpy as jnp
from jax import lax
from jax.experimental import pallas as pl
from jax.experimental.pallas import tpu as pltpu

F32 = jnp.float32
BF16 = jnp.bfloat16

D_MODEL = 1024
DEPTH = 2
PLE_DIM = 256
D_FF = 2816
LN_EPS = 1e-5
DEEPNORM_ALPHA = (2 * DEPTH) ** 0.25
MACARON_W = 0.5

RW_HEADS = 8
RW_HEAD = 64
RW_WIDTH = RW_HEADS * RW_HEAD
RW_DECAY_LORA = 64
RW_AAA_LORA = 64
RW_GATE_LORA = 160
RW_GN_EPS = 1e-5 * RW_HEAD
RW_IN = 3 * RW_WIDTH + RW_DECAY_LORA + RW_AAA_LORA + RW_GATE_LORA

GLA_HEADS = 4
GLA_DK = 64
GLA_DV = 128
GLA_GATE_LORA = 16
GLA_GATE_NORM = 16.0
GLA_IN = 2 * GLA_HEADS * GLA_DK + 2 * GLA_HEADS * GLA_DV + GLA_GATE_LORA

GDN_HEADS = 4
GDN_DK = 128
GDN_DV = 128
GDN_CONV = 4
GDN_QKV = GDN_HEADS * (2 * GDN_DK + GDN_DV)
GDN_IN = GDN_QKV + 2 * GDN_HEADS + GDN_HEADS * GDN_DV

HEAD_NORM_EPS = 1e-6
L2_EPS = 1e-6
N_BRANCH = 3
BRANCH_WIDTH = 512

LANES = 128
CHUNK = 64
GDN_CHUNKS_PER_STEP = 2
RW_CHUNKS_PER_STEP = 2
RW_PAD = 1920
GLA_PAD = 1664
GDN_PAD = 2176
MIX_PAD = RW_PAD + GLA_PAD + GDN_PAD
VMEM_LIMIT = 56 * 1024 * 1024


def _cparams(sem):
    return pltpu.CompilerParams(dimension_semantics=sem, vmem_limit_bytes=VMEM_LIMIT)


def _const_spec(shape):
    nd = len(shape)
    return pl.BlockSpec(shape, lambda *_: (0,) * nd, pipeline_mode=pl.Buffered(1))


def _mm(a, b):
    return jnp.dot(a.astype(BF16), b.astype(BF16), preferred_element_type=F32)


def _mm_nt(a, b):
    return lax.dot_general(a.astype(BF16), b.astype(BF16), (((1,), (1,)), ((), ())),
                           preferred_element_type=F32)


def _mm_f32(a, b):
    return jnp.dot(a, b, preferred_element_type=F32, precision=lax.Precision.HIGHEST)


def _layer_norm(y, g, b):
    m = jnp.mean(y, -1, keepdims=True)
    d = y - m
    var = jnp.mean(d * d, -1, keepdims=True)
    return d * lax.rsqrt(var + LN_EPS) * g + b


def _sigmoid(x):
    return 1.0 / (1.0 + jnp.exp(-x))


def _softplus(x):
    return jnp.maximum(x, 0.0) + jnp.log(1.0 + jnp.exp(-jnp.abs(x)))


def _iota2(shape):
    return (lax.broadcasted_iota(jnp.int32, shape, 0), lax.broadcasted_iota(jnp.int32, shape, 1))


def _tril_f32(n):
    r, c = _iota2((n, n))
    return jnp.where(r >= c, 1.0, 0.0).astype(F32)


def _sibling_mask(r, c, s):
    k = s.bit_length() - 1
    return ((r >> (k + 1)) == (c >> (k + 1))) & (((r >> k) & 1) == 1) & (((c >> k) & 1) == 0)


def _unit_lower_inverses(lows, n):
    r, c = _iota2((n, n))
    eye = jnp.where(r == c, 1.0, 0.0).astype(F32)
    first = _sibling_mask(r, c, 1)
    xs = [eye - jnp.where(first, low, 0.0) for low in lows]
    s = 2
    while s < CHUNK:
        sib = _sibling_mask(r, c, s)
        ts = [_mm(jnp.where(sib, low, 0.0), x) for low, x in zip(lows, xs)]
        us = [_mm(x, t) for x, t in zip(xs, ts)]
        xs = [x - u for x, u in zip(xs, us)]
        s *= 2
    return xs


def _stack_heads(z):
    lane = lax.broadcasted_iota(jnp.int32, z.shape, 1)
    first = lane < (LANES // 2)
    return jnp.concatenate([jnp.where(first, z, 0.0), jnp.where(first, 0.0, z)], axis=0)


def _block_causal(n, strict):
    r, c = _iota2((n, n))
    same = (r >> 6) == (c >> 6)
    return same & ((r > c) if strict else (r >= c))


FF_CHUNK = 1408


def _swiglu_ln(x, w1_ref, w3_ref, w2_ref, g, b):
    xb = x.astype(BF16)
    acc = None
    for c0 in range(0, D_FF, FF_CHUNK):
        h1 = jnp.dot(xb, w1_ref[:, c0:c0 + FF_CHUNK], preferred_element_type=F32)
        h3 = jnp.dot(xb, w3_ref[:, c0:c0 + FF_CHUNK], preferred_element_type=F32)
        act = (h1 * _sigmoid(h1) * h3).astype(BF16)
        part = jnp.dot(act, w2_ref[c0:c0 + FF_CHUNK, :], preferred_element_type=F32)
        acc = part if acc is None else acc + part
    return _layer_norm(DEEPNORM_ALPHA * x + MACARON_W * acc, g, b)


def _ffn_body(x_ref, w1_ref, w3_ref, w2_ref, g_ref, b_ref, o_ref):
    o_ref[...] = _swiglu_ln(x_ref[...], w1_ref, w3_ref, w2_ref, g_ref[...], b_ref[...])


def _ffn_ple_body(x_ref, p_ref, w1_ref, w3_ref, w2_ref, g_ref, b_ref, wg_ref, wp_ref, g2_ref, b2_ref, o_ref):
    x = _swiglu_ln(x_ref[...], w1_ref, w3_ref, w2_ref, g_ref[...], b_ref[...])
    gate = _sigmoid(jnp.dot(x.astype(BF16), wg_ref[...], preferred_element_type=F32))
    emb = jnp.dot(p_ref[...].astype(BF16), wp_ref[...], preferred_element_type=F32)
    o_ref[...] = _layer_norm(DEEPNORM_ALPHA * x + gate * emb, g2_ref[...], b2_ref[...])


def _ffn_call(x, w1, w3, w2, g, b, tm):
    n = x.shape[0]
    row = pl.BlockSpec((tm, D_MODEL), lambda i: (i, 0))
    return pl.pallas_call(
        _ffn_body,
        grid=(n // tm,),
        in_specs=[row, _const_spec((D_MODEL, D_FF)), _const_spec((D_MODEL, D_FF)), _const_spec((D_FF, D_MODEL)),
                  _const_spec((1, D_MODEL)), _const_spec((1, D_MODEL))],
        out_specs=row,
        out_shape=jax.ShapeDtypeStruct((n, D_MODEL), F32),
        compiler_params=_cparams(("arbitrary",)),
        name="ffn_ln",
    )(x, w1, w3, w2, g, b)


def _ffn_ple_call(x, p, w1, w3, w2, g, b, wg, wp, g2, b2, tm):
    n = x.shape[0]
    row = pl.BlockSpec((tm, D_MODEL), lambda i: (i, 0))
    return pl.pallas_call(
        _ffn_ple_body,
        grid=(n // tm,),
        in_specs=[row, pl.BlockSpec((tm, PLE_DIM), lambda i: (i, 0)),
                  _const_spec((D_MODEL, D_FF)), _const_spec((D_MODEL, D_FF)), _const_spec((D_FF, D_MODEL)),
                  _const_spec((1, D_MODEL)), _const_spec((1, D_MODEL)),
                  _const_spec((D_MODEL, D_MODEL)), _const_spec((PLE_DIM, D_MODEL)),
                  _const_spec((1, D_MODEL)), _const_spec((1, D_MODEL))],
        out_specs=row,
        out_shape=jax.ShapeDtypeStruct((n, D_MODEL), F32),
        compiler_params=_cparams(("arbitrary",)),
        name="ffn_ple_ln",
    )(x, p, w1, w3, w2, g, b, wg, wp, g2, b2)


def _proj_body(x_ref, w_ref, rw_ref, gla_ref, gdn_ref):
    xb = x_ref[...].astype(BF16)
    rw_ref[...] = jnp.dot(xb, w_ref[:, 0:RW_PAD], preferred_element_type=F32)
    gla_ref[...] = jnp.dot(xb, w_ref[:, RW_PAD:RW_PAD + GLA_PAD], preferred_element_type=F32)
    gdn_ref[...] = jnp.dot(xb, w_ref[:, RW_PAD + GLA_PAD:MIX_PAD], preferred_element_type=F32)


def _proj_call(x, w, tm):
    n = x.shape[0]
    return pl.pallas_call(
        _proj_body,
        grid=(n // tm,),
        in_specs=[pl.BlockSpec((tm, D_MODEL), lambda i: (i, 0)), _const_spec((D_MODEL, MIX_PAD))],
        out_specs=[pl.BlockSpec((tm, RW_PAD), lambda i: (i, 0)),
                   pl.BlockSpec((tm, GLA_PAD), lambda i: (i, 0)),
                   pl.BlockSpec((tm, GDN_PAD), lambda i: (i, 0))],
        out_shape=[jax.ShapeDtypeStruct((n, RW_PAD), F32),
                   jax.ShapeDtypeStruct((n, GLA_PAD), F32),
                   jax.ShapeDtypeStruct((n, GDN_PAD), F32)],
        compiler_params=_cparams(("arbitrary",)),
        name="mixer_in_proj",
    )(x, w)


def _merge_body(x_ref, orw_ref, ogla_ref, ogdn_ref, wgate_ref, wbr_ref, wo_ref, g_ref, b_ref, o_ref):
    x = x_ref[...]
    xb = x.astype(BF16)
    merged = None
    for n, br_ref in enumerate((orw_ref, ogla_ref, ogdn_ref)):
        gate = _sigmoid(jnp.dot(xb, wgate_ref[:, n * D_MODEL:(n + 1) * D_MODEL], preferred_element_type=F32))
        term = gate * jnp.dot(br_ref[...], wbr_ref[n], preferred_element_type=F32)
        merged = term if merged is None else merged + term
    mix = jnp.dot(merged.astype(BF16), wo_ref[...], preferred_element_type=F32)
    o_ref[...] = _layer_norm(DEEPNORM_ALPHA * x + mix, g_ref[...], b_ref[...])


def _merge_call(x, o_rw, o_gla, o_gdn, wgate, wbr, wo, g, b, tm):
    n = x.shape[0]
    row = pl.BlockSpec((tm, D_MODEL), lambda i: (i, 0))
    br = pl.BlockSpec((tm, BRANCH_WIDTH), lambda i: (i, 0))
    return pl.pallas_call(
        _merge_body,
        grid=(n // tm,),
        in_specs=[row, br, br, br, _const_spec((D_MODEL, N_BRANCH * D_MODEL)),
                  _const_spec((N_BRANCH, BRANCH_WIDTH, D_MODEL)), _const_spec((D_MODEL, D_MODEL)),
                  _const_spec((1, D_MODEL)), _const_spec((1, D_MODEL))],
        out_specs=row,
        out_shape=jax.ShapeDtypeStruct((n, D_MODEL), F32),
        compiler_params=_cparams(("arbitrary",)),
        name="merge_ln",
    )(x, o_rw, o_gla, o_gdn, wgate, wbr, wo, g, b)


def _gdn_body(h_ref, cw_ref, alog_ref, dtb_ref, ng_ref, o_ref,
              ext_ref, q_s, k_s, v_s, g_s, beta_s, st_ref, *, tb):
    t_idx = pl.program_id(1)

    @pl.when(t_idx == 0)
    def _():
        ext_ref[0:8, :] = jnp.zeros((8, GDN_QKV), F32)
        st_ref[...] = jnp.zeros_like(st_ref)

    @pl.when(t_idx != 0)
    def _():
        ext_ref[0:8, :] = ext_ref[tb:tb + 8, :]

    ext_ref[8:tb + 8, :] = h_ref[:, 0:GDN_QKV]

    hw = GDN_HEADS * GDN_DK
    for grp, dst in enumerate((q_s, k_s, v_s)):
        cols = slice(grp * hw, (grp + 1) * hw)
        acc = ext_ref[8:tb + 8, cols] * cw_ref[GDN_CONV - 1:GDN_CONV, cols]
        for d in range(1, GDN_CONV):
            acc = acc + ext_ref[8 - d:8 - d + tb, cols] * cw_ref[GDN_CONV - 1 - d:GDN_CONV - d, cols]
        act = acc * _sigmoid(acc)
        if grp == 2:
            dst[...] = act
        else:
            scale = GDN_DK ** -0.5 if grp == 0 else 1.0
            for h in range(GDN_HEADS):
                t = act[:, h * GDN_DK:(h + 1) * GDN_DK]
                dst[:, h * GDN_DK:(h + 1) * GDN_DK] = t * (lax.rsqrt(jnp.sum(t * t, -1, keepdims=True) + L2_EPS) * scale)

    ab = h_ref[:, GDN_QKV + GDN_HEADS * GDN_DV:GDN_PAD]
    g_s[...] = -jnp.exp(alog_ref[...]) * _softplus(ab + dtb_ref[...])
    beta_s[...] = _sigmoid(ab)

    n2 = 2 * CHUNK
    ng = ng_ref[...]

    cpi = GDN_CHUNKS_PER_STEP
    jobs = [(j, pr) for j in range(cpi) for pr in range(GDN_HEADS // 2)]

    def pair_rows(src, rows, pr, width):
        return jnp.concatenate([src[rows, h * width:(h + 1) * width] for h in (2 * pr, 2 * pr + 1)], axis=0)

    def pair_col(a, pr, off):
        return jnp.concatenate([a[:, off + h:off + h + 1] for h in (2 * pr, 2 * pr + 1)], axis=0)

    def step(it, carry):
        rows = [pl.ds(pl.multiple_of(it * (cpi * CHUNK) + j * CHUNK, CHUNK), CHUNK) for j in range(cpi)]
        tril = _tril_f32(CHUNK)
        causal = _block_causal(n2, strict=False)
        strict = _block_causal(n2, strict=True)
        gcs = [_mm_f32(tril, g_s[rw, :]) for rw in rows]
        betas = [beta_s[rw, :] for rw in rows]
        gi = [pair_col(gcs[j], pr, 0) for j, pr in jobs]
        bcol = [pair_col(betas[j], pr, GDN_HEADS) for j, pr in jobs]
        qs = [pair_rows(q_s, rows[j], pr, GDN_DK) for j, pr in jobs]
        ks = [pair_rows(k_s, rows[j], pr, GDN_DK) for j, pr in jobs]
        vs = [pair_rows(v_s, rows[j], pr, GDN_DV) for j, pr in jobs]
        dec = []
        for g in gi:
            gi_b = jnp.broadcast_to(g, (n2, n2))
            diff = gi_b - gi_b.T
            dec.append(jnp.where(causal, jnp.exp(jnp.where(causal, diff, 0.0)), 0.0))
        kb = [k * b for k, b in zip(ks, bcol)]
        gram = [_mm_nt(jnp.concatenate([a, q], axis=0), k) for a, q, k in zip(kb, qs, ks)]
        low = [jnp.where(strict, g[0:n2] * d, 0.0) for g, d in zip(gram, dec)]
        att = [g[n2:2 * n2] * d for g, d in zip(gram, dec)]
        t_inv = _unit_lower_inverses(low, n2)
        eg = [jnp.exp(g) for g in gi]
        wu = [_mm(t, jnp.concatenate([a * e, v * b], axis=1))
              for t, a, e, v, b in zip(t_inv, kb, eg, vs, bcol)]
        aw = [_mm(a, x) for a, x in zip(att, wu)]
        per_head = {}
        for idx, (j, pr) in enumerate(jobs):
            for half, h in enumerate((2 * pr, 2 * pr + 1)):
                sl = slice(half * CHUNK, (half + 1) * CHUNK)
                g_end = gcs[j][CHUNK - 1:CHUNK, h:h + 1]
                k_hat = ks[idx][sl] * jnp.exp(g_end - gi[idx][sl])
                kwu = _mm(k_hat.T, wu[idx][sl])
                q_eff = qs[idx][sl] * eg[idx][sl] - aw[idx][sl, 0:GDN_DV]
                per_head[(j, h)] = (q_eff, aw[idx][sl, GDN_DV:2 * GDN_DV], jnp.exp(g_end), kwu)
        sts = [st_ref[h] for h in range(GDN_HEADS)]
        for j in range(cpi):
            for h in range(GDN_HEADS):
                q_eff, o_loc, g_dec, kwu = per_head[(j, h)]
                st = sts[h]
                o = _mm(q_eff, st) + o_loc
                sts[h] = g_dec * st - _mm(kwu[:, 0:GDN_DV], st) + kwu[:, GDN_DV:2 * GDN_DV]
                o = o * lax.rsqrt(jnp.mean(o * o, -1, keepdims=True) + HEAD_NORM_EPS) * ng
                gate = h_ref[rows[j], GDN_QKV + h * GDN_DV:GDN_QKV + (h + 1) * GDN_DV]
                o_ref[rows[j], h * GDN_DV:(h + 1) * GDN_DV] = (o * gate * _sigmoid(gate)).astype(o_ref.dtype)
        for h in range(GDN_HEADS):
            st_ref[h] = sts[h]
        return carry

    lax.fori_loop(0, tb // (cpi * CHUNK), step, 0)


def _gdn_call(h, conv_w, alog, dtb, norm_g, bsz, seq, tb):
    nt = seq // tb
    return pl.pallas_call(
        functools.partial(_gdn_body, tb=tb),
        grid=(bsz, nt),
        in_specs=[pl.BlockSpec((tb, GDN_PAD), lambda b, t: (b * nt + t, 0)),
                  _const_spec((GDN_CONV, GDN_QKV)), _const_spec((1, LANES)), _const_spec((1, LANES)),
                  _const_spec((1, GDN_DV))],
        out_specs=pl.BlockSpec((tb, GDN_HEADS * GDN_DV), lambda b, t: (b * nt + t, 0)),
        out_shape=jax.ShapeDtypeStruct((bsz * seq, GDN_HEADS * GDN_DV), BF16),
        scratch_shapes=[pltpu.VMEM((tb + 8, GDN_QKV), F32),
                        pltpu.VMEM((tb, GDN_HEADS * GDN_DK), F32),
                        pltpu.VMEM((tb, GDN_HEADS * GDN_DK), F32),
                        pltpu.VMEM((tb, GDN_HEADS * GDN_DV), F32),
                        pltpu.VMEM((tb, LANES), F32),
                        pltpu.VMEM((tb, LANES), F32),
                        pltpu.VMEM((GDN_HEADS, GDN_DK, GDN_DV), F32)],
        compiler_params=_cparams(("arbitrary", "arbitrary")),
        name="gated_deltanet",
    )(h, conv_w, alog, dtb, norm_g)


def _gla_levels(q, k, la, b):
    c = CHUNK
    ri = lax.broadcasted_iota(jnp.int32, (c, LANES), 0)
    r2, c2 = _iota2((2 * c, c))
    r2 = r2 & (c - 1)
    scores = jnp.where(r2 == c2, _mm_nt(_stack_heads(q), k), 0.0)
    s = c // 2
    while s >= 1:
        row_side = ((ri >> (s.bit_length() - 1)) & 1) == 1
        if s >= 4:
            ref = jnp.concatenate(
                [jnp.broadcast_to(b[g * 2 * s + s - 1:g * 2 * s + s, :], (2 * s, LANES)) for g in range(c // (2 * s))],
                axis=0)
            arg = -jnp.abs(b - ref)
        elif s == 2:
            m = ri & 3
            up = pltpu.roll(la, c - 1, 0)
            dn = pltpu.roll(la, 1, 0)
            arg = jnp.where(m == 0, up, jnp.where(m == 1, 0.0, jnp.where(m == 2, la, la + dn)))
        else:
            arg = jnp.where(row_side, la, 0.0)
        e = jnp.exp(arg)
        qt = jnp.where(row_side, q * e, 0.0)
        kt = jnp.where(row_side, 0.0, k * e)
        scores = scores + jnp.where(_sibling_mask(r2, c2, s), _mm_nt(_stack_heads(qt), kt), 0.0)
        s //= 2
    return scores


def _gla_body(h_ref, w2_ref, gkb_ref, ng_ref, o_ref, la_s, st_ref, *, tb):
    t_idx = pl.program_id(1)

    @pl.when(t_idx == 0)
    def _():
        st_ref[...] = jnp.zeros_like(st_ref)

    hk = GLA_HEADS * GLA_DK
    hv = GLA_HEADS * GLA_DV
    z = _mm_f32(h_ref[:, 2 * hk + 2 * hv:GLA_PAD], w2_ref[...]) + gkb_ref[...]
    la_s[...] = (jnp.minimum(z, 0.0) - jnp.log(1.0 + jnp.exp(-jnp.abs(z)))) * (1.0 / GLA_GATE_NORM)
    ng = ng_ref[...]

    def chunk(c, carry):
        r0 = pl.multiple_of(c * CHUNK, CHUNK)
        rows = pl.ds(r0, CHUNK)
        tril = _tril_f32(CHUNK)
        row_first = lax.broadcasted_iota(jnp.int32, (LANES, GLA_DV), 0) < GLA_DK
        for pair in range(GLA_HEADS // 2):
            lanes = slice(pair * LANES, (pair + 1) * LANES)
            q = h_ref[rows, lanes] * (GLA_DK ** -0.5)
            k = h_ref[rows, hk + pair * LANES:hk + (pair + 1) * LANES]
            la = la_s[rows, lanes]
            b = _mm_f32(tril, la)
            scores = _gla_levels(q, k, la, b)
            st = st_ref[pair]
            inter = _mm(_stack_heads(q * jnp.exp(b)), st)
            bt = b.T
            b_end = bt[:, CHUNK - 1:CHUNK]
            k_hat_t = k.T * jnp.exp(b_end - bt)
            vpair = h_ref[rows, 2 * hk + 2 * pair * GLA_DV:2 * hk + (2 * pair + 2) * GLA_DV]
            upd = _mm(k_hat_t, vpair)
            st_ref[pair] = st * jnp.exp(b_end) + jnp.where(row_first, upd[:, 0:GLA_DV], upd[:, GLA_DV:2 * GLA_DV])
            for j in range(2):
                h = 2 * pair + j
                sl = slice(j * CHUNK, (j + 1) * CHUNK)
                o = _mm(scores[sl], vpair[:, j * GLA_DV:(j + 1) * GLA_DV]) + inter[sl]
                o = o * lax.rsqrt(jnp.mean(o * o, -1, keepdims=True) + HEAD_NORM_EPS) * ng
                gate = h_ref[rows, 2 * hk + hv + h * GLA_DV:2 * hk + hv + (h + 1) * GLA_DV]
                o_ref[rows, h * GLA_DV:(h + 1) * GLA_DV] = (o * gate * _sigmoid(gate)).astype(o_ref.dtype)
        return carry

    lax.fori_loop(0, tb // CHUNK, chunk, 0)


def _gla_call(h, w2, gkb, norm_g, bsz, seq, tb):
    nt = seq // tb
    return pl.pallas_call(
        functools.partial(_gla_body, tb=tb),
        grid=(bsz, nt),
        in_specs=[pl.BlockSpec((tb, GLA_PAD), lambda b, t: (b * nt + t, 0)),
                  _const_spec((LANES, GLA_HEADS * GLA_DK)), _const_spec((1, GLA_HEADS * GLA_DK)),
                  _const_spec((1, GLA_DV))],
        out_specs=pl.BlockSpec((tb, GLA_HEADS * GLA_DV), lambda b, t: (b * nt + t, 0)),
        out_shape=jax.ShapeDtypeStruct((bsz * seq, GLA_HEADS * GLA_DV), BF16),
        scratch_shapes=[pltpu.VMEM((tb, GLA_HEADS * GLA_DK), F32),
                        pltpu.VMEM((GLA_HEADS // 2, 2 * GLA_DK, GLA_DV), F32)],
        compiler_params=_cparams(("arbitrary", "arbitrary")),
        name="gated_linear_attention",
    )(h, w2, gkb, norm_g)


def _seg_sum(x, ones_bd):
    hi = x.astype(BF16)
    lo = (x - hi.astype(F32)).astype(BF16)
    return (jnp.dot(hi, ones_bd, preferred_element_type=F32) + jnp.dot(lo, ones_bd, preferred_element_type=F32))


def _rw_body(h_ref, mu_ref, w0_ref, w2_ref, a0_ref, a2_ref, g2_ref, kk_ref, ka_ref, rk_ref, gng_ref, gnb_ref,
             o_ref, ext_ref, r_s, k_s, v_s, ld_s, an_s, bn_s, g_s, st_ref, *, tb):
    t_idx = pl.program_id(1)

    @pl.when(t_idx == 0)
    def _():
        ext_ref[0:8, :] = jnp.zeros((8, RW_PAD), F32)
        st_ref[...] = jnp.zeros_like(st_ref)

    @pl.when(t_idx != 0)
    def _():
        ext_ref[0:8, :] = ext_ref[tb:tb + 8, :]

    ext_ref[8:tb + 8, :] = h_ref[...]

    def shifted(cols):
        cur = ext_ref[8:tb + 8, cols]
        prev = ext_ref[7:tb + 7, cols]
        return cur + (prev - cur) * mu_ref[:, cols]

    w = RW_WIDTH
    r2, c2 = _iota2((LANES, LANES))
    ones_bd = jnp.where((r2 >> 6) == (c2 >> 6), 1.0, 0.0).astype(BF16)

    lo = shifted(slice(3 * w, 3 * w + LANES))
    glo = shifted(slice(3 * w + LANES, RW_PAD))
    wv = -_softplus(-(w0_ref[...] + _mm_f32(jnp.tanh(lo), w2_ref[...]))) - 0.5
    ld_s[...] = -jnp.exp(wv)
    a = _sigmoid(a0_ref[...] + _mm_f32(lo, a2_ref[...]))
    g_s[...] = _mm_f32(_sigmoid(glo), g2_ref[...])
    r_s[...] = shifted(slice(0, w))
    v_s[...] = shifted(slice(2 * w, 3 * w))
    k = shifted(slice(w, 2 * w))
    kk = k * kk_ref[...]
    for p in range(RW_HEADS // 2):
        lanes = slice(p * LANES, (p + 1) * LANES)
        t = kk[:, lanes]
        kkn = t * lax.rsqrt(_seg_sum(t * t, ones_bd) + L2_EPS)
        an_s[:, lanes] = -kkn
        bn_s[:, lanes] = kkn * a[:, lanes]
    k_s[...] = k * (1.0 + (a - 1.0) * ka_ref[...])

    n2 = 2 * CHUNK

    cpi = RW_CHUNKS_PER_STEP
    npair = RW_HEADS // 2
    jobs = [(j, p) for j in range(cpi) for p in range(npair)]

    def step(it, carry):
        rows = [pl.ds(pl.multiple_of(it * (cpi * CHUNK) + j * CHUNK, CHUNK), CHUNK) for j in range(cpi)]
        tril = _tril_f32(CHUNK)
        strict = _block_causal(n2, strict=True)
        incl = _block_causal(n2, strict=False)
        zero = jnp.zeros((n2, LANES), F32)

        def load(src):
            return [src[rows[j], p * LANES:(p + 1) * LANES] for j, p in jobs]

        r, kx, v, ld, an, bn = (load(s) for s in (r_s, k_s, v_s, ld_s, an_s, bn_s))
        ci = [_mm_f32(tril, t) for t in ld]
        e_in = [jnp.exp(-c) for c in ci]
        e_out = [jnp.exp(c[CHUNK - 1:CHUNK, :] - c) for c in ci]
        a_s = [_stack_heads(a * jnp.exp(c - t)) for a, c, t in zip(an, ci, ld)]
        r_t = [_stack_heads(x * jnp.exp(c)) for x, c in zip(r, ci)]
        b_hat = [_stack_heads(b * e) for b, e in zip(bn, e_out)]
        k_hat = [_stack_heads(k * e) for k, e in zip(kx, e_out)]
        v_st = [_stack_heads(x) for x in v]
        gram = [_mm_nt(jnp.concatenate([a, x], axis=0),
                       jnp.concatenate([_stack_heads(b * e), _stack_heads(k * e)], axis=0))
                for a, x, b, k, e in zip(a_s, r_t, bn, kx, e_in)]
        a_ab = [jnp.where(strict, g[0:n2, 0:n2], 0.0) for g in gram]
        a_ak = [jnp.where(strict, g[0:n2, n2:2 * n2], 0.0) for g in gram]
        a_rbk = [jnp.concatenate([jnp.where(incl, g[n2:2 * n2, 0:n2], 0.0),
                                  jnp.where(incl, g[n2:2 * n2, n2:2 * n2], 0.0)], axis=1) for g in gram]
        t_inv = _unit_lower_inverses([-a for a in a_ab], n2)
        akv = [_mm(a, x) for a, x in zip(a_ak, v_st)]
        x_au = [_mm(t, jnp.concatenate([a, b], axis=1)) for t, a, b in zip(t_inv, a_s, akv)]
        z = [_mm(a, jnp.concatenate([x, jnp.concatenate([zero, w], axis=1)], axis=0))
             for a, x, w in zip(a_rbk, x_au, v_st)]
        r_p = [x + y[:, 0:LANES] for x, y in zip(r_t, z)]
        bt = [b.T for b in b_hat]
        wmat = [_mm(b, x[:, 0:LANES]) for b, x in zip(bt, x_au)]
        n0 = [_mm(jnp.concatenate([b, k.T], axis=1), jnp.concatenate([x[:, LANES:2 * LANES], w], axis=0))
              for b, k, x, w in zip(bt, k_hat, x_au, v_st)]
        g_col = [jnp.exp(c.T[:, CHUNK - 1:CHUNK]) for c in ci]
        sts = [st_ref[p] for p in range(npair)]
        for idx, (j, p) in enumerate(jobs):
            lanes = slice(p * LANES, (p + 1) * LANES)
            st = sts[p]
            ys = _mm(r_p[idx], st) + z[idx][:, LANES:2 * LANES]
            sts[p] = g_col[idx] * st + _mm(wmat[idx], st) + n0[idx]
            y = ys[0:CHUNK] + ys[CHUNK:n2]
            mean = _seg_sum(y, ones_bd) * (1.0 / RW_HEAD)
            d = y - mean
            var = _seg_sum(d * d, ones_bd) * (1.0 / RW_HEAD)
            yn = d * lax.rsqrt(var + RW_GN_EPS) * gng_ref[:, lanes] + gnb_ref[:, lanes]
            bonus = _seg_sum(r[idx] * kx[idx] * rk_ref[:, lanes], ones_bd) * v[idx]
            o_ref[rows[j], lanes] = ((yn + bonus) * g_s[rows[j], lanes]).astype(o_ref.dtype)
        for p in range(npair):
            st_ref[p] = sts[p]
        return carry

    lax.fori_loop(0, tb // (cpi * CHUNK), step, 0)


def _rw_call(h, mu, w0, w2, a0, a2, g2, k_k, k_a, r_k, gn_g, gn_b, bsz, seq, tb):
    nt = seq // tb
    vec = _const_spec((1, RW_WIDTH))
    return pl.pallas_call(
        functools.partial(_rw_body, tb=tb),
        grid=(bsz, nt),
        in_specs=[pl.BlockSpec((tb, RW_PAD), lambda b, t: (b * nt + t, 0)),
                  _const_spec((1, RW_PAD)), vec, _const_spec((LANES, RW_WIDTH)), vec,
                  _const_spec((LANES, RW_WIDTH)), _const_spec((2 * LANES, RW_WIDTH)),
                  vec, vec, vec, vec, vec],
        out_specs=pl.BlockSpec((tb, RW_WIDTH), lambda b, t: (b * nt + t, 0)),
        out_shape=jax.ShapeDtypeStruct((bsz * seq, RW_WIDTH), BF16),
        scratch_shapes=[pltpu.VMEM((tb + 8, RW_PAD), F32)] + [pltpu.VMEM((tb, RW_WIDTH), F32)] * 7
        + [pltpu.VMEM((RW_HEADS // 2, LANES, LANES), F32)],
        compiler_params=_cparams(("arbitrary", "arbitrary")),
        name="rwkv7_time_mix",
    )(h, mu, w0, w2, a0, a2, g2, k_k, k_a, r_k, gn_g, gn_b)


def _pad_to(a, size, axis):
    pad = [(0, 0)] * a.ndim
    pad[axis] = (0, size - a.shape[axis])
    return jnp.pad(a, pad)


def _mixer_in_weights(w_in):
    o = 0
    rw = w_in[:, o:o + RW_IN]
    o += RW_IN
    gla = w_in[:, o:o + GLA_IN]
    o += GLA_IN
    gdn = w_in[:, o:o + GDN_IN]
    o += GDN_IN
    gate = w_in[:, o:]
    qkv_w = 2 * GLA_HEADS * GLA_DK + GLA_HEADS * GLA_DV
    gla_p = jnp.concatenate([gla[:, 0:qkv_w], gla[:, qkv_w + GLA_GATE_LORA:], gla[:, qkv_w:qkv_w + GLA_GATE_LORA]], axis=1)
    gdn_p = jnp.concatenate([gdn[:, 0:GDN_QKV], gdn[:, GDN_QKV + 2 * GDN_HEADS:], gdn[:, GDN_QKV:GDN_QKV + 2 * GDN_HEADS]], axis=1)
    mix = jnp.concatenate([_pad_to(rw, RW_PAD, 1), _pad_to(gla_p, GLA_PAD, 1), _pad_to(gdn_p, GDN_PAD, 1)], axis=1)
    return mix.astype(BF16), gate.astype(BF16)


def _row(v):
    return v.reshape(1, -1)


def kernel(x, p, ln_g, ln_b, ffn_w1, ffn_w3, ffn_w2, w_in, rw_mu, rw_w0, rw_w2, rw_a0, rw_a2, rw_g2, rw_k_k, rw_k_a, rw_r_k, rw_gn_g, rw_gn_b, gla_gk_w2, gla_gk_b, gla_norm_g, gdn_conv_w, gdn_a_log, gdn_dt_bias, gdn_norm_g, w_branch, w_o, ple_w_gate, ple_w_proj):
    bsz, seq, d = x.shape
    depth = p.shape[0]
    n = bsz * seq
    tm = min(512, n)
    tb = min(512, seq)
    x = x.reshape(n, d)
    for i in range(depth):
        w1 = ffn_w1[i].astype(BF16)
        w3 = ffn_w3[i].astype(BF16)
        w2 = ffn_w2[i].astype(BF16)
        x = _ffn_call(x, w1[0], w3[0], w2[0], _row(ln_g[i, 0]), _row(ln_b[i, 0]), tm)

        w_mix, w_gate = _mixer_in_weights(w_in[i])
        h_rw, h_gla, h_gdn = _proj_call(x, w_mix, min(256, n))
        lora = jnp.zeros((LANES, RW_WIDTH), F32)
        o_rw = _rw_call(
            h_rw, _row(_pad_to(rw_mu[i], RW_PAD, 0)), _row(rw_w0[i]),
            lora.at[0:RW_DECAY_LORA].set(rw_w2[i]), _row(rw_a0[i]),
            lora.at[RW_DECAY_LORA:RW_DECAY_LORA + RW_AAA_LORA].set(rw_a2[i]),
            _pad_to(rw_g2[i], 2 * LANES, 0), _row(rw_k_k[i]), _row(rw_k_a[i]), _row(rw_r_k[i]),
            _row(rw_gn_g[i]), _row(rw_gn_b[i]), bsz, seq, tb)
        o_gla = _gla_call(h_gla, _pad_to(gla_gk_w2[i], LANES, 0), _row(gla_gk_b[i]), _row(gla_norm_g[i]), bsz, seq, tb)
        o_gdn = _gdn_call(h_gdn, gdn_conv_w[i], _row(_pad_to(gdn_a_log[i], LANES, 0)),
                          _row(_pad_to(gdn_dt_bias[i], LANES, 0)), _row(gdn_norm_g[i]), bsz, seq, tb)
        x = _merge_call(x, o_rw, o_gla, o_gdn, w_gate, w_branch[i].astype(BF16), w_o[i].astype(BF16),
                        _row(ln_g[i, 1]), _row(ln_b[i, 1]), tm)

        x = _ffn_ple_call(x, p[i].reshape(n, PLE_DIM), w1[1], w3[1], w2[1], _row(ln_g[i, 2]), _row(ln_b[i, 2]),
                          ple_w_gate[i].astype(BF16), ple_w_proj[i].astype(BF16),
                          _row(ln_g[i, 3]), _row(ln_b[i, 3]), tm)
    return x.reshape(bsz, seq, d)
```

```python
import functools

import jax
import jax.numpy as jnp
from jax import lax
from jax.experimental import pallas as pl
from jax.experimental.pallas import tpu as pltpu

F32 = jnp.float32
BF16 = jnp.bfloat16

D_MODEL = 1024
DEPTH = 2
PLE_DIM = 256
D_FF = 2816
LN_EPS = 1e-5
DEEPNORM_ALPHA = (2 * DEPTH) ** 0.25
MACARON_W = 0.5

RW_HEADS = 8
RW_HEAD = 64
RW_WIDTH = RW_HEADS * RW_HEAD
RW_DECAY_LORA = 64
RW_AAA_LORA = 64
RW_GATE_LORA = 160
RW_GN_EPS = 1e-5 * RW_HEAD
RW_IN = 3 * RW_WIDTH + RW_DECAY_LORA + RW_AAA_LORA + RW_GATE_LORA

GLA_HEADS = 4
GLA_DK = 64
GLA_DV = 128
GLA_GATE_LORA = 16
GLA_GATE_NORM = 16.0
GLA_IN = 2 * GLA_HEADS * GLA_DK + 2 * GLA_HEADS * GLA_DV + GLA_GATE_LORA

GDN_HEADS = 4
GDN_DK = 128
GDN_DV = 128
GDN_CONV = 4
GDN_QKV = GDN_HEADS * (2 * GDN_DK + GDN_DV)
GDN_IN = GDN_QKV + 2 * GDN_HEADS + GDN_HEADS * GDN_DV

HEAD_NORM_EPS = 1e-6
L2_EPS = 1e-6
N_BRANCH = 3
BRANCH_WIDTH = 512

LANES = 128
CHUNK = 64
GDN_CHUNKS_PER_STEP = 4
RW_CHUNKS_PER_STEP = 4
GLA_CHUNKS_PER_STEP = 2
RW_PAD = 1920
GLA_PAD = 1664
GDN_PAD = 2176
MIX_PAD = RW_PAD + GLA_PAD + GDN_PAD
VMEM_LIMIT = 56 * 1024 * 1024


def _param(arr, *prefix):
    tail = arr.shape[len(prefix):]
    zeros = (0,) * len(tail)
    spec = pl.BlockSpec((None,) * len(prefix) + tail, lambda *_: prefix + zeros, pipeline_mode=pl.Buffered(1))
    return arr, spec


def _call(body, grid, sem, streamed, params, out_specs, out_shape, scratch, name):
    ops = list(streamed) + list(params)
    return pl.pallas_call(
        body,
        grid=grid,
        in_specs=[spec for _, spec in ops],
        out_specs=out_specs,
        out_shape=out_shape,
        scratch_shapes=scratch,
        compiler_params=pltpu.CompilerParams(dimension_semantics=sem, vmem_limit_bytes=VMEM_LIMIT),
        name=name,
    )(*[arr for arr, _ in ops])


def _mm(a, b):
    return jnp.dot(a.astype(BF16), b.astype(BF16), preferred_element_type=F32)


def _mm_nt(a, b):
    return lax.dot_general(a.astype(BF16), b.astype(BF16), (((1,), (1,)), ((), ())),
                           preferred_element_type=F32)


def _split_bf16(x, parts):
    out = []
    for _ in range(parts - 1):
        piece = x.astype(BF16)
        out.append(piece)
        x = x - piece.astype(F32)
    out.append(x.astype(BF16))
    return out


def _mm_f32(a, b):
    a_hi, a_lo = _split_bf16(a, 2)
    b_hi, b_lo = _split_bf16(b, 2)
    dot = functools.partial(jnp.dot, preferred_element_type=F32)
    return dot(a_hi, b_hi) + (dot(a_hi, b_lo) + dot(a_lo, b_hi))


def _chunk_cumsum(x):
    tril = _tril_f32(CHUNK).astype(BF16)
    out = []
    for r0 in range(0, x.shape[0], CHUNK):
        pieces = _split_bf16(x[r0:r0 + CHUNK], 3)
        acc = jnp.dot(tril, pieces[2], preferred_element_type=F32)
        acc = acc + jnp.dot(tril, pieces[1], preferred_element_type=F32)
        out.append(acc + jnp.dot(tril, pieces[0], preferred_element_type=F32))
    return jnp.concatenate(out, axis=0)


def _layer_norm(y, g, b):
    m = jnp.mean(y, -1, keepdims=True)
    d = y - m
    var = jnp.mean(d * d, -1, keepdims=True)
    return d * lax.rsqrt(var + LN_EPS) * g + b


def _sigmoid(x):
    return 1.0 / (1.0 + jnp.exp(-x))


def _softplus(x):
    return jnp.maximum(x, 0.0) + jnp.log(1.0 + jnp.exp(-jnp.abs(x)))


def _iota2(shape):
    return (lax.broadcasted_iota(jnp.int32, shape, 0), lax.broadcasted_iota(jnp.int32, shape, 1))


def _tril_f32(n):
    r, c = _iota2((n, n))
    return jnp.where(r >= c, 1.0, 0.0).astype(F32)


def _sibling_mask(r, c, s):
    k = s.bit_length() - 1
    return ((r >> (k + 1)) == (c >> (k + 1))) & (((r >> k) & 1) == 1) & (((c >> k) & 1) == 0)


def _unit_lower_inverses(lows, n):
    r, c = _iota2((n, n))
    eye = jnp.where(r == c, 1.0, 0.0).astype(F32)
    first = _sibling_mask(r, c, 1)
    xs = [eye - jnp.where(first, low, 0.0) for low in lows]
    s = 2
    while s < CHUNK:
        sib = _sibling_mask(r, c, s)
        ts = [_mm(jnp.where(sib, low, 0.0), x) for low, x in zip(lows, xs)]
        us = [_mm(x, t) for x, t in zip(xs, ts)]
        xs = [x - u for x, u in zip(xs, us)]
        s *= 2
    return xs


def _stack_heads(z):
    lane = lax.broadcasted_iota(jnp.int32, z.shape, 1)
    first = lane < (LANES // 2)
    return jnp.concatenate([jnp.where(first, z, 0.0), jnp.where(first, 0.0, z)], axis=0)


def _block_causal(n, strict):
    r, c = _iota2((n, n))
    same = (r >> 6) == (c >> 6)
    return same & ((r > c) if strict else (r >= c))


FF_CHUNK = 1408


def _swiglu_ln(x, w1_ref, w3_ref, w2_ref, g, b):
    xb = x.astype(BF16)
    acc = None
    for c0 in range(0, D_FF, FF_CHUNK):
        h1 = jnp.dot(xb, w1_ref[:, c0:c0 + FF_CHUNK], preferred_element_type=F32)
        h3 = jnp.dot(xb, w3_ref[:, c0:c0 + FF_CHUNK], preferred_element_type=F32)
        act = (h1 * _sigmoid(h1) * h3).astype(BF16)
        part = jnp.dot(act, w2_ref[c0:c0 + FF_CHUNK, :], preferred_element_type=F32)
        acc = part if acc is None else acc + part
    return _layer_norm(DEEPNORM_ALPHA * x + MACARON_W * acc, g, b)


def _ffn_body(x_ref, w1_ref, w3_ref, w2_ref, g_ref, b_ref, o_ref):
    o_ref[...] = _swiglu_ln(x_ref[...], w1_ref, w3_ref, w2_ref, g_ref[...], b_ref[...])


def _ffn_ple_body(x_ref, p_ref, w1_ref, w3_ref, w2_ref, g_ref, b_ref, wg_ref, wp_ref, g2_ref, b2_ref, o_ref):
    x = _swiglu_ln(x_ref[...], w1_ref, w3_ref, w2_ref, g_ref[...], b_ref[...])
    gate = _sigmoid(jnp.dot(x.astype(BF16), wg_ref[...], preferred_element_type=F32))
    emb = jnp.dot(p_ref[...].astype(BF16), wp_ref[...], preferred_element_type=F32)
    o_ref[...] = _layer_norm(DEEPNORM_ALPHA * x + gate * emb, g2_ref[...], b2_ref[...])


def _token_rows(x, tm):
    return x, pl.BlockSpec((tm, x.shape[1]), lambda i: (i, 0))


def _ffn_call(x, params, tm):
    n = x.shape[0]
    return _call(_ffn_body, (n // tm,), ("arbitrary",), [_token_rows(x, tm)], params,
                 pl.BlockSpec((tm, D_MODEL), lambda i: (i, 0)), jax.ShapeDtypeStruct((n, D_MODEL), F32), [], "ffn_ln")


def _ffn_ple_call(x, p, layer, params, tm):
    n = x.shape[0]
    p_rows = (p, pl.BlockSpec((None, tm, PLE_DIM), lambda i: (layer, i, 0)))
    return _call(_ffn_ple_body, (n // tm,), ("arbitrary",), [_token_rows(x, tm), p_rows], params,
                 pl.BlockSpec((tm, D_MODEL), lambda i: (i, 0)), jax.ShapeDtypeStruct((n, D_MODEL), F32), [], "ffn_ple_ln")


def _merge_body(x_ref, orw_ref, ogla_ref, ogdn_ref, wgate_ref, wbr_ref, wo_ref, g_ref, b_ref, o_ref):
    x = x_ref[...]
    xb = x.astype(BF16)
    merged = None
    for n, br_ref in enumerate((orw_ref, ogla_ref, ogdn_ref)):
        gate = _sigmoid(jnp.dot(xb, wgate_ref[:, n * D_MODEL:(n + 1) * D_MODEL], preferred_element_type=F32))
        term = gate * jnp.dot(br_ref[...], wbr_ref[n], preferred_element_type=F32)
        merged = term if merged is None else merged + term
    mix = jnp.dot(merged.astype(BF16), wo_ref[...], preferred_element_type=F32)
    o_ref[...] = _layer_norm(DEEPNORM_ALPHA * x + mix, g_ref[...], b_ref[...])


def _merge_call(x, branches, params, tm):
    n = x.shape[0]
    return _call(_merge_body, (n // tm,), ("arbitrary",), [_token_rows(t, tm) for t in (x,) + tuple(branches)], params,
                 pl.BlockSpec((tm, D_MODEL), lambda i: (i, 0)), jax.ShapeDtypeStruct((n, D_MODEL), F32), [], "merge_ln")


def _gdn_body(x_ref, w_ref, cw_ref, alog_ref, dtb_ref, ng_ref, o_ref,
              ext_ref, q_s, k_s, v_s, g_s, beta_s, gate_s, st_ref, *, tb):
    t_idx = pl.program_id(1)

    @pl.when(t_idx == 0)
    def _():
        ext_ref[0:8, :] = jnp.zeros((8, GDN_QKV), F32)
        st_ref[...] = jnp.zeros_like(st_ref)

    @pl.when(t_idx != 0)
    def _():
        ext_ref[0:8, :] = ext_ref[tb:tb + 8, :]

    xb = x_ref[...].astype(BF16)
    ext_ref[8:tb + 8, :] = jnp.dot(xb, w_ref[:, 0:GDN_QKV], preferred_element_type=F32)
    gate_s[...] = jnp.dot(xb, w_ref[:, GDN_QKV:GDN_QKV + GDN_HEADS * GDN_DV], preferred_element_type=F32)

    hw = GDN_HEADS * GDN_DK
    for grp, dst in enumerate((q_s, k_s, v_s)):
        cols = slice(grp * hw, (grp + 1) * hw)
        acc = ext_ref[8:tb + 8, cols] * cw_ref[GDN_CONV - 1:GDN_CONV, cols]
        for d in range(1, GDN_CONV):
            acc = acc + ext_ref[8 - d:8 - d + tb, cols] * cw_ref[GDN_CONV - 1 - d:GDN_CONV - d, cols]
        act = acc * _sigmoid(acc)
        if grp == 2:
            dst[...] = act
        else:
            scale = GDN_DK ** -0.5 if grp == 0 else 1.0
            for h in range(GDN_HEADS):
                t = act[:, h * GDN_DK:(h + 1) * GDN_DK]
                dst[:, h * GDN_DK:(h + 1) * GDN_DK] = t * (lax.rsqrt(jnp.sum(t * t, -1, keepdims=True) + L2_EPS) * scale)

    ab = jnp.dot(xb, w_ref[:, GDN_QKV + GDN_HEADS * GDN_DV:GDN_PAD], preferred_element_type=F32)
    g_s[...] = _chunk_cumsum(-jnp.exp(alog_ref[...]) * _softplus(ab + dtb_ref[...]))
    beta_s[...] = _sigmoid(ab)

    n2 = 2 * CHUNK
    ng = ng_ref[...]

    cpi = GDN_CHUNKS_PER_STEP
    jobs = [(j, pr) for j in range(cpi) for pr in range(GDN_HEADS // 2)]

    def pair_rows(src, rows, pr, width):
        return jnp.concatenate([src[rows, h * width:(h + 1) * width] for h in (2 * pr, 2 * pr + 1)], axis=0)

    def pair_col(a, pr, off):
        return jnp.concatenate([a[:, off + h:off + h + 1] for h in (2 * pr, 2 * pr + 1)], axis=0)

    def step(it, carry):
        rows = [pl.ds(pl.multiple_of(it * (cpi * CHUNK) + j * CHUNK, CHUNK), CHUNK) for j in range(cpi)]
        causal = _block_causal(n2, strict=False)
        strict = _block_causal(n2, strict=True)
        gcs = [g_s[rw, :] for rw in rows]
        betas = [beta_s[rw, :] for rw in rows]
        gi = [pair_col(gcs[j], pr, 0) for j, pr in jobs]
        bcol = [pair_col(betas[j], pr, GDN_HEADS) for j, pr in jobs]
        qs = [pair_rows(q_s, rows[j], pr, GDN_DK) for j, pr in jobs]
        ks = [pair_rows(k_s, rows[j], pr, GDN_DK) for j, pr in jobs]
        vs = [pair_rows(v_s, rows[j], pr, GDN_DV) for j, pr in jobs]
        dec = []
        for g in gi:
            gi_b = jnp.broadcast_to(g, (n2, n2))
            diff = gi_b - gi_b.T
            dec.append(jnp.where(causal, jnp.exp(jnp.where(causal, diff, 0.0)), 0.0))
        kb = [k * b for k, b in zip(ks, bcol)]
        gram = [_mm_nt(jnp.concatenate([a, q], axis=0), k) for a, q, k in zip(kb, qs, ks)]
        low = [jnp.where(strict, g[0:n2] * d, 0.0) for g, d in zip(gram, dec)]
        att = [g[n2:2 * n2] * d for g, d in zip(gram, dec)]
        t_inv = _unit_lower_inverses(low, n2)
        eg = [jnp.exp(g) for g in gi]
        wu = [_mm(t, jnp.concatenate([a * e, v * b], axis=1))
              for t, a, e, v, b in zip(t_inv, kb, eg, vs, bcol)]
        aw = [_mm(a, x) for a, x in zip(att, wu)]
        per_head = {}
        for idx, (j, pr) in enumerate(jobs):
            for half, h in enumerate((2 * pr, 2 * pr + 1)):
                sl = slice(half * CHUNK, (half + 1) * CHUNK)
                g_end = gcs[j][CHUNK - 1:CHUNK, h:h + 1]
                k_hat = ks[idx][sl] * jnp.exp(g_end - gi[idx][sl])
                kwu = _mm(k_hat.T, wu[idx][sl])
                q_eff = qs[idx][sl] * eg[idx][sl] - aw[idx][sl, 0:GDN_DV]
                per_head[(j, h)] = (q_eff, aw[idx][sl, GDN_DV:2 * GDN_DV], jnp.exp(g_end), kwu)
        sts = [st_ref[h] for h in range(GDN_HEADS)]
        for j in range(cpi):
            for h in range(GDN_HEADS):
                q_eff, o_loc, g_dec, kwu = per_head[(j, h)]
                st = sts[h]
                o = _mm(q_eff, st) + o_loc
                sts[h] = g_dec * st - _mm(kwu[:, 0:GDN_DV], st) + kwu[:, GDN_DV:2 * GDN_DV]
                o = o * lax.rsqrt(jnp.mean(o * o, -1, keepdims=True) + HEAD_NORM_EPS) * ng
                gate = gate_s[rows[j], h * GDN_DV:(h + 1) * GDN_DV]
                o_ref[rows[j], h * GDN_DV:(h + 1) * GDN_DV] = (o * gate * _sigmoid(gate)).astype(o_ref.dtype)
        for h in range(GDN_HEADS):
            st_ref[h] = sts[h]
        return carry

    lax.fori_loop(0, tb // (cpi * CHUNK), step, 0)


def _mixer_call(body, x, params, width, scratch, bsz, seq, tb, name):
    nt = seq // tb
    rows = lambda b, t: (b * nt + t, 0)
    return _call(functools.partial(body, tb=tb), (bsz, nt), ("arbitrary", "arbitrary"),
                 [(x, pl.BlockSpec((tb, D_MODEL), rows))], params,
                 pl.BlockSpec((tb, width), rows), jax.ShapeDtypeStruct((bsz * seq, width), BF16), scratch, name)


def _gdn_call(x, params, bsz, seq, tb):
    wide = pltpu.VMEM((tb, GDN_HEADS * GDN_DK), F32)
    narrow = pltpu.VMEM((tb, LANES), F32)
    scratch = [pltpu.VMEM((tb + 8, GDN_QKV), F32), wide, wide, wide, narrow, narrow, wide,
               pltpu.VMEM((GDN_HEADS, GDN_DK, GDN_DV), F32)]
    return _mixer_call(_gdn_body, x, params, GDN_HEADS * GDN_DV, scratch, bsz, seq, tb, "gated_deltanet")


def _gla_level_arg(s, la, b, ri):
    c = CHUNK
    if s >= 4:
        ref = jnp.concatenate(
            [jnp.broadcast_to(b[g * 2 * s + s - 1:g * 2 * s + s, :], (2 * s, LANES)) for g in range(c // (2 * s))],
            axis=0)
        return -jnp.abs(b - ref)
    if s == 2:
        m = ri & 3
        up = pltpu.roll(la, c - 1, 0)
        dn = pltpu.roll(la, 1, 0)
        return jnp.where(m == 0, up, jnp.where(m == 1, 0.0, jnp.where(m == 2, la, la + dn)))
    return jnp.where((ri & 1) == 1, la, 0.0)


def _gla_scores(qs, ks, las, bs):
    c = CHUNK
    ri = lax.broadcasted_iota(jnp.int32, (c, LANES), 0)
    r2, c2 = _iota2((2 * c, c))
    r2 = r2 & (c - 1)
    diag = r2 == c2
    scores = [jnp.where(diag, _mm_nt(_stack_heads(q), k), 0.0) for q, k in zip(qs, ks)]
    s = c // 2
    while s >= 1:
        row_side = ((ri >> (s.bit_length() - 1)) & 1) == 1
        sib = _sibling_mask(r2, c2, s)
        es = [jnp.exp(_gla_level_arg(s, la, b, ri)) for la, b in zip(las, bs)]
        part = [_mm_nt(_stack_heads(jnp.where(row_side, q * e, 0.0)), jnp.where(row_side, 0.0, k * e))
                for q, k, e in zip(qs, ks, es)]
        scores = [sc + jnp.where(sib, p, 0.0) for sc, p in zip(scores, part)]
        s //= 2
    return scores


def _gla_body(x_ref, w_ref, w2_ref, gkb_ref, ng_ref, o_ref, h_s, la_s, b_s, st_ref, *, tb):
    t_idx = pl.program_id(1)

    @pl.when(t_idx == 0)
    def _():
        st_ref[...] = jnp.zeros_like(st_ref)

    hk = GLA_HEADS * GLA_DK
    hv = GLA_HEADS * GLA_DV
    xb = x_ref[...].astype(BF16)
    h_s[...] = jnp.dot(xb, w_ref[:, 0:2 * hk + 2 * hv], preferred_element_type=F32)
    gk_lo = jnp.dot(xb, w_ref[:, 2 * hk + 2 * hv:GLA_PAD], preferred_element_type=F32)
    z = _mm_f32(gk_lo, w2_ref[...]) + gkb_ref[...]
    la = (jnp.minimum(z, 0.0) - jnp.log(1.0 + jnp.exp(-jnp.abs(z)))) * (1.0 / GLA_GATE_NORM)
    la_s[...] = la
    b_s[...] = _chunk_cumsum(la)
    ng = ng_ref[...]
    cpi = GLA_CHUNKS_PER_STEP
    npair = GLA_HEADS // 2
    jobs = [(j, p) for j in range(cpi) for p in range(npair)]

    def step(it, carry):
        rows = [pl.ds(pl.multiple_of(it * (cpi * CHUNK) + j * CHUNK, CHUNK), CHUNK) for j in range(cpi)]
        row_first = lax.broadcasted_iota(jnp.int32, (LANES, GLA_DV), 0) < GLA_DK
        qs = [h_s[rows[j], p * LANES:(p + 1) * LANES] * (GLA_DK ** -0.5) for j, p in jobs]
        ks = [h_s[rows[j], hk + p * LANES:hk + (p + 1) * LANES] for j, p in jobs]
        las = [la_s[rows[j], p * LANES:(p + 1) * LANES] for j, p in jobs]
        bs = [b_s[rows[j], p * LANES:(p + 1) * LANES] for j, p in jobs]
        vpair = [h_s[rows[j], 2 * hk + 2 * p * GLA_DV:2 * hk + (2 * p + 2) * GLA_DV] for j, p in jobs]
        scores = _gla_scores(qs, ks, las, bs)
        q_in = [_stack_heads(q * jnp.exp(b)) for q, b in zip(qs, bs)]
        bts = [b.T for b in bs]
        b_end = [bt[:, CHUNK - 1:CHUNK] for bt in bts]
        upd = [_mm(k.T * jnp.exp(be - bt), vp) for k, be, bt, vp in zip(ks, b_end, bts, vpair)]
        intra = [[_mm(sc[half * CHUNK:(half + 1) * CHUNK], vp[:, half * GLA_DV:(half + 1) * GLA_DV])
                  for half in range(2)] for sc, vp in zip(scores, vpair)]
        sts = [st_ref[p] for p in range(npair)]
        for idx, (j, p) in enumerate(jobs):
            st = sts[p]
            inter = _mm(q_in[idx], st)
            sts[p] = st * jnp.exp(b_end[idx]) + jnp.where(row_first, upd[idx][:, 0:GLA_DV], upd[idx][:, GLA_DV:2 * GLA_DV])
            for half in range(2):
                h = 2 * p + half
                o = intra[idx][half] + inter[half * CHUNK:(half + 1) * CHUNK]
                o = o * lax.rsqrt(jnp.mean(o * o, -1, keepdims=True) + HEAD_NORM_EPS) * ng
                gate = h_s[rows[j], 2 * hk + hv + h * GLA_DV:2 * hk + hv + (h + 1) * GLA_DV]
                o_ref[rows[j], h * GLA_DV:(h + 1) * GLA_DV] = (o * gate * _sigmoid(gate)).astype(o_ref.dtype)
        for p in range(npair):
            st_ref[p] = sts[p]
        return carry

    lax.fori_loop(0, tb // (cpi * CHUNK), step, 0)


def _gla_call(x, params, bsz, seq, tb):
    keys = pltpu.VMEM((tb, GLA_HEADS * GLA_DK), F32)
    scratch = [pltpu.VMEM((tb, GLA_PAD - LANES), F32), keys, keys,
               pltpu.VMEM((GLA_HEADS // 2, 2 * GLA_DK, GLA_DV), F32)]
    return _mixer_call(_gla_body, x, params, GLA_HEADS * GLA_DV, scratch, bsz, seq, tb, "gated_linear_attention")


def _seg_sum(x, ones_bd):
    hi = x.astype(BF16)
    lo = (x - hi.astype(F32)).astype(BF16)
    return (jnp.dot(hi, ones_bd, preferred_element_type=F32) + jnp.dot(lo, ones_bd, preferred_element_type=F32))


def _rw_body(x_ref, w_ref, mu_ref, w0_ref, w2_ref, a0_ref, a2_ref, g2_ref, kk_ref, ka_ref, rk_ref, gng_ref, gnb_ref,
             o_ref, ext_ref, r_s, k_s, v_s, ld_s, ci_s, an_s, bn_s, g_s, st_ref, *, tb):
    t_idx = pl.program_id(1)

    @pl.when(t_idx == 0)
    def _():
        ext_ref[0:8, :] = jnp.zeros((8, RW_PAD), F32)
        st_ref[...] = jnp.zeros_like(st_ref)

    @pl.when(t_idx != 0)
    def _():
        ext_ref[0:8, :] = ext_ref[tb:tb + 8, :]

    ext_ref[8:tb + 8, :] = jnp.dot(x_ref[...].astype(BF16), w_ref[...], preferred_element_type=F32)

    def shifted(cols):
        cur = ext_ref[8:tb + 8, cols]
        prev = ext_ref[7:tb + 7, cols]
        return cur + (prev - cur) * mu_ref[:, cols]

    w = RW_WIDTH
    r2, c2 = _iota2((LANES, LANES))
    ones_bd = jnp.where((r2 >> 6) == (c2 >> 6), 1.0, 0.0).astype(BF16)

    lo = shifted(slice(3 * w, 3 * w + LANES))
    glo = shifted(slice(3 * w + LANES, RW_PAD))
    wv = -_softplus(-(w0_ref[...] + _mm_f32(jnp.tanh(lo), w2_ref[...]))) - 0.5
    ld = -jnp.exp(wv)
    ld_s[...] = ld
    ci_s[...] = _chunk_cumsum(ld)
    a = _sigmoid(a0_ref[...] + _mm_f32(lo, a2_ref[...]))
    g_s[...] = _mm_f32(_sigmoid(glo), g2_ref[...])
    r_s[...] = shifted(slice(0, w))
    v_s[...] = shifted(slice(2 * w, 3 * w))
    k = shifted(slice(w, 2 * w))
    kk = k * kk_ref[...]
    for p in range(RW_HEADS // 2):
        lanes = slice(p * LANES, (p + 1) * LANES)
        t = kk[:, lanes]
        kkn = t * lax.rsqrt(_seg_sum(t * t, ones_bd) + L2_EPS)
        an_s[:, lanes] = -kkn
        bn_s[:, lanes] = kkn * a[:, lanes]
    k_s[...] = k * (1.0 + (a - 1.0) * ka_ref[...])

    n2 = 2 * CHUNK

    cpi = RW_CHUNKS_PER_STEP
    npair = RW_HEADS // 2
    jobs = [(j, p) for j in range(cpi) for p in range(npair)]

    def step(it, carry):
        rows = [pl.ds(pl.multiple_of(it * (cpi * CHUNK) + j * CHUNK, CHUNK), CHUNK) for j in range(cpi)]
        strict = _block_causal(n2, strict=True)
        incl = _block_causal(n2, strict=False)
        zero = jnp.zeros((n2, LANES), F32)

        def load(src):
            return [src[rows[j], p * LANES:(p + 1) * LANES] for j, p in jobs]

        r, kx, v, ld, ci, an, bn = (load(s) for s in (r_s, k_s, v_s, ld_s, ci_s, an_s, bn_s))
        e_in = [jnp.exp(-c) for c in ci]
        e_out = [jnp.exp(c[CHUNK - 1:CHUNK, :] - c) for c in ci]
        a_s = [_stack_heads(a * jnp.exp(c - t)) for a, c, t in zip(an, ci, ld)]
        r_t = [_stack_heads(x * jnp.exp(c)) for x, c in zip(r, ci)]
        b_hat = [_stack_heads(b * e) for b, e in zip(bn, e_out)]
        k_hat = [_stack_heads(k * e) for k, e in zip(kx, e_out)]
        v_st = [_stack_heads(x) for x in v]
        gram = [_mm_nt(jnp.concatenate([a, x], axis=0),
                       jnp.concatenate([_stack_heads(b * e), _stack_heads(k * e)], axis=0))
                for a, x, b, k, e in zip(a_s, r_t, bn, kx, e_in)]
        a_ab = [jnp.where(strict, g[0:n2, 0:n2], 0.0) for g in gram]
        a_ak = [jnp.where(strict, g[0:n2, n2:2 * n2], 0.0) for g in gram]
        a_rbk = [jnp.concatenate([jnp.where(incl, g[n2:2 * n2, 0:n2], 0.0),
                                  jnp.where(incl, g[n2:2 * n2, n2:2 * n2], 0.0)], axis=1) for g in gram]
        t_inv = _unit_lower_inverses([-a for a in a_ab], n2)
        akv = [_mm(a, x) for a, x in zip(a_ak, v_st)]
        x_au = [_mm(t, jnp.concatenate([a, b], axis=1)) for t, a, b in zip(t_inv, a_s, akv)]
        z = [_mm(a, jnp.concatenate([x, jnp.concatenate([zero, w], axis=1)], axis=0))
             for a, x, w in zip(a_rbk, x_au, v_st)]
        r_p = [x + y[:, 0:LANES] for x, y in zip(r_t, z)]
        bt = [b.T for b in b_hat]
        wmat = [_mm(b, x[:, 0:LANES]) for b, x in zip(bt, x_au)]
        n0 = [_mm(jnp.concatenate([b, k.T], axis=1), jnp.concatenate([x[:, LANES:2 * LANES], w], axis=0))
              for b, k, x, w in zip(bt, k_hat, x_au, v_st)]
        g_col = [jnp.exp(c.T[:, CHUNK - 1:CHUNK]) for c in ci]
        sts = [st_ref[p] for p in range(npair)]
        for idx, (j, p) in enumerate(jobs):
            lanes = slice(p * LANES, (p + 1) * LANES)
            st = sts[p]
            ys = _mm(r_p[idx], st) + z[idx][:, LANES:2 * LANES]
            sts[p] = g_col[idx] * st + _mm(wmat[idx], st) + n0[idx]
            y = ys[0:CHUNK] + ys[CHUNK:n2]
            mean = _seg_sum(y, ones_bd) * (1.0 / RW_HEAD)
            d = y - mean
            var = _seg_sum(d * d, ones_bd) * (1.0 / RW_HEAD)
            yn = d * lax.rsqrt(var + RW_GN_EPS) * gng_ref[:, lanes] + gnb_ref[:, lanes]
            bonus = _seg_sum(r[idx] * kx[idx] * rk_ref[:, lanes], ones_bd) * v[idx]
            o_ref[rows[j], lanes] = ((yn + bonus) * g_s[rows[j], lanes]).astype(o_ref.dtype)
        for p in range(npair):
            st_ref[p] = sts[p]
        return carry

    lax.fori_loop(0, tb // (cpi * CHUNK), step, 0)


def _rw_call(x, params, bsz, seq, tb):
    scratch = ([pltpu.VMEM((tb + 8, RW_PAD), F32)] + [pltpu.VMEM((tb, RW_WIDTH), F32)] * 8
               + [pltpu.VMEM((RW_HEADS // 2, LANES, LANES), F32)])
    return _mixer_call(_rw_body, x, params, RW_WIDTH, scratch, bsz, seq, tb, "rwkv7_time_mix")


def _pad_to(a, size, axis):
    pad = [(0, 0)] * a.ndim
    pad[axis] = (0, size - a.shape[axis])
    return jnp.pad(a, pad)


def _mixer_in_weights(w_in):
    o = 0
    rw = w_in[..., o:o + RW_IN]
    o += RW_IN
    gla = w_in[..., o:o + GLA_IN]
    o += GLA_IN
    gdn = w_in[..., o:o + GDN_IN]
    o += GDN_IN
    gate = w_in[..., o:]
    qkv_w = 2 * GLA_HEADS * GLA_DK + GLA_HEADS * GLA_DV
    gla_p = jnp.concatenate([gla[..., 0:qkv_w], gla[..., qkv_w + GLA_GATE_LORA:], gla[..., qkv_w:qkv_w + GLA_GATE_LORA]], axis=-1)
    gdn_p = jnp.concatenate([gdn[..., 0:GDN_QKV], gdn[..., GDN_QKV + 2 * GDN_HEADS:], gdn[..., GDN_QKV:GDN_QKV + 2 * GDN_HEADS]], axis=-1)
    return (_pad_to(rw, RW_PAD, 2).astype(BF16), _pad_to(gla_p, GLA_PAD, 2).astype(BF16),
            _pad_to(gdn_p, GDN_PAD, 2).astype(BF16), gate.astype(BF16))


def _rows(v):
    return v.reshape(v.shape[:-1] + (1, v.shape[-1]))


def _tiles(n, seq):
    return min(512, n), min(512, seq)


def kernel(x, p, ln_g, ln_b, ffn_w1, ffn_w3, ffn_w2, w_in, rw_mu, rw_w0, rw_w2, rw_a0, rw_a2, rw_g2, rw_k_k, rw_k_a, rw_r_k, rw_gn_g, rw_gn_b, gla_gk_w2, gla_gk_b, gla_norm_g, gdn_conv_w, gdn_a_log, gdn_dt_bias, gdn_norm_g, w_branch, w_o, ple_w_gate, ple_w_proj):
    bsz, seq, d = x.shape
    depth = p.shape[0]
    n = bsz * seq
    tm, tb = _tiles(n, seq)
    x = x.reshape(n, d)
    p = p.reshape(depth, n, PLE_DIM)

    w1, w3, w2 = ffn_w1.astype(BF16), ffn_w3.astype(BF16), ffn_w2.astype(BF16)
    w_rw, w_gla, w_gdn, w_gate = _mixer_in_weights(w_in)
    w_br, w_out = w_branch.astype(BF16), w_o.astype(BF16)
    w_pg, w_pp = ple_w_gate.astype(BF16), ple_w_proj.astype(BF16)
    lng, lnb = _rows(ln_g), _rows(ln_b)
    rw_vecs = [_rows(v) for v in (rw_k_k, rw_k_a, rw_r_k.reshape(depth, RW_WIDTH), rw_gn_g, rw_gn_b)]
    rw_mu_p = _rows(_pad_to(rw_mu, RW_PAD, 1))
    rw_w0_r, rw_a0_r = _rows(rw_w0), _rows(rw_a0)
    rw_w2_p = _pad_to(rw_w2, LANES, 1)
    rw_a2_p = jnp.pad(rw_a2, ((0, 0), (RW_DECAY_LORA, 0), (0, 0)))
    rw_g2_p = _pad_to(rw_g2, 2 * LANES, 1)
    gla_w2_p, gla_b_r, gla_ng = _pad_to(gla_gk_w2, LANES, 1), _rows(gla_gk_b), _rows(gla_norm_g)
    gdn_alog, gdn_dtb, gdn_ng = _rows(_pad_to(gdn_a_log, LANES, 1)), _rows(_pad_to(gdn_dt_bias, LANES, 1)), _rows(gdn_norm_g)

    for i in range(depth):
        x = _ffn_call(x, [_param(w1, i, 0), _param(w3, i, 0), _param(w2, i, 0), _param(lng, i, 0), _param(lnb, i, 0)], tm)
        o_rw = _rw_call(x, [_param(w_rw, i), _param(rw_mu_p, i), _param(rw_w0_r, i), _param(rw_w2_p, i),
                            _param(rw_a0_r, i), _param(rw_a2_p, i), _param(rw_g2_p, i)]
                        + [_param(v, i) for v in rw_vecs], bsz, seq, tb)
        o_gla = _gla_call(x, [_param(w_gla, i), _param(gla_w2_p, i), _param(gla_b_r, i), _param(gla_ng, i)], bsz, seq, tb)
        o_gdn = _gdn_call(x, [_param(w_gdn, i), _param(gdn_conv_w, i), _param(gdn_alog, i), _param(gdn_dtb, i),
                              _param(gdn_ng, i)], bsz, seq, tb)
        x = _merge_call(x, (o_rw, o_gla, o_gdn), [_param(w_gate, i), _param(w_br, i), _param(w_out, i),
                                                  _param(lng, i, 1), _param(lnb, i, 1)], tm)
        x = _ffn_ple_call(x, p, i, [_param(w1, i, 1), _param(w3, i, 1), _param(w2, i, 1), _param(lng, i, 2), _param(lnb, i, 2),
                                    _param(w_pg, i), _param(w_pp, i), _param(lng, i, 3), _param(lnb, i, 3)], tm)
    return x.reshape(bsz, seq, d)
```

```python
import functools

import jax
import jax.numpy as jnp
from jax import lax
from jax.experimental import pallas as pl
from jax.experimental.pallas import tpu as pltpu

F32 = jnp.float32
BF16 = jnp.bfloat16

D_MODEL = 1024
DEPTH = 2
PLE_DIM = 256
D_FF = 2816
LN_EPS = 1e-5
DEEPNORM_ALPHA = (2 * DEPTH) ** 0.25
MACARON_W = 0.5

RW_HEADS = 8
RW_HEAD = 64
RW_WIDTH = RW_HEADS * RW_HEAD
RW_DECAY_LORA = 64
RW_AAA_LORA = 64
RW_GATE_LORA = 160
RW_GN_EPS = 1e-5 * RW_HEAD
RW_IN = 3 * RW_WIDTH + RW_DECAY_LORA + RW_AAA_LORA + RW_GATE_LORA

GLA_HEADS = 4
GLA_DK = 64
GLA_DV = 128
GLA_GATE_LORA = 16
GLA_GATE_NORM = 16.0
GLA_IN = 2 * GLA_HEADS * GLA_DK + 2 * GLA_HEADS * GLA_DV + GLA_GATE_LORA

GDN_HEADS = 4
GDN_DK = 128
GDN_DV = 128
GDN_CONV = 4
GDN_QKV = GDN_HEADS * (2 * GDN_DK + GDN_DV)
GDN_IN = GDN_QKV + 2 * GDN_HEADS + GDN_HEADS * GDN_DV

HEAD_NORM_EPS = 1e-6
L2_EPS = 1e-6
N_BRANCH = 3
BRANCH_WIDTH = 512

LANES = 128
SUBLANES = 8
CHUNK = 64
GDN_CHUNKS_PER_STEP = 4
RW_CHUNKS_PER_STEP = 4
GLA_CHUNKS_PER_STEP = 2
RW_PAD = 1920
GLA_PAD = 1664
GDN_PAD = 2176
MIX_PAD = RW_PAD + GLA_PAD + GDN_PAD
VMEM_LIMIT = 56 * 1024 * 1024


def _param(arr, *prefix):
    tail = arr.shape[len(prefix):]
    zeros = (0,) * len(tail)
    spec = pl.BlockSpec((None,) * len(prefix) + tail, lambda *_: prefix + zeros, pipeline_mode=pl.Buffered(1))
    return arr, spec


def _call(body, grid, sem, streamed, params, out_specs, out_shape, scratch, name):
    ops = list(streamed) + list(params)
    return pl.pallas_call(
        body,
        grid=grid,
        in_specs=[spec for _, spec in ops],
        out_specs=out_specs,
        out_shape=out_shape,
        scratch_shapes=scratch,
        compiler_params=pltpu.CompilerParams(dimension_semantics=sem, vmem_limit_bytes=VMEM_LIMIT),
        name=name,
    )(*[arr for arr, _ in ops])


def _mm(a, b):
    return jnp.dot(a.astype(BF16), b.astype(BF16), preferred_element_type=F32)


def _mm_nt(a, b):
    return lax.dot_general(a.astype(BF16), b.astype(BF16), (((1,), (1,)), ((), ())),
                           preferred_element_type=F32)


def _split_bf16(x, parts):
    out = []
    for _ in range(parts - 1):
        piece = x.astype(BF16)
        out.append(piece)
        x = x - piece.astype(F32)
    out.append(x.astype(BF16))
    return out


def _mm_f32(a, b):
    a_hi, a_lo = _split_bf16(a, 2)
    b_hi, b_lo = _split_bf16(b, 2)
    dot = functools.partial(jnp.dot, preferred_element_type=F32)
    return dot(a_hi, b_hi) + (dot(a_hi, b_lo) + dot(a_lo, b_hi))


def _chunk_cumsum(x):
    tril = _tril_f32(CHUNK).astype(BF16)
    out = []
    for r0 in range(0, x.shape[0], CHUNK):
        pieces = _split_bf16(x[r0:r0 + CHUNK], 3)
        acc = jnp.dot(tril, pieces[2], preferred_element_type=F32)
        acc = acc + jnp.dot(tril, pieces[1], preferred_element_type=F32)
        out.append(acc + jnp.dot(tril, pieces[0], preferred_element_type=F32))
    return jnp.concatenate(out, axis=0)


def _layer_norm(y, g, b):
    m = jnp.mean(y, -1, keepdims=True)
    d = y - m
    var = jnp.mean(d * d, -1, keepdims=True)
    return d * lax.rsqrt(var + LN_EPS) * g + b


def _sigmoid(x):
    return 1.0 / (1.0 + jnp.exp(-x))


def _softplus(x):
    return jnp.maximum(x, 0.0) + jnp.log(1.0 + jnp.exp(-jnp.abs(x)))


def _iota2(shape):
    return (lax.broadcasted_iota(jnp.int32, shape, 0), lax.broadcasted_iota(jnp.int32, shape, 1))


def _tril_f32(n):
    r, c = _iota2((n, n))
    return jnp.where(r >= c, 1.0, 0.0).astype(F32)


def _sibling_mask(r, c, s):
    k = s.bit_length() - 1
    return ((r >> (k + 1)) == (c >> (k + 1))) & (((r >> k) & 1) == 1) & (((c >> k) & 1) == 0)


def _unit_lower_inverses(lows, n):
    r, c = _iota2((n, n))
    eye = jnp.where(r == c, 1.0, 0.0).astype(F32)
    first = _sibling_mask(r, c, 1)
    xs = [eye - jnp.where(first, low, 0.0) for low in lows]
    s = 2
    while s < CHUNK:
        sib = _sibling_mask(r, c, s)
        if s % SUBLANES:
            ts = [_mm(jnp.where(sib, low, 0.0), x) for low, x in zip(lows, xs)]
            us = [_mm(x, t) for x, t in zip(xs, ts)]
            xs = [x - u for x, u in zip(xs, us)]
        else:
            blank = jnp.zeros((s, n), F32)

            def odd(a, s=s):
                return jnp.concatenate([a[g * s:(g + 1) * s] for g in range(1, n // s, 2)], axis=0)

            def spread(a, s=s, blank=blank):
                return jnp.concatenate([piece for g in range(n // (2 * s)) for piece in (blank, a[g * s:(g + 1) * s])],
                                       axis=0)

            ts = [_mm(odd(jnp.where(sib, low, 0.0)), x) for low, x in zip(lows, xs)]
            us = [_mm(odd(x), spread(t)) for x, t in zip(xs, ts)]
            xs = [x - spread(u) for x, u in zip(xs, us)]
        s *= 2
    return xs


def _stack_heads(z):
    lane = lax.broadcasted_iota(jnp.int32, z.shape, 1)
    first = lane < (LANES // 2)
    return jnp.concatenate([jnp.where(first, z, 0.0), jnp.where(first, 0.0, z)], axis=0)


def _block_causal(n, strict):
    r, c = _iota2((n, n))
    same = (r >> 6) == (c >> 6)
    return same & ((r > c) if strict else (r >= c))


FF_CHUNK = 1408


def _swiglu_ln(x, w1_ref, w3_ref, w2_ref, g, b):
    xb = x.astype(BF16)
    acc = None
    for c0 in range(0, D_FF, FF_CHUNK):
        h1 = jnp.dot(xb, w1_ref[:, c0:c0 + FF_CHUNK], preferred_element_type=F32)
        h3 = jnp.dot(xb, w3_ref[:, c0:c0 + FF_CHUNK], preferred_element_type=F32)
        act = (h1 * _sigmoid(h1) * h3).astype(BF16)
        part = jnp.dot(act, w2_ref[c0:c0 + FF_CHUNK, :], preferred_element_type=F32)
        acc = part if acc is None else acc + part
    return _layer_norm(DEEPNORM_ALPHA * x + MACARON_W * acc, g, b)


def _ffn_body(x_ref, w1_ref, w3_ref, w2_ref, g_ref, b_ref, o_ref):
    o_ref[...] = _swiglu_ln(x_ref[...], w1_ref, w3_ref, w2_ref, g_ref[...], b_ref[...])


def _ffn_ple_body(x_ref, p_ref, w1_ref, w3_ref, w2_ref, g_ref, b_ref, wg_ref, wp_ref, g2_ref, b2_ref, o_ref):
    x = _swiglu_ln(x_ref[...], w1_ref, w3_ref, w2_ref, g_ref[...], b_ref[...])
    gate = _sigmoid(jnp.dot(x.astype(BF16), wg_ref[...], preferred_element_type=F32))
    emb = jnp.dot(p_ref[...].astype(BF16), wp_ref[...], preferred_element_type=F32)
    o_ref[...] = _layer_norm(DEEPNORM_ALPHA * x + gate * emb, g2_ref[...], b2_ref[...])


def _token_rows(x, tm):
    return x, pl.BlockSpec((tm, x.shape[1]), lambda i: (i, 0))


def _ffn_call(x, params, tm):
    n = x.shape[0]
    return _call(_ffn_body, (n // tm,), ("arbitrary",), [_token_rows(x, tm)], params,
                 pl.BlockSpec((tm, D_MODEL), lambda i: (i, 0)), jax.ShapeDtypeStruct((n, D_MODEL), F32), [], "ffn_ln")


def _ffn_ple_call(x, p, layer, params, tm):
    n = x.shape[0]
    p_rows = (p, pl.BlockSpec((None, tm, PLE_DIM), lambda i: (layer, i, 0)))
    return _call(_ffn_ple_body, (n // tm,), ("arbitrary",), [_token_rows(x, tm), p_rows], params,
                 pl.BlockSpec((tm, D_MODEL), lambda i: (i, 0)), jax.ShapeDtypeStruct((n, D_MODEL), F32), [], "ffn_ple_ln")


def _merge_body(x_ref, orw_ref, ogla_ref, ogdn_ref, wgate_ref, wbr_ref, wo_ref, g_ref, b_ref, o_ref):
    x = x_ref[...]
    xb = x.astype(BF16)
    merged = None
    for n, br_ref in enumerate((orw_ref, ogla_ref, ogdn_ref)):
        gate = _sigmoid(jnp.dot(xb, wgate_ref[:, n * D_MODEL:(n + 1) * D_MODEL], preferred_element_type=F32))
        term = gate * jnp.dot(br_ref[...], wbr_ref[n], preferred_element_type=F32)
        merged = term if merged is None else merged + term
    mix = jnp.dot(merged.astype(BF16), wo_ref[...], preferred_element_type=F32)
    o_ref[...] = _layer_norm(DEEPNORM_ALPHA * x + mix, g_ref[...], b_ref[...])


def _merge_call(x, branches, params, tm):
    n = x.shape[0]
    return _call(_merge_body, (n // tm,), ("arbitrary",), [_token_rows(t, tm) for t in (x,) + tuple(branches)], params,
                 pl.BlockSpec((tm, D_MODEL), lambda i: (i, 0)), jax.ShapeDtypeStruct((n, D_MODEL), F32), [], "merge_ln")


def _gdn_body(x_ref, w_ref, cw_ref, alog_ref, dtb_ref, ng_ref, o_ref,
              ext_ref, q_s, k_s, v_s, g_s, beta_s, gate_s, st_ref, *, tb):
    t_idx = pl.program_id(1)

    @pl.when(t_idx == 0)
    def _():
        ext_ref[0:8, :] = jnp.zeros((8, GDN_QKV), F32)
        st_ref[...] = jnp.zeros_like(st_ref)

    @pl.when(t_idx != 0)
    def _():
        ext_ref[0:8, :] = ext_ref[tb:tb + 8, :]

    xb = x_ref[...].astype(BF16)
    ext_ref[8:tb + 8, :] = jnp.dot(xb, w_ref[:, 0:GDN_QKV], preferred_element_type=F32)
    gate_s[...] = jnp.dot(xb, w_ref[:, GDN_QKV:GDN_QKV + GDN_HEADS * GDN_DV], preferred_element_type=F32)

    hw = GDN_HEADS * GDN_DK
    for grp, dst in enumerate((q_s, k_s, v_s)):
        cols = slice(grp * hw, (grp + 1) * hw)
        acc = ext_ref[8:tb + 8, cols] * cw_ref[GDN_CONV - 1:GDN_CONV, cols]
        for d in range(1, GDN_CONV):
            acc = acc + ext_ref[8 - d:8 - d + tb, cols] * cw_ref[GDN_CONV - 1 - d:GDN_CONV - d, cols]
        act = acc * _sigmoid(acc)
        if grp == 2:
            dst[...] = act
        else:
            scale = GDN_DK ** -0.5 if grp == 0 else 1.0
            for h in range(GDN_HEADS):
                t = act[:, h * GDN_DK:(h + 1) * GDN_DK]
                dst[:, h * GDN_DK:(h + 1) * GDN_DK] = t * (lax.rsqrt(jnp.sum(t * t, -1, keepdims=True) + L2_EPS) * scale)

    ab = jnp.dot(xb, w_ref[:, GDN_QKV + GDN_HEADS * GDN_DV:GDN_PAD], preferred_element_type=F32)
    g_s[...] = _chunk_cumsum(-jnp.exp(alog_ref[...]) * _softplus(ab + dtb_ref[...]))
    beta_s[...] = _sigmoid(ab)

    n2 = 2 * CHUNK
    ng = ng_ref[...]

    cpi = GDN_CHUNKS_PER_STEP
    jobs = [(j, pr) for j in range(cpi) for pr in range(GDN_HEADS // 2)]

    def pair_rows(src, rows, pr, width):
        return jnp.concatenate([src[rows, h * width:(h + 1) * width] for h in (2 * pr, 2 * pr + 1)], axis=0)

    def pair_col(a, pr, off):
        return jnp.concatenate([a[:, off + h:off + h + 1] for h in (2 * pr, 2 * pr + 1)], axis=0)

    def step(it, carry):
        rows = [pl.ds(pl.multiple_of(it * (cpi * CHUNK) + j * CHUNK, CHUNK), CHUNK) for j in range(cpi)]
        causal = _block_causal(n2, strict=False)
        strict = _block_causal(n2, strict=True)
        gcs = [g_s[rw, :] for rw in rows]
        betas = [beta_s[rw, :] for rw in rows]
        gi = [pair_col(gcs[j], pr, 0) for j, pr in jobs]
        bcol = [pair_col(betas[j], pr, GDN_HEADS) for j, pr in jobs]
        qs = [pair_rows(q_s, rows[j], pr, GDN_DK) for j, pr in jobs]
        ks = [pair_rows(k_s, rows[j], pr, GDN_DK) for j, pr in jobs]
        vs = [pair_rows(v_s, rows[j], pr, GDN_DV) for j, pr in jobs]
        dec = []
        for g in gi:
            gi_b = jnp.broadcast_to(g, (n2, n2))
            diff = gi_b - gi_b.T
            dec.append(jnp.where(causal, jnp.exp(jnp.where(causal, diff, 0.0)), 0.0))
        kb = [k * b for k, b in zip(ks, bcol)]
        gram = [_mm_nt(jnp.concatenate([a, q], axis=0), k) for a, q, k in zip(kb, qs, ks)]
        low = [jnp.where(strict, g[0:n2] * d, 0.0) for g, d in zip(gram, dec)]
        att = [g[n2:2 * n2] * d for g, d in zip(gram, dec)]
        t_inv = _unit_lower_inverses(low, n2)
        eg = [jnp.exp(g) for g in gi]
        wu = [_mm(t, jnp.concatenate([a * e, v * b], axis=1))
              for t, a, e, v, b in zip(t_inv, kb, eg, vs, bcol)]
        aw = [_mm(a, x) for a, x in zip(att, wu)]
        first = lax.broadcasted_iota(jnp.int32, (GDN_DK, n2), 1) < CHUNK
        g_end = [jnp.concatenate([jnp.broadcast_to(gcs[j][CHUNK - 1:CHUNK, h:h + 1], (CHUNK, 1))
                                  for h in (2 * pr, 2 * pr + 1)], axis=0) for j, pr in jobs]
        k_hat_t = [(k * jnp.exp(ge - g)).T for k, ge, g in zip(ks, g_end, gi)]
        kwu2 = [_mm(jnp.concatenate([jnp.where(first, kt, 0.0), jnp.where(first, 0.0, kt)], axis=0), x)
                for kt, x in zip(k_hat_t, wu)]
        per_head = {}
        for idx, (j, pr) in enumerate(jobs):
            for half, h in enumerate((2 * pr, 2 * pr + 1)):
                sl = slice(half * CHUNK, (half + 1) * CHUNK)
                kwu = kwu2[idx][half * GDN_DK:(half + 1) * GDN_DK]
                q_eff = qs[idx][sl] * eg[idx][sl] - aw[idx][sl, 0:GDN_DV]
                g_dec = jnp.exp(gcs[j][CHUNK - 1:CHUNK, h:h + 1])
                per_head[(j, h)] = (q_eff, aw[idx][sl, GDN_DV:2 * GDN_DV], g_dec, kwu)
        sts = [st_ref[h] for h in range(GDN_HEADS)]
        for j in range(cpi):
            for h in range(GDN_HEADS):
                q_eff, o_loc, g_dec, kwu = per_head[(j, h)]
                st = sts[h]
                both = _mm(jnp.concatenate([q_eff, kwu[:, 0:GDN_DV]], axis=0), st)
                o = both[0:CHUNK] + o_loc
                sts[h] = g_dec * st - both[CHUNK:CHUNK + GDN_DK] + kwu[:, GDN_DV:2 * GDN_DV]
                o = o * lax.rsqrt(jnp.mean(o * o, -1, keepdims=True) + HEAD_NORM_EPS) * ng
                gate = gate_s[rows[j], h * GDN_DV:(h + 1) * GDN_DV]
                o_ref[rows[j], h * GDN_DV:(h + 1) * GDN_DV] = (o * gate * _sigmoid(gate)).astype(o_ref.dtype)
        for h in range(GDN_HEADS):
            st_ref[h] = sts[h]
        return carry

    lax.fori_loop(0, tb // (cpi * CHUNK), step, 0)


def _mixer_call(body, x, params, width, scratch, bsz, seq, tb, name):
    nt = seq // tb
    rows = lambda b, t: (b * nt + t, 0)
    return _call(functools.partial(body, tb=tb), (bsz, nt), ("arbitrary", "arbitrary"),
                 [(x, pl.BlockSpec((tb, D_MODEL), rows))], params,
                 pl.BlockSpec((tb, width), rows), jax.ShapeDtypeStruct((bsz * seq, width), BF16), scratch, name)


def _gdn_call(x, params, bsz, seq, tb):
    wide = pltpu.VMEM((tb, GDN_HEADS * GDN_DK), F32)
    narrow = pltpu.VMEM((tb, LANES), F32)
    scratch = [pltpu.VMEM((tb + 8, GDN_QKV), F32), wide, wide, wide, narrow, narrow, wide,
               pltpu.VMEM((GDN_HEADS, GDN_DK, GDN_DV), F32)]
    return _mixer_call(_gdn_body, x, params, GDN_HEADS * GDN_DV, scratch, bsz, seq, tb, "gated_deltanet")


def _gla_level_arg(s, la, b, ri):
    c = CHUNK
    if s >= 4:
        ref = jnp.concatenate(
            [jnp.broadcast_to(b[g * 2 * s + s - 1:g * 2 * s + s, :], (2 * s, LANES)) for g in range(c // (2 * s))],
            axis=0)
        return -jnp.abs(b - ref)
    if s == 2:
        m = ri & 3
        up = pltpu.roll(la, c - 1, 0)
        dn = pltpu.roll(la, 1, 0)
        return jnp.where(m == 0, up, jnp.where(m == 1, 0.0, jnp.where(m == 2, la, la + dn)))
    return jnp.where((ri & 1) == 1, la, 0.0)


def _gla_scores(qs, ks, las, bs):
    c = CHUNK
    ri = lax.broadcasted_iota(jnp.int32, (c, LANES), 0)
    r2, c2 = _iota2((2 * c, c))
    r2 = r2 & (c - 1)
    diag = r2 == c2
    scores = [jnp.where(diag, _mm_nt(_stack_heads(q), k), 0.0) for q, k in zip(qs, ks)]
    s = c // 2
    while s >= 1:
        row_side = ((ri >> (s.bit_length() - 1)) & 1) == 1
        sib = _sibling_mask(r2, c2, s)
        es = [jnp.exp(_gla_level_arg(s, la, b, ri)) for la, b in zip(las, bs)]
        part = [_mm_nt(_stack_heads(jnp.where(row_side, q * e, 0.0)), jnp.where(row_side, 0.0, k * e))
                for q, k, e in zip(qs, ks, es)]
        scores = [sc + jnp.where(sib, p, 0.0) for sc, p in zip(scores, part)]
        s //= 2
    return scores


def _gla_body(x_ref, w_ref, w2_ref, gkb_ref, ng_ref, o_ref, h_s, la_s, b_s, st_ref, *, tb):
    t_idx = pl.program_id(1)

    @pl.when(t_idx == 0)
    def _():
        st_ref[...] = jnp.zeros_like(st_ref)

    hk = GLA_HEADS * GLA_DK
    hv = GLA_HEADS * GLA_DV
    xb = x_ref[...].astype(BF16)
    h_s[...] = jnp.dot(xb, w_ref[:, 0:2 * hk + 2 * hv], preferred_element_type=F32)
    gk_lo = jnp.dot(xb, w_ref[:, 2 * hk + 2 * hv:GLA_PAD], preferred_element_type=F32)
    z = _mm_f32(gk_lo, w2_ref[...]) + gkb_ref[...]
    la = (jnp.minimum(z, 0.0) - jnp.log(1.0 + jnp.exp(-jnp.abs(z)))) * (1.0 / GLA_GATE_NORM)
    la_s[...] = la
    b_s[...] = _chunk_cumsum(la)
    ng = ng_ref[...]
    cpi = GLA_CHUNKS_PER_STEP
    npair = GLA_HEADS // 2
    jobs = [(j, p) for j in range(cpi) for p in range(npair)]

    def step(it, carry):
        rows = [pl.ds(pl.multiple_of(it * (cpi * CHUNK) + j * CHUNK, CHUNK), CHUNK) for j in range(cpi)]
        row_first = lax.broadcasted_iota(jnp.int32, (LANES, GLA_DV), 0) < GLA_DK
        qs = [h_s[rows[j], p * LANES:(p + 1) * LANES] * (GLA_DK ** -0.5) for j, p in jobs]
        ks = [h_s[rows[j], hk + p * LANES:hk + (p + 1) * LANES] for j, p in jobs]
        las = [la_s[rows[j], p * LANES:(p + 1) * LANES] for j, p in jobs]
        bs = [b_s[rows[j], p * LANES:(p + 1) * LANES] for j, p in jobs]
        vpair = [h_s[rows[j], 2 * hk + 2 * p * GLA_DV:2 * hk + (2 * p + 2) * GLA_DV] for j, p in jobs]
        scores = _gla_scores(qs, ks, las, bs)
        q_in = [_stack_heads(q * jnp.exp(b)) for q, b in zip(qs, bs)]
        bts = [b.T for b in bs]
        b_end = [bt[:, CHUNK - 1:CHUNK] for bt in bts]
        upd = [_mm(k.T * jnp.exp(be - bt), vp) for k, be, bt, vp in zip(ks, b_end, bts, vpair)]
        intra = [[_mm(sc[half * CHUNK:(half + 1) * CHUNK], vp[:, half * GLA_DV:(half + 1) * GLA_DV])
                  for half in range(2)] for sc, vp in zip(scores, vpair)]
        sts = [st_ref[p] for p in range(npair)]
        for idx, (j, p) in enumerate(jobs):
            st = sts[p]
            inter = _mm(q_in[idx], st)
            sts[p] = st * jnp.exp(b_end[idx]) + jnp.where(row_first, upd[idx][:, 0:GLA_DV], upd[idx][:, GLA_DV:2 * GLA_DV])
            for half in range(2):
                h = 2 * p + half
                o = intra[idx][half] + inter[half * CHUNK:(half + 1) * CHUNK]
                o = o * lax.rsqrt(jnp.mean(o * o, -1, keepdims=True) + HEAD_NORM_EPS) * ng
                gate = h_s[rows[j], 2 * hk + hv + h * GLA_DV:2 * hk + hv + (h + 1) * GLA_DV]
                o_ref[rows[j], h * GLA_DV:(h + 1) * GLA_DV] = (o * gate * _sigmoid(gate)).astype(o_ref.dtype)
        for p in range(npair):
            st_ref[p] = sts[p]
        return carry

    lax.fori_loop(0, tb // (cpi * CHUNK), step, 0)


def _gla_call(x, params, bsz, seq, tb):
    keys = pltpu.VMEM((tb, GLA_HEADS * GLA_DK), F32)
    scratch = [pltpu.VMEM((tb, GLA_PAD - LANES), F32), keys, keys,
               pltpu.VMEM((GLA_HEADS // 2, 2 * GLA_DK, GLA_DV), F32)]
    return _mixer_call(_gla_body, x, params, GLA_HEADS * GLA_DV, scratch, bsz, seq, tb, "gated_linear_attention")


def _seg_sum(x, ones_bd):
    n = x.shape[0]
    both = jnp.dot(jnp.concatenate(_split_bf16(x, 2), axis=0), ones_bd, preferred_element_type=F32)
    return both[0:n] + both[n:2 * n]


def _rw_body(x_ref, w_ref, mu_ref, w0_ref, w2_ref, a0_ref, a2_ref, g2_ref, kk_ref, ka_ref, rk_ref, gng_ref, gnb_ref,
             o_ref, ext_ref, r_s, k_s, v_s, ld_s, ci_s, an_s, bn_s, g_s, st_ref, *, tb):
    t_idx = pl.program_id(1)

    @pl.when(t_idx == 0)
    def _():
        ext_ref[0:8, :] = jnp.zeros((8, RW_PAD), F32)
        st_ref[...] = jnp.zeros_like(st_ref)

    @pl.when(t_idx != 0)
    def _():
        ext_ref[0:8, :] = ext_ref[tb:tb + 8, :]

    w = RW_WIDTH
    n2 = 2 * CHUNK
    cpi = RW_CHUNKS_PER_STEP
    sb = cpi * CHUNK
    npair = RW_HEADS // 2
    jobs = [(j, p) for j in range(cpi) for p in range(npair)]
    r2, c2 = _iota2((LANES, LANES))
    ones_bd = jnp.where((r2 >> 6) == (c2 >> 6), 1.0, 0.0).astype(BF16)
    strict = _block_causal(n2, strict=True)
    incl = _block_causal(n2, strict=False)
    zero = jnp.zeros((n2, LANES), F32)

    ext_ref[8:tb + 8, :] = jnp.dot(x_ref[...].astype(BF16), w_ref[...], preferred_element_type=F32)

    def shifted(cols):
        cur = ext_ref[8:tb + 8, cols]
        return cur + (ext_ref[7:tb + 7, cols] - cur) * mu_ref[:, cols]

    lo = shifted(slice(3 * w, 3 * w + LANES))
    wv = -_softplus(-(w0_ref[...] + _mm_f32(jnp.tanh(lo), w2_ref[...]))) - 0.5
    ld = -jnp.exp(wv)
    ld_s[...] = ld
    ci_s[...] = _chunk_cumsum(ld)
    a = _sigmoid(a0_ref[...] + _mm_f32(lo, a2_ref[...]))
    glo = shifted(slice(3 * w + LANES, RW_PAD))
    g_s[...] = _mm(_sigmoid(glo), g2_ref[...])
    r_s[...] = shifted(slice(0, w))
    v_s[...] = shifted(slice(2 * w, 3 * w))
    for p in range(npair):
        lanes = slice(p * LANES, (p + 1) * LANES)
        k = shifted(slice(w + p * LANES, w + (p + 1) * LANES))
        t = k * kk_ref[:, lanes]
        kkn = t * lax.rsqrt(_seg_sum(t * t, ones_bd) + L2_EPS)
        an_s[:, lanes] = -kkn
        bn_s[:, lanes] = kkn * a[:, lanes]
        k_s[:, lanes] = k * (1.0 + (a[:, lanes] - 1.0) * ka_ref[:, lanes])

    def step(it, carry):
        rows = [pl.ds(pl.multiple_of(it * sb + j * CHUNK, CHUNK), CHUNK) for j in range(cpi)]

        def load(src):
            return [src[rows[j], p * LANES:(p + 1) * LANES] for j, p in jobs]

        r, kx, v, ld, ci, an, bn = (load(src) for src in (r_s, k_s, v_s, ld_s, ci_s, an_s, bn_s))
        e_in = [jnp.exp(-c) for c in ci]
        e_out = [jnp.exp(c[CHUNK - 1:CHUNK, :] - c) for c in ci]
        a_s = [_stack_heads(a * jnp.exp(c - t)) for a, c, t in zip(an, ci, ld)]
        r_t = [_stack_heads(x * jnp.exp(c)) for x, c in zip(r, ci)]
        b_hat = [_stack_heads(b * e) for b, e in zip(bn, e_out)]
        k_hat = [_stack_heads(k * e) for k, e in zip(kx, e_out)]
        v_st = [_stack_heads(x) for x in v]
        gram = [_mm_nt(jnp.concatenate([a, x], axis=0),
                       jnp.concatenate([_stack_heads(b * e), _stack_heads(k * e)], axis=0))
                for a, x, b, k, e in zip(a_s, r_t, bn, kx, e_in)]
        a_ab = [jnp.where(strict, g[0:n2, 0:n2], 0.0) for g in gram]
        a_ak = [jnp.where(strict, g[0:n2, n2:2 * n2], 0.0) for g in gram]
        a_rbk = [jnp.concatenate([jnp.where(incl, g[n2:2 * n2, 0:n2], 0.0),
                                  jnp.where(incl, g[n2:2 * n2, n2:2 * n2], 0.0)], axis=1) for g in gram]
        t_inv = _unit_lower_inverses([-a for a in a_ab], n2)
        akv = [_mm(a, x) for a, x in zip(a_ak, v_st)]
        x_au = [_mm(t, jnp.concatenate([a, b], axis=1)) for t, a, b in zip(t_inv, a_s, akv)]
        bk_t = [jnp.concatenate([b.T, k.T], axis=1) for b, k in zip(b_hat, k_hat)]
        zz = [_mm(jnp.concatenate([a, t], axis=0), jnp.concatenate([x, jnp.concatenate([zero, u], axis=1)], axis=0))
              for a, t, x, u in zip(a_rbk, bk_t, x_au, v_st)]
        rw_p = [jnp.concatenate([x + y[0:n2, 0:LANES], y[n2:2 * n2, 0:LANES]], axis=0) for x, y in zip(r_t, zz)]
        g_col = [jnp.exp(c.T[:, CHUNK - 1:CHUNK]) for c in ci]
        sts = [st_ref[p] for p in range(npair)]
        for idx, (j, p) in enumerate(jobs):
            lanes = slice(p * LANES, (p + 1) * LANES)
            st = sts[p]
            both = _mm(rw_p[idx], st)
            ys = both[0:n2] + zz[idx][0:n2, LANES:2 * LANES]
            sts[p] = g_col[idx] * st + both[n2:2 * n2] + zz[idx][n2:2 * n2, LANES:2 * LANES]
            y = ys[0:CHUNK] + ys[CHUNK:n2]
            sums = _seg_sum(jnp.concatenate([y, r[idx] * kx[idx] * rk_ref[:, lanes]], axis=0), ones_bd)
            d = y - sums[0:CHUNK] * (1.0 / RW_HEAD)
            var = _seg_sum(d * d, ones_bd) * (1.0 / RW_HEAD)
            yn = d * lax.rsqrt(var + RW_GN_EPS) * gng_ref[:, lanes] + gnb_ref[:, lanes]
            bonus = sums[CHUNK:n2] * v[idx]
            o_ref[rows[j], lanes] = ((yn + bonus) * g_s[rows[j], lanes]).astype(o_ref.dtype)
        for p in range(npair):
            st_ref[p] = sts[p]
        return carry

    lax.fori_loop(0, tb // sb, step, 0)


def _rw_call(x, params, bsz, seq, tb):
    scratch = ([pltpu.VMEM((tb + 8, RW_PAD), F32)] + [pltpu.VMEM((tb, RW_WIDTH), F32)] * 8
               + [pltpu.VMEM((RW_HEADS // 2, LANES, LANES), F32)])
    return _mixer_call(_rw_body, x, params, RW_WIDTH, scratch, bsz, seq, tb, "rwkv7_time_mix")


def _pad_to(a, size, axis):
    pad = [(0, 0)] * a.ndim
    pad[axis] = (0, size - a.shape[axis])
    return jnp.pad(a, pad)


def _mixer_in_weights(w_in):
    o = 0
    rw = w_in[..., o:o + RW_IN]
    o += RW_IN
    gla = w_in[..., o:o + GLA_IN]
    o += GLA_IN
    gdn = w_in[..., o:o + GDN_IN]
    o += GDN_IN
    gate = w_in[..., o:]
    qkv_w = 2 * GLA_HEADS * GLA_DK + GLA_HEADS * GLA_DV
    gla_p = jnp.concatenate([gla[..., 0:qkv_w], gla[..., qkv_w + GLA_GATE_LORA:], gla[..., qkv_w:qkv_w + GLA_GATE_LORA]], axis=-1)
    gdn_p = jnp.concatenate([gdn[..., 0:GDN_QKV], gdn[..., GDN_QKV + 2 * GDN_HEADS:], gdn[..., GDN_QKV:GDN_QKV + 2 * GDN_HEADS]], axis=-1)
    return (_pad_to(rw, RW_PAD, 2).astype(BF16), _pad_to(gla_p, GLA_PAD, 2).astype(BF16),
            _pad_to(gdn_p, GDN_PAD, 2).astype(BF16), gate.astype(BF16))


def _rows(v):
    return v.reshape(v.shape[:-1] + (1, v.shape[-1]))


def _tiles(n, seq):
    return min(512, n), min(512, seq)


def kernel(x, p, ln_g, ln_b, ffn_w1, ffn_w3, ffn_w2, w_in, rw_mu, rw_w0, rw_w2, rw_a0, rw_a2, rw_g2, rw_k_k, rw_k_a, rw_r_k, rw_gn_g, rw_gn_b, gla_gk_w2, gla_gk_b, gla_norm_g, gdn_conv_w, gdn_a_log, gdn_dt_bias, gdn_norm_g, w_branch, w_o, ple_w_gate, ple_w_proj):
    bsz, seq, d = x.shape
    depth = p.shape[0]
    n = bsz * seq
    tm, tb = _tiles(n, seq)
    x = x.reshape(n, d)
    p = p.reshape(depth, n, PLE_DIM)

    w1, w3, w2 = ffn_w1.astype(BF16), ffn_w3.astype(BF16), ffn_w2.astype(BF16)
    w_rw, w_gla, w_gdn, w_gate = _mixer_in_weights(w_in)
    w_br, w_out = w_branch.astype(BF16), w_o.astype(BF16)
    w_pg, w_pp = ple_w_gate.astype(BF16), ple_w_proj.astype(BF16)
    lng, lnb = _rows(ln_g), _rows(ln_b)
    rw_vecs = [_rows(v) for v in (rw_k_k, rw_k_a, rw_r_k.reshape(depth, RW_WIDTH), rw_gn_g, rw_gn_b)]
    rw_mu_p = _rows(_pad_to(rw_mu, RW_PAD, 1))
    rw_w0_r, rw_a0_r = _rows(rw_w0), _rows(rw_a0)
    rw_w2_p = _pad_to(rw_w2, LANES, 1)
    rw_a2_p = jnp.pad(rw_a2, ((0, 0), (RW_DECAY_LORA, 0), (0, 0)))
    rw_g2_p = _pad_to(rw_g2, 2 * LANES, 1)
    gla_w2_p, gla_b_r, gla_ng = _pad_to(gla_gk_w2, LANES, 1), _rows(gla_gk_b), _rows(gla_norm_g)
    gdn_alog, gdn_dtb, gdn_ng = _rows(_pad_to(gdn_a_log, LANES, 1)), _rows(_pad_to(gdn_dt_bias, LANES, 1)), _rows(gdn_norm_g)

    for i in range(depth):
        x = _ffn_call(x, [_param(w1, i, 0), _param(w3, i, 0), _param(w2, i, 0), _param(lng, i, 0), _param(lnb, i, 0)], tm)
        o_rw = _rw_call(x, [_param(w_rw, i), _param(rw_mu_p, i), _param(rw_w0_r, i), _param(rw_w2_p, i),
                            _param(rw_a0_r, i), _param(rw_a2_p, i), _param(rw_g2_p, i)]
                        + [_param(v, i) for v in rw_vecs], bsz, seq, tb)
        o_gla = _gla_call(x, [_param(w_gla, i), _param(gla_w2_p, i), _param(gla_b_r, i), _param(gla_ng, i)], bsz, seq, tb)
        o_gdn = _gdn_call(x, [_param(w_gdn, i), _param(gdn_conv_w, i), _param(gdn_alog, i), _param(gdn_dtb, i),
                              _param(gdn_ng, i)], bsz, seq, tb)
        x = _merge_call(x, (o_rw, o_gla, o_gdn), [_param(w_gate, i), _param(w_br, i), _param(w_out, i),
                                                  _param(lng, i, 1), _param(lnb, i, 1)], tm)
        x = _ffn_ple_call(x, p, i, [_param(w1, i, 1), _param(w3, i, 1), _param(w2, i, 1), _param(lng, i, 2), _param(lnb, i, 2),
                                    _param(w_pg, i), _param(w_pp, i), _param(lng, i, 3), _param(lnb, i, 3)], tm)
    return x.reshape(bsz, seq, d)
```

```python
import functools

import jax
import jax.numpy as jnp
from jax import lax
from jax.experimental import pallas as pl
from jax.experimental.pallas import tpu as pltpu

F32 = jnp.float32
BF16 = jnp.bfloat16

D_MODEL = 1024
DEPTH = 2
PLE_DIM = 256
D_FF = 2816
LN_EPS = 1e-5
DEEPNORM_ALPHA = (2 * DEPTH) ** 0.25
MACARON_W = 0.5

RW_HEADS = 8
RW_HEAD = 64
RW_WIDTH = RW_HEADS * RW_HEAD
RW_DECAY_LORA = 64
RW_AAA_LORA = 64
RW_GATE_LORA = 160
RW_GN_EPS = 1e-5 * RW_HEAD
RW_IN = 3 * RW_WIDTH + RW_DECAY_LORA + RW_AAA_LORA + RW_GATE_LORA

GLA_HEADS = 4
GLA_DK = 64
GLA_DV = 128
GLA_GATE_LORA = 16
GLA_GATE_NORM = 16.0
GLA_IN = 2 * GLA_HEADS * GLA_DK + 2 * GLA_HEADS * GLA_DV + GLA_GATE_LORA

GDN_HEADS = 4
GDN_DK = 128
GDN_DV = 128
GDN_CONV = 4
GDN_QKV = GDN_HEADS * (2 * GDN_DK + GDN_DV)
GDN_IN = GDN_QKV + 2 * GDN_HEADS + GDN_HEADS * GDN_DV

HEAD_NORM_EPS = 1e-6
L2_EPS = 1e-6
N_BRANCH = 3
BRANCH_WIDTH = 512

LANES = 128
SUBLANES = 8
CHUNK = 64
GDN_CHUNKS_PER_STEP = 8
RW_CHUNKS_PER_STEP = 4
GLA_CHUNKS_PER_STEP = 4
RW_PAD = 1920
GLA_PAD = 1664
GDN_PAD = 2176
MIX_PAD = RW_PAD + GLA_PAD + GDN_PAD
VMEM_LIMIT = 56 * 1024 * 1024


def _param(arr, *prefix):
    tail = arr.shape[len(prefix):]
    zeros = (0,) * len(tail)
    spec = pl.BlockSpec((None,) * len(prefix) + tail, lambda *_: prefix + zeros, pipeline_mode=pl.Buffered(1))
    return arr, spec


def _call(body, grid, sem, streamed, params, out_specs, out_shape, scratch, name):
    ops = list(streamed) + list(params)
    return pl.pallas_call(
        body,
        grid=grid,
        in_specs=[spec for _, spec in ops],
        out_specs=out_specs,
        out_shape=out_shape,
        scratch_shapes=scratch,
        compiler_params=pltpu.CompilerParams(dimension_semantics=sem, vmem_limit_bytes=VMEM_LIMIT),
        name=name,
    )(*[arr for arr, _ in ops])


def _mm(a, b):
    return jnp.dot(a.astype(BF16), b.astype(BF16), preferred_element_type=F32)


def _mm_nt(a, b):
    return lax.dot_general(a.astype(BF16), b.astype(BF16), (((1,), (1,)), ((), ())),
                           preferred_element_type=F32)


def _split_bf16(x, parts):
    out = []
    for _ in range(parts - 1):
        piece = x.astype(BF16)
        out.append(piece)
        x = x - piece.astype(F32)
    out.append(x.astype(BF16))
    return out


def _mm_f32(a, b):
    a_hi, a_lo = _split_bf16(a, 2)
    b_hi, b_lo = _split_bf16(b, 2)
    dot = functools.partial(jnp.dot, preferred_element_type=F32)
    return dot(a_hi, b_hi) + (dot(a_hi, b_lo) + dot(a_lo, b_hi))


def _chunk_cumsum(x):
    tril = _tril_f32(CHUNK).astype(BF16)
    out = []
    for r0 in range(0, x.shape[0], CHUNK):
        pieces = _split_bf16(x[r0:r0 + CHUNK], 3)
        acc = jnp.dot(tril, pieces[2], preferred_element_type=F32)
        acc = acc + jnp.dot(tril, pieces[1], preferred_element_type=F32)
        out.append(acc + jnp.dot(tril, pieces[0], preferred_element_type=F32))
    return jnp.concatenate(out, axis=0)


def _layer_norm(y, g, b):
    m = jnp.mean(y, -1, keepdims=True)
    d = y - m
    var = jnp.mean(d * d, -1, keepdims=True)
    return d * lax.rsqrt(var + LN_EPS) * g + b


def _sigmoid(x):
    return 1.0 / (1.0 + jnp.exp(-x))


def _softplus(x):
    return jnp.maximum(x, 0.0) + jnp.log(1.0 + jnp.exp(-jnp.abs(x)))


def _iota2(shape):
    return (lax.broadcasted_iota(jnp.int32, shape, 0), lax.broadcasted_iota(jnp.int32, shape, 1))


def _tril_f32(n):
    r, c = _iota2((n, n))
    return jnp.where(r >= c, 1.0, 0.0).astype(F32)


def _sibling_mask(r, c, s):
    k = s.bit_length() - 1
    return ((r >> (k + 1)) == (c >> (k + 1))) & (((r >> k) & 1) == 1) & (((c >> k) & 1) == 0)


def _unit_lower_inverses(lows, n):
    r, c = _iota2((n, n))
    eye = jnp.where(r == c, 1.0, 0.0).astype(F32)
    first = _sibling_mask(r, c, 1)
    xs = [eye - jnp.where(first, low, 0.0) for low in lows]
    s = 2
    while s < CHUNK:
        sib = _sibling_mask(r, c, s)
        if s % SUBLANES:
            ts = [_mm(jnp.where(sib, low, 0.0), x) for low, x in zip(lows, xs)]
            us = [_mm(x, t) for x, t in zip(xs, ts)]
            xs = [x - u for x, u in zip(xs, us)]
        else:
            blank = jnp.zeros((s, n), F32)

            def odd(a, s=s):
                return jnp.concatenate([a[g * s:(g + 1) * s] for g in range(1, n // s, 2)], axis=0)

            def spread(a, s=s, blank=blank):
                return jnp.concatenate([piece for g in range(n // (2 * s)) for piece in (blank, a[g * s:(g + 1) * s])],
                                       axis=0)

            ts = [_mm(odd(jnp.where(sib, low, 0.0)), x) for low, x in zip(lows, xs)]
            us = [_mm(odd(x), spread(t)) for x, t in zip(xs, ts)]
            xs = [x - spread(u) for x, u in zip(xs, us)]
        s *= 2
    return xs


def _stack_heads(z):
    lane = lax.broadcasted_iota(jnp.int32, z.shape, 1)
    first = lane < (LANES // 2)
    return jnp.concatenate([jnp.where(first, z, 0.0), jnp.where(first, 0.0, z)], axis=0)


def _block_causal(n, strict):
    r, c = _iota2((n, n))
    same = (r >> 6) == (c >> 6)
    return same & ((r > c) if strict else (r >= c))


FF_CHUNK = 1408


def _swiglu_ln(x, w1_ref, w3_ref, w2_ref, g, b):
    xb = x.astype(BF16)
    acc = None
    for c0 in range(0, D_FF, FF_CHUNK):
        h1 = jnp.dot(xb, w1_ref[:, c0:c0 + FF_CHUNK], preferred_element_type=F32)
        h3 = jnp.dot(xb, w3_ref[:, c0:c0 + FF_CHUNK], preferred_element_type=F32)
        act = (h1 * _sigmoid(h1) * h3).astype(BF16)
        part = jnp.dot(act, w2_ref[c0:c0 + FF_CHUNK, :], preferred_element_type=F32)
        acc = part if acc is None else acc + part
    return _layer_norm(DEEPNORM_ALPHA * x + MACARON_W * acc, g, b)


def _ffn_body(x_ref, w1_ref, w3_ref, w2_ref, g_ref, b_ref, o_ref):
    o_ref[...] = _swiglu_ln(x_ref[...], w1_ref, w3_ref, w2_ref, g_ref[...], b_ref[...])


def _ffn_ple_body(x_ref, p_ref, w1_ref, w3_ref, w2_ref, g_ref, b_ref, wg_ref, wp_ref, g2_ref, b2_ref, o_ref):
    x = _swiglu_ln(x_ref[...], w1_ref, w3_ref, w2_ref, g_ref[...], b_ref[...])
    gate = _sigmoid(jnp.dot(x.astype(BF16), wg_ref[...], preferred_element_type=F32))
    emb = jnp.dot(p_ref[...].astype(BF16), wp_ref[...], preferred_element_type=F32)
    o_ref[...] = _layer_norm(DEEPNORM_ALPHA * x + gate * emb, g2_ref[...], b2_ref[...])


def _token_rows(x, tm):
    return x, pl.BlockSpec((tm, x.shape[1]), lambda i: (i, 0))


def _ffn_call(x, params, tm):
    n = x.shape[0]
    return _call(_ffn_body, (n // tm,), ("arbitrary",), [_token_rows(x, tm)], params,
                 pl.BlockSpec((tm, D_MODEL), lambda i: (i, 0)), jax.ShapeDtypeStruct((n, D_MODEL), F32), [], "ffn_ln")


def _ffn_ple_call(x, p, layer, params, tm):
    n = x.shape[0]
    p_rows = (p, pl.BlockSpec((None, tm, PLE_DIM), lambda i: (layer, i, 0)))
    return _call(_ffn_ple_body, (n // tm,), ("arbitrary",), [_token_rows(x, tm), p_rows], params,
                 pl.BlockSpec((tm, D_MODEL), lambda i: (i, 0)), jax.ShapeDtypeStruct((n, D_MODEL), F32), [], "ffn_ple_ln")


def _merge_body(x_ref, orw_ref, ogla_ref, ogdn_ref, wgate_ref, wbr_ref, wo_ref, g_ref, b_ref, o_ref):
    x = x_ref[...]
    xb = x.astype(BF16)
    merged = None
    for n, br_ref in enumerate((orw_ref, ogla_ref, ogdn_ref)):
        gate = _sigmoid(jnp.dot(xb, wgate_ref[:, n * D_MODEL:(n + 1) * D_MODEL], preferred_element_type=F32))
        term = gate * jnp.dot(br_ref[...], wbr_ref[n], preferred_element_type=F32)
        merged = term if merged is None else merged + term
    mix = jnp.dot(merged.astype(BF16), wo_ref[...], preferred_element_type=F32)
    o_ref[...] = _layer_norm(DEEPNORM_ALPHA * x + mix, g_ref[...], b_ref[...])


def _merge_call(x, branches, params, tm):
    n = x.shape[0]
    return _call(_merge_body, (n // tm,), ("arbitrary",), [_token_rows(t, tm) for t in (x,) + tuple(branches)], params,
                 pl.BlockSpec((tm, D_MODEL), lambda i: (i, 0)), jax.ShapeDtypeStruct((n, D_MODEL), F32), [], "merge_ln")


def _gdn_body(x_ref, w_ref, cw_ref, alog_ref, dtb_ref, ng_ref, o_ref,
              ext_ref, q_s, k_s, v_s, g_s, beta_s, gate_s, st_ref, *, tb):
    t_idx = pl.program_id(1)

    @pl.when(t_idx == 0)
    def _():
        ext_ref[0:8, :] = jnp.zeros((8, GDN_QKV), F32)
        st_ref[...] = jnp.zeros_like(st_ref)

    @pl.when(t_idx != 0)
    def _():
        ext_ref[0:8, :] = ext_ref[tb:tb + 8, :]

    xb = x_ref[...].astype(BF16)
    ext_ref[8:tb + 8, :] = jnp.dot(xb, w_ref[:, 0:GDN_QKV], preferred_element_type=F32)
    gate_s[...] = jnp.dot(xb, w_ref[:, GDN_QKV:GDN_QKV + GDN_HEADS * GDN_DV], preferred_element_type=F32)

    hw = GDN_HEADS * GDN_DK
    for grp, dst in enumerate((q_s, k_s, v_s)):
        cols = slice(grp * hw, (grp + 1) * hw)
        acc = ext_ref[8:tb + 8, cols] * cw_ref[GDN_CONV - 1:GDN_CONV, cols]
        for d in range(1, GDN_CONV):
            acc = acc + ext_ref[8 - d:8 - d + tb, cols] * cw_ref[GDN_CONV - 1 - d:GDN_CONV - d, cols]
        act = acc * _sigmoid(acc)
        if grp == 2:
            dst[...] = act
        else:
            scale = GDN_DK ** -0.5 if grp == 0 else 1.0
            for h in range(GDN_HEADS):
                t = act[:, h * GDN_DK:(h + 1) * GDN_DK]
                dst[:, h * GDN_DK:(h + 1) * GDN_DK] = t * (lax.rsqrt(jnp.sum(t * t, -1, keepdims=True) + L2_EPS) * scale)

    ab = jnp.dot(xb, w_ref[:, GDN_QKV + GDN_HEADS * GDN_DV:GDN_PAD], preferred_element_type=F32)
    g_s[...] = _chunk_cumsum(-jnp.exp(alog_ref[...]) * _softplus(ab + dtb_ref[...]))
    beta_s[...] = _sigmoid(ab)

    n2 = 2 * CHUNK
    ng = ng_ref[...]

    cpi = GDN_CHUNKS_PER_STEP
    jobs = [(j, pr) for j in range(cpi) for pr in range(GDN_HEADS // 2)]

    def pair_rows(src, rows, pr, width):
        return jnp.concatenate([src[rows, h * width:(h + 1) * width] for h in (2 * pr, 2 * pr + 1)], axis=0)

    def pair_col(a, pr, off):
        return jnp.concatenate([a[:, off + h:off + h + 1] for h in (2 * pr, 2 * pr + 1)], axis=0)

    def step(it, carry):
        rows = [pl.ds(pl.multiple_of(it * (cpi * CHUNK) + j * CHUNK, CHUNK), CHUNK) for j in range(cpi)]
        causal = _block_causal(n2, strict=False)
        strict = _block_causal(n2, strict=True)
        gcs = [g_s[rw, :] for rw in rows]
        betas = [beta_s[rw, :] for rw in rows]
        gi = [pair_col(gcs[j], pr, 0) for j, pr in jobs]
        bcol = [pair_col(betas[j], pr, GDN_HEADS) for j, pr in jobs]
        qs = [pair_rows(q_s, rows[j], pr, GDN_DK) for j, pr in jobs]
        ks = [pair_rows(k_s, rows[j], pr, GDN_DK) for j, pr in jobs]
        vs = [pair_rows(v_s, rows[j], pr, GDN_DV) for j, pr in jobs]
        dec = []
        for g in gi:
            gi_b = jnp.broadcast_to(g, (n2, n2))
            diff = gi_b - gi_b.T
            dec.append(jnp.where(causal, jnp.exp(jnp.where(causal, diff, 0.0)), 0.0))
        kb = [k * b for k, b in zip(ks, bcol)]
        gram = [_mm_nt(jnp.concatenate([a, q], axis=0), k) for a, q, k in zip(kb, qs, ks)]
        low = [jnp.where(strict, g[0:n2] * d, 0.0) for g, d in zip(gram, dec)]
        att = [g[n2:2 * n2] * d for g, d in zip(gram, dec)]
        t_inv = _unit_lower_inverses(low, n2)
        eg = [jnp.exp(g) for g in gi]
        wu = [_mm(t, jnp.concatenate([a * e, v * b], axis=1))
              for t, a, e, v, b in zip(t_inv, kb, eg, vs, bcol)]
        aw = [_mm(a, x) for a, x in zip(att, wu)]
        first = lax.broadcasted_iota(jnp.int32, (GDN_DK, n2), 1) < CHUNK
        g_end = [jnp.concatenate([jnp.broadcast_to(gcs[j][CHUNK - 1:CHUNK, h:h + 1], (CHUNK, 1))
                                  for h in (2 * pr, 2 * pr + 1)], axis=0) for j, pr in jobs]
        k_hat_t = [(k * jnp.exp(ge - g)).T for k, ge, g in zip(ks, g_end, gi)]
        kwu2 = [_mm(jnp.concatenate([jnp.where(first, kt, 0.0), jnp.where(first, 0.0, kt)], axis=0), x)
                for kt, x in zip(k_hat_t, wu)]
        per_head = {}
        for idx, (j, pr) in enumerate(jobs):
            for half, h in enumerate((2 * pr, 2 * pr + 1)):
                sl = slice(half * CHUNK, (half + 1) * CHUNK)
                kwu = kwu2[idx][half * GDN_DK:(half + 1) * GDN_DK]
                q_eff = qs[idx][sl] * eg[idx][sl] - aw[idx][sl, 0:GDN_DV]
                g_dec = jnp.exp(gcs[j][CHUNK - 1:CHUNK, h:h + 1])
                per_head[(j, h)] = (q_eff, aw[idx][sl, GDN_DV:2 * GDN_DV], g_dec, kwu)
        sts = [st_ref[h] for h in range(GDN_HEADS)]
        for j in range(cpi):
            for h in range(GDN_HEADS):
                q_eff, o_loc, g_dec, kwu = per_head[(j, h)]
                st = sts[h]
                both = _mm(jnp.concatenate([q_eff, kwu[:, 0:GDN_DV]], axis=0), st)
                o = both[0:CHUNK] + o_loc
                sts[h] = g_dec * st - both[CHUNK:CHUNK + GDN_DK] + kwu[:, GDN_DV:2 * GDN_DV]
                o = o * lax.rsqrt(jnp.mean(o * o, -1, keepdims=True) + HEAD_NORM_EPS) * ng
                gate = gate_s[rows[j], h * GDN_DV:(h + 1) * GDN_DV]
                o_ref[rows[j], h * GDN_DV:(h + 1) * GDN_DV] = (o * gate * _sigmoid(gate)).astype(o_ref.dtype)
        for h in range(GDN_HEADS):
            st_ref[h] = sts[h]
        return carry

    lax.fori_loop(0, tb // (cpi * CHUNK), step, 0)


def _mixer_call(body, x, params, width, scratch, bsz, seq, tb, name):
    nt = seq // tb
    rows = lambda b, t: (b * nt + t, 0)
    return _call(functools.partial(body, tb=tb), (bsz, nt), ("arbitrary", "arbitrary"),
                 [(x, pl.BlockSpec((tb, D_MODEL), rows))], params,
                 pl.BlockSpec((tb, width), rows), jax.ShapeDtypeStruct((bsz * seq, width), BF16), scratch, name)


def _gdn_call(x, params, bsz, seq, tb):
    wide = pltpu.VMEM((tb, GDN_HEADS * GDN_DK), F32)
    narrow = pltpu.VMEM((tb, LANES), F32)
    scratch = [pltpu.VMEM((tb + 8, GDN_QKV), F32), wide, wide, wide, narrow, narrow, wide,
               pltpu.VMEM((GDN_HEADS, GDN_DK, GDN_DV), F32)]
    return _mixer_call(_gdn_body, x, params, GDN_HEADS * GDN_DV, scratch, bsz, seq, tb, "gated_deltanet")


def _gla_level_arg(s, la, b, ri):
    c = CHUNK
    if s >= 4:
        ref = jnp.concatenate(
            [jnp.broadcast_to(b[g * 2 * s + s - 1:g * 2 * s + s, :], (2 * s, LANES)) for g in range(c // (2 * s))],
            axis=0)
        return -jnp.abs(b - ref)
    if s == 2:
        m = ri & 3
        up = pltpu.roll(la, c - 1, 0)
        dn = pltpu.roll(la, 1, 0)
        return jnp.where(m == 0, up, jnp.where(m == 1, 0.0, jnp.where(m == 2, la, la + dn)))
    return jnp.where((ri & 1) == 1, la, 0.0)


def _gla_scores(qs, ks, las, bs, between):
    c = CHUNK
    ri = lax.broadcasted_iota(jnp.int32, (c, LANES), 0)
    r2, c2 = _iota2((2 * c, c))
    r2 = r2 & (c - 1)
    diag = r2 == c2
    scores = [jnp.where(diag, _mm_nt(_stack_heads(q), k), 0.0) for q, k in zip(qs, ks)]
    between()
    s = c // 2
    while s >= 1:
        row_side = ((ri >> (s.bit_length() - 1)) & 1) == 1
        sib = _sibling_mask(r2, c2, s)
        es = [jnp.exp(_gla_level_arg(s, la, b, ri)) for la, b in zip(las, bs)]
        part = [_mm_nt(_stack_heads(jnp.where(row_side, q * e, 0.0)), jnp.where(row_side, 0.0, k * e))
                for q, k, e in zip(qs, ks, es)]
        between()
        scores = [sc + jnp.where(sib, p, 0.0) for sc, p in zip(scores, part)]
        s //= 2
    return scores


def _gla_body(x_ref, w_ref, w2_ref, gkb_ref, ng_ref, o_ref, h_s, la_s, b_s, st_ref, *, tb):
    t_idx = pl.program_id(1)

    @pl.when(t_idx == 0)
    def _():
        st_ref[...] = jnp.zeros_like(st_ref)

    hk = GLA_HEADS * GLA_DK
    hv = GLA_HEADS * GLA_DV
    ng = ng_ref[...]
    cpi = GLA_CHUNKS_PER_STEP
    sb = cpi * CHUNK
    npair = GLA_HEADS // 2
    jobs = [(j, p) for j in range(cpi) for p in range(npair)]
    row_first = lax.broadcasted_iota(jnp.int32, (LANES, GLA_DV), 0) < GLA_DK

    def prep_units(s):
        rows = slice(s * sb, (s + 1) * sb)
        vals = {}

        def proj(c0, c1):
            def run():
                if "xb" not in vals:
                    vals["xb"] = x_ref[rows, :].astype(BF16)
                h_s[rows, c0:c1] = jnp.dot(vals["xb"], w_ref[:, c0:c1], preferred_element_type=F32)
            return run

        def decay():
            gk_lo = jnp.dot(vals["xb"], w_ref[:, 2 * hk + 2 * hv:GLA_PAD], preferred_element_type=F32)
            z = _mm_f32(gk_lo, w2_ref[...]) + gkb_ref[...]
            la = (jnp.minimum(z, 0.0) - jnp.log(1.0 + jnp.exp(-jnp.abs(z)))) * (1.0 / GLA_GATE_NORM)
            la_s[rows, :] = la
            b_s[rows, :] = _chunk_cumsum(la)

        return [proj(c0, c0 + 2 * LANES) for c0 in range(0, 2 * hk + 2 * hv, 2 * LANES)] + [decay]

    def recurrence(s, pending):
        rows = [slice(s * sb + j * CHUNK, s * sb + (j + 1) * CHUNK) for j in range(cpi)]

        def fill():
            if pending:
                pending.pop(0)()
        qs = [h_s[rows[j], p * LANES:(p + 1) * LANES] * (GLA_DK ** -0.5) for j, p in jobs]
        ks = [h_s[rows[j], hk + p * LANES:hk + (p + 1) * LANES] for j, p in jobs]
        las = [la_s[rows[j], p * LANES:(p + 1) * LANES] for j, p in jobs]
        bs = [b_s[rows[j], p * LANES:(p + 1) * LANES] for j, p in jobs]
        vpair = [h_s[rows[j], 2 * hk + 2 * p * GLA_DV:2 * hk + (2 * p + 2) * GLA_DV] for j, p in jobs]
        scores = _gla_scores(qs, ks, las, bs, fill)
        q_in = [_stack_heads(q * jnp.exp(b)) for q, b in zip(qs, bs)]
        bts = [b.T for b in bs]
        b_end = [bt[:, CHUNK - 1:CHUNK] for bt in bts]
        upd = [_mm(k.T * jnp.exp(be - bt), vp) for k, be, bt, vp in zip(ks, b_end, bts, vpair)]
        intra = [[_mm(sc[half * CHUNK:(half + 1) * CHUNK], vp[:, half * GLA_DV:(half + 1) * GLA_DV])
                  for half in range(2)] for sc, vp in zip(scores, vpair)]
        sts = [st_ref[p] for p in range(npair)]
        for idx, (j, p) in enumerate(jobs):
            st = sts[p]
            inter = _mm(q_in[idx], st)
            sts[p] = st * jnp.exp(b_end[idx]) + jnp.where(row_first, upd[idx][:, 0:GLA_DV], upd[idx][:, GLA_DV:2 * GLA_DV])
            for half in range(2):
                h = 2 * p + half
                o = intra[idx][half] + inter[half * CHUNK:(half + 1) * CHUNK]
                o = o * lax.rsqrt(jnp.mean(o * o, -1, keepdims=True) + HEAD_NORM_EPS) * ng
                gate = h_s[rows[j], 2 * hk + hv + h * GLA_DV:2 * hk + hv + (h + 1) * GLA_DV]
                o_ref[rows[j], h * GLA_DV:(h + 1) * GLA_DV] = (o * gate * _sigmoid(gate)).astype(o_ref.dtype)
        for p in range(npair):
            st_ref[p] = sts[p]
        while pending:
            fill()

    for unit in prep_units(0):
        unit()
    for s in range(tb // sb):
        recurrence(s, prep_units(s + 1) if (s + 1) * sb < tb else [])


def _gla_call(x, params, bsz, seq, tb):
    keys = pltpu.VMEM((tb, GLA_HEADS * GLA_DK), F32)
    scratch = [pltpu.VMEM((tb, GLA_PAD - LANES), F32), keys, keys,
               pltpu.VMEM((GLA_HEADS // 2, 2 * GLA_DK, GLA_DV), F32)]
    return _mixer_call(_gla_body, x, params, GLA_HEADS * GLA_DV, scratch, bsz, seq, tb, "gated_linear_attention")


def _seg_sum(x, ones_bd):
    n = x.shape[0]
    both = jnp.dot(jnp.concatenate(_split_bf16(x, 2), axis=0), ones_bd, preferred_element_type=F32)
    return both[0:n] + both[n:2 * n]


def _rw_body(x_ref, w_ref, mu_ref, w0_ref, w2_ref, a0_ref, a2_ref, g2_ref, kk_ref, ka_ref, rk_ref, gng_ref, gnb_ref,
             o_ref, ext_ref, r_s, k_s, v_s, ld_s, ci_s, an_s, bn_s, g_s, st_ref, *, tb):
    t_idx = pl.program_id(1)

    @pl.when(t_idx == 0)
    def _():
        ext_ref[0:8, :] = jnp.zeros((8, RW_PAD), F32)
        st_ref[...] = jnp.zeros_like(st_ref)

    @pl.when(t_idx != 0)
    def _():
        ext_ref[0:8, :] = ext_ref[tb:tb + 8, :]

    w = RW_WIDTH
    n2 = 2 * CHUNK
    cpi = RW_CHUNKS_PER_STEP
    sb = cpi * CHUNK
    npair = RW_HEADS // 2
    jobs = [(j, p) for j in range(cpi) for p in range(npair)]
    r2, c2 = _iota2((LANES, LANES))
    ones_bd = jnp.where((r2 >> 6) == (c2 >> 6), 1.0, 0.0).astype(BF16)
    strict = _block_causal(n2, strict=True)
    incl = _block_causal(n2, strict=False)
    zero = jnp.zeros((n2, LANES), F32)

    ext_ref[8:tb + 8, :] = jnp.dot(x_ref[...].astype(BF16), w_ref[...], preferred_element_type=F32)

    def shifted(cols):
        cur = ext_ref[8:tb + 8, cols]
        return cur + (ext_ref[7:tb + 7, cols] - cur) * mu_ref[:, cols]

    lo = shifted(slice(3 * w, 3 * w + LANES))
    wv = -_softplus(-(w0_ref[...] + _mm_f32(jnp.tanh(lo), w2_ref[...]))) - 0.5
    ld = -jnp.exp(wv)
    ld_s[...] = ld
    ci_s[...] = _chunk_cumsum(ld)
    a = _sigmoid(a0_ref[...] + _mm_f32(lo, a2_ref[...]))
    glo = shifted(slice(3 * w + LANES, RW_PAD))
    g_s[...] = _mm(_sigmoid(glo), g2_ref[...])
    r_s[...] = shifted(slice(0, w))
    v_s[...] = shifted(slice(2 * w, 3 * w))
    for p in range(npair):
        lanes = slice(p * LANES, (p + 1) * LANES)
        k = shifted(slice(w + p * LANES, w + (p + 1) * LANES))
        t = k * kk_ref[:, lanes]
        kkn = t * lax.rsqrt(_seg_sum(t * t, ones_bd) + L2_EPS)
        an_s[:, lanes] = -kkn
        bn_s[:, lanes] = kkn * a[:, lanes]
        k_s[:, lanes] = k * (1.0 + (a[:, lanes] - 1.0) * ka_ref[:, lanes])

    def step(it, carry):
        rows = [pl.ds(pl.multiple_of(it * sb + j * CHUNK, CHUNK), CHUNK) for j in range(cpi)]

        def load(src):
            return [src[rows[j], p * LANES:(p + 1) * LANES] for j, p in jobs]

        r, kx, v, ld, ci, an, bn = (load(src) for src in (r_s, k_s, v_s, ld_s, ci_s, an_s, bn_s))
        e_in = [jnp.exp(-c) for c in ci]
        e_out = [jnp.exp(c[CHUNK - 1:CHUNK, :] - c) for c in ci]
        a_s = [_stack_heads(a * jnp.exp(c - t)) for a, c, t in zip(an, ci, ld)]
        r_t = [_stack_heads(x * jnp.exp(c)) for x, c in zip(r, ci)]
        b_hat = [_stack_heads(b * e) for b, e in zip(bn, e_out)]
        k_hat = [_stack_heads(k * e) for k, e in zip(kx, e_out)]
        v_st = [_stack_heads(x) for x in v]
        gram = [_mm_nt(jnp.concatenate([a, x], axis=0),
                       jnp.concatenate([_stack_heads(b * e), _stack_heads(k * e)], axis=0))
                for a, x, b, k, e in zip(a_s, r_t, bn, kx, e_in)]
        a_ab = [jnp.where(strict, g[0:n2, 0:n2], 0.0) for g in gram]
        a_ak = [jnp.where(strict, g[0:n2, n2:2 * n2], 0.0) for g in gram]
        a_rbk = [jnp.concatenate([jnp.where(incl, g[n2:2 * n2, 0:n2], 0.0),
                                  jnp.where(incl, g[n2:2 * n2, n2:2 * n2], 0.0)], axis=1) for g in gram]
        t_inv = _unit_lower_inverses([-a for a in a_ab], n2)
        akv = [_mm(a, x) for a, x in zip(a_ak, v_st)]
        x_au = [_mm(t, jnp.concatenate([a, b], axis=1)) for t, a, b in zip(t_inv, a_s, akv)]
        bk_t = [jnp.concatenate([b.T, k.T], axis=1) for b, k in zip(b_hat, k_hat)]
        zz = [_mm(jnp.concatenate([a, t], axis=0), jnp.concatenate([x, jnp.concatenate([zero, u], axis=1)], axis=0))
              for a, t, x, u in zip(a_rbk, bk_t, x_au, v_st)]
        rw_p = [jnp.concatenate([x + y[0:n2, 0:LANES], y[n2:2 * n2, 0:LANES]], axis=0) for x, y in zip(r_t, zz)]
        g_col = [jnp.exp(c.T[:, CHUNK - 1:CHUNK]) for c in ci]
        sts = [st_ref[p] for p in range(npair)]
        for idx, (j, p) in enumerate(jobs):
            lanes = slice(p * LANES, (p + 1) * LANES)
            st = sts[p]
            both = _mm(rw_p[idx], st)
            ys = both[0:n2] + zz[idx][0:n2, LANES:2 * LANES]
            sts[p] = g_col[idx] * st + both[n2:2 * n2] + zz[idx][n2:2 * n2, LANES:2 * LANES]
            y = ys[0:CHUNK] + ys[CHUNK:n2]
            sums = _seg_sum(jnp.concatenate([y, r[idx] * kx[idx] * rk_ref[:, lanes]], axis=0), ones_bd)
            d = y - sums[0:CHUNK] * (1.0 / RW_HEAD)
            var = _seg_sum(d * d, ones_bd) * (1.0 / RW_HEAD)
            yn = d * lax.rsqrt(var + RW_GN_EPS) * gng_ref[:, lanes] + gnb_ref[:, lanes]
            bonus = sums[CHUNK:n2] * v[idx]
            o_ref[rows[j], lanes] = ((yn + bonus) * g_s[rows[j], lanes]).astype(o_ref.dtype)
        for p in range(npair):
            st_ref[p] = sts[p]
        return carry

    lax.fori_loop(0, tb // sb, step, 0)


def _rw_call(x, params, bsz, seq, tb):
    scratch = ([pltpu.VMEM((tb + 8, RW_PAD), F32)] + [pltpu.VMEM((tb, RW_WIDTH), F32)] * 8
               + [pltpu.VMEM((RW_HEADS // 2, LANES, LANES), F32)])
    return _mixer_call(_rw_body, x, params, RW_WIDTH, scratch, bsz, seq, tb, "rwkv7_time_mix")


def _pad_to(a, size, axis):
    pad = [(0, 0)] * a.ndim
    pad[axis] = (0, size - a.shape[axis])
    return jnp.pad(a, pad)


def _mixer_in_weights(w_in):
    o = 0
    rw = w_in[..., o:o + RW_IN]
    o += RW_IN
    gla = w_in[..., o:o + GLA_IN]
    o += GLA_IN
    gdn = w_in[..., o:o + GDN_IN]
    o += GDN_IN
    gate = w_in[..., o:]
    qkv_w = 2 * GLA_HEADS * GLA_DK + GLA_HEADS * GLA_DV
    gla_p = jnp.concatenate([gla[..., 0:qkv_w], gla[..., qkv_w + GLA_GATE_LORA:], gla[..., qkv_w:qkv_w + GLA_GATE_LORA]], axis=-1)
    gdn_p = jnp.concatenate([gdn[..., 0:GDN_QKV], gdn[..., GDN_QKV + 2 * GDN_HEADS:], gdn[..., GDN_QKV:GDN_QKV + 2 * GDN_HEADS]], axis=-1)
    return (_pad_to(rw, RW_PAD, 2).astype(BF16), _pad_to(gla_p, GLA_PAD, 2).astype(BF16),
            _pad_to(gdn_p, GDN_PAD, 2).astype(BF16), gate.astype(BF16))


def _rows(v):
    return v.reshape(v.shape[:-1] + (1, v.shape[-1]))


def _tiles(n, seq):
    return min(512, n), min(512, seq), min(1024, seq)


def kernel(x, p, ln_g, ln_b, ffn_w1, ffn_w3, ffn_w2, w_in, rw_mu, rw_w0, rw_w2, rw_a0, rw_a2, rw_g2, rw_k_k, rw_k_a, rw_r_k, rw_gn_g, rw_gn_b, gla_gk_w2, gla_gk_b, gla_norm_g, gdn_conv_w, gdn_a_log, gdn_dt_bias, gdn_norm_g, w_branch, w_o, ple_w_gate, ple_w_proj):
    bsz, seq, d = x.shape
    depth = p.shape[0]
    n = bsz * seq
    tm, tb, tb_gla = _tiles(n, seq)
    x = x.reshape(n, d)
    p = p.reshape(depth, n, PLE_DIM)

    w1, w3, w2 = ffn_w1.astype(BF16), ffn_w3.astype(BF16), ffn_w2.astype(BF16)
    w_rw, w_gla, w_gdn, w_gate = _mixer_in_weights(w_in)
    w_br, w_out = w_branch.astype(BF16), w_o.astype(BF16)
    w_pg, w_pp = ple_w_gate.astype(BF16), ple_w_proj.astype(BF16)
    lng, lnb = _rows(ln_g), _rows(ln_b)
    rw_vecs = [_rows(v) for v in (rw_k_k, rw_k_a, rw_r_k.reshape(depth, RW_WIDTH), rw_gn_g, rw_gn_b)]
    rw_mu_p = _rows(_pad_to(rw_mu, RW_PAD, 1))
    rw_w0_r, rw_a0_r = _rows(rw_w0), _rows(rw_a0)
    rw_w2_p = _pad_to(rw_w2, LANES, 1)
    rw_a2_p = jnp.pad(rw_a2, ((0, 0), (RW_DECAY_LORA, 0), (0, 0)))
    rw_g2_p = _pad_to(rw_g2, 2 * LANES, 1)
    gla_w2_p, gla_b_r, gla_ng = _pad_to(gla_gk_w2, LANES, 1), _rows(gla_gk_b), _rows(gla_norm_g)
    gdn_alog, gdn_dtb, gdn_ng = _rows(_pad_to(gdn_a_log, LANES, 1)), _rows(_pad_to(gdn_dt_bias, LANES, 1)), _rows(gdn_norm_g)

    for i in range(depth):
        x = _ffn_call(x, [_param(w1, i, 0), _param(w3, i, 0), _param(w2, i, 0), _param(lng, i, 0), _param(lnb, i, 0)], tm)
        o_rw = _rw_call(x, [_param(w_rw, i), _param(rw_mu_p, i), _param(rw_w0_r, i), _param(rw_w2_p, i),
                            _param(rw_a0_r, i), _param(rw_a2_p, i), _param(rw_g2_p, i)]
                        + [_param(v, i) for v in rw_vecs], bsz, seq, tb)
        o_gla = _gla_call(x, [_param(w_gla, i), _param(gla_w2_p, i), _param(gla_b_r, i), _param(gla_ng, i)], bsz, seq, tb_gla)
        o_gdn = _gdn_call(x, [_param(w_gdn, i), _param(gdn_conv_w, i), _param(gdn_alog, i), _param(gdn_dtb, i),
                              _param(gdn_ng, i)], bsz, seq, tb)
        x = _merge_call(x, (o_rw, o_gla, o_gdn), [_param(w_gate, i), _param(w_br, i), _param(w_out, i),
                                                  _param(lng, i, 1), _param(lnb, i, 1)], tm)
        x = _ffn_ple_call(x, p, i, [_param(w1, i, 1), _param(w3, i, 1), _param(w2, i, 1), _param(lng, i, 2), _param(lnb, i, 2),
                                    _param(w_pg, i), _param(w_pp, i), _param(lng, i, 3), _param(lnb, i, 3)], tm)
    return x.reshape(bsz, seq, d)
```

```python
import functools

import jax
import jax.numpy as jnp
from jax import lax
from jax.experimental import pallas as pl
from jax.experimental.pallas import tpu as pltpu

F32 = jnp.float32
BF16 = jnp.bfloat16

D_MODEL = 1024
DEPTH = 2
PLE_DIM = 256
D_FF = 2816
LN_EPS = 1e-5
DEEPNORM_ALPHA = (2 * DEPTH) ** 0.25
MACARON_W = 0.5

RW_HEADS = 8
RW_HEAD = 64
RW_WIDTH = RW_HEADS * RW_HEAD
RW_DECAY_LORA = 64
RW_AAA_LORA = 64
RW_GATE_LORA = 160
RW_GN_EPS = 1e-5 * RW_HEAD
RW_IN = 3 * RW_WIDTH + RW_DECAY_LORA + RW_AAA_LORA + RW_GATE_LORA

GLA_HEADS = 4
GLA_DK = 64
GLA_DV = 128
GLA_GATE_LORA = 16
GLA_GATE_NORM = 16.0
GLA_IN = 2 * GLA_HEADS * GLA_DK + 2 * GLA_HEADS * GLA_DV + GLA_GATE_LORA

GDN_HEADS = 4
GDN_DK = 128
GDN_DV = 128
GDN_CONV = 4
GDN_QKV = GDN_HEADS * (2 * GDN_DK + GDN_DV)
GDN_IN = GDN_QKV + 2 * GDN_HEADS + GDN_HEADS * GDN_DV

HEAD_NORM_EPS = 1e-6
L2_EPS = 1e-6
N_BRANCH = 3
BRANCH_WIDTH = 512

LANES = 128
SUBLANES = 8
CHUNK = 64
GDN_CHUNKS_PER_WAVE = 8
RW_CHUNKS_PER_STEP = 4
GLA_CHUNKS_PER_STEP = 4
RW_PAD = 1920
GLA_PAD = 1664
GDN_PAD = 2176
MIX_PAD = RW_PAD + GLA_PAD + GDN_PAD
VMEM_LIMIT = 56 * 1024 * 1024


def _param(arr, *prefix):
    tail = arr.shape[len(prefix):]
    zeros = (0,) * len(tail)
    spec = pl.BlockSpec((None,) * len(prefix) + tail, lambda *_: prefix + zeros, pipeline_mode=pl.Buffered(1))
    return arr, spec


def _call(body, grid, sem, streamed, params, out_specs, out_shape, scratch, name):
    ops = list(streamed) + list(params)
    return pl.pallas_call(
        body,
        grid=grid,
        in_specs=[spec for _, spec in ops],
        out_specs=out_specs,
        out_shape=out_shape,
        scratch_shapes=scratch,
        compiler_params=pltpu.CompilerParams(dimension_semantics=sem, vmem_limit_bytes=VMEM_LIMIT),
        name=name,
    )(*[arr for arr, _ in ops])


def _mm(a, b):
    return jnp.dot(a.astype(BF16), b.astype(BF16), preferred_element_type=F32)


def _mm_nt(a, b):
    return lax.dot_general(a.astype(BF16), b.astype(BF16), (((1,), (1,)), ((), ())),
                           preferred_element_type=F32)


def _split_bf16(x, parts):
    out = []
    for _ in range(parts - 1):
        piece = x.astype(BF16)
        out.append(piece)
        x = x - piece.astype(F32)
    out.append(x.astype(BF16))
    return out


def _mm_f32(a, b):
    a_hi, a_lo = _split_bf16(a, 2)
    b_hi, b_lo = _split_bf16(b, 2)
    dot = functools.partial(jnp.dot, preferred_element_type=F32)
    return dot(a_hi, b_hi) + (dot(a_hi, b_lo) + dot(a_lo, b_hi))


def _chunk_cumsum(x):
    tril = _tril_f32(CHUNK).astype(BF16)
    out = []
    for r0 in range(0, x.shape[0], CHUNK):
        pieces = _split_bf16(x[r0:r0 + CHUNK], 3)
        acc = jnp.dot(tril, pieces[2], preferred_element_type=F32)
        acc = acc + jnp.dot(tril, pieces[1], preferred_element_type=F32)
        out.append(acc + jnp.dot(tril, pieces[0], preferred_element_type=F32))
    return jnp.concatenate(out, axis=0)


def _layer_norm(y, g, b):
    m = jnp.mean(y, -1, keepdims=True)
    d = y - m
    var = jnp.mean(d * d, -1, keepdims=True)
    return d * lax.rsqrt(var + LN_EPS) * g + b


def _sigmoid(x):
    return 0.5 * jnp.tanh(0.5 * x) + 0.5


def _softplus(x):
    return jnp.maximum(x, 0.0) + jnp.log(1.0 + jnp.exp(-jnp.abs(x)))


def _iota2(shape):
    return (lax.broadcasted_iota(jnp.int32, shape, 0), lax.broadcasted_iota(jnp.int32, shape, 1))


def _tril_f32(n):
    r, c = _iota2((n, n))
    return jnp.where(r >= c, 1.0, 0.0).astype(F32)


def _sibling_mask(r, c, s):
    k = s.bit_length() - 1
    return ((r >> (k + 1)) == (c >> (k + 1))) & (((r >> k) & 1) == 1) & (((c >> k) & 1) == 0)


def _unit_lower_inverses(lows, n, between=lambda: None):
    r, c = _iota2((n, n))
    eye = jnp.where(r == c, 1.0, 0.0).astype(F32)
    first = _sibling_mask(r, c, 1)
    xs = [eye - jnp.where(first, low, 0.0) for low in lows]
    s = 2
    while s < CHUNK:
        sib = _sibling_mask(r, c, s)
        if s % SUBLANES:
            ts = [_mm(jnp.where(sib, low, 0.0), x) for low, x in zip(lows, xs)]
            between()
            us = [_mm(x, t) for x, t in zip(xs, ts)]
            between()
            xs = [x - u for x, u in zip(xs, us)]
        else:
            blank = jnp.zeros((s, n), F32)

            def odd(a, s=s):
                return jnp.concatenate([a[g * s:(g + 1) * s] for g in range(1, n // s, 2)], axis=0)

            def spread(a, s=s, blank=blank):
                return jnp.concatenate([piece for g in range(n // (2 * s)) for piece in (blank, a[g * s:(g + 1) * s])],
                                       axis=0)

            ts = [_mm(odd(jnp.where(sib, low, 0.0)), x) for low, x in zip(lows, xs)]
            between()
            us = [_mm(odd(x), spread(t)) for x, t in zip(xs, ts)]
            between()
            xs = [x - spread(u) for x, u in zip(xs, us)]
        s *= 2
    return xs


def _stack_heads(z):
    lane = lax.broadcasted_iota(jnp.int32, z.shape, 1)
    first = lane < (LANES // 2)
    return jnp.concatenate([jnp.where(first, z, 0.0), jnp.where(first, 0.0, z)], axis=0)


def _block_causal(n, strict):
    r, c = _iota2((n, n))
    same = (r >> 6) == (c >> 6)
    return same & ((r > c) if strict else (r >= c))


FF_CHUNK = 1408


def _swiglu_ln(x, w1_ref, w3_ref, w2_ref, g, b):
    xb = x.astype(BF16)
    acc = None
    for c0 in range(0, D_FF, FF_CHUNK):
        h1 = jnp.dot(xb, w1_ref[:, c0:c0 + FF_CHUNK], preferred_element_type=F32)
        h3 = jnp.dot(xb, w3_ref[:, c0:c0 + FF_CHUNK], preferred_element_type=F32)
        act = (h1 * _sigmoid(h1) * h3).astype(BF16)
        part = jnp.dot(act, w2_ref[c0:c0 + FF_CHUNK, :], preferred_element_type=F32)
        acc = part if acc is None else acc + part
    return _layer_norm(DEEPNORM_ALPHA * x + MACARON_W * acc, g, b)


def _ffn_body(x_ref, w1_ref, w3_ref, w2_ref, g_ref, b_ref, o_ref):
    o_ref[...] = _swiglu_ln(x_ref[...], w1_ref, w3_ref, w2_ref, g_ref[...], b_ref[...])


def _ffn_ple_body(x_ref, p_ref, w1_ref, w3_ref, w2_ref, g_ref, b_ref, wg_ref, wp_ref, g2_ref, b2_ref, o_ref):
    x = _swiglu_ln(x_ref[...], w1_ref, w3_ref, w2_ref, g_ref[...], b_ref[...])
    gate = _sigmoid(jnp.dot(x.astype(BF16), wg_ref[...], preferred_element_type=F32))
    emb = jnp.dot(p_ref[...].astype(BF16), wp_ref[...], preferred_element_type=F32)
    o_ref[...] = _layer_norm(DEEPNORM_ALPHA * x + gate * emb, g2_ref[...], b2_ref[...])


def _token_rows(x, tm):
    return x, pl.BlockSpec((tm, x.shape[1]), lambda i: (i, 0))


def _ffn_call(x, params, tm):
    n = x.shape[0]
    return _call(_ffn_body, (n // tm,), ("arbitrary",), [_token_rows(x, tm)], params,
                 pl.BlockSpec((tm, D_MODEL), lambda i: (i, 0)), jax.ShapeDtypeStruct((n, D_MODEL), F32), [], "ffn_ln")


def _ffn_ple_call(x, p, layer, params, tm):
    n = x.shape[0]
    p_rows = (p, pl.BlockSpec((None, tm, PLE_DIM), lambda i: (layer, i, 0)))
    return _call(_ffn_ple_body, (n // tm,), ("arbitrary",), [_token_rows(x, tm), p_rows], params,
                 pl.BlockSpec((tm, D_MODEL), lambda i: (i, 0)), jax.ShapeDtypeStruct((n, D_MODEL), F32), [], "ffn_ple_ln")


def _merge_body(x_ref, orw_ref, ogla_ref, ogdn_ref, wgate_ref, wbr_ref, wo_ref, g_ref, b_ref, o_ref):
    x = x_ref[...]
    xb = x.astype(BF16)
    merged = None
    for n, br_ref in enumerate((orw_ref, ogla_ref, ogdn_ref)):
        gate = _sigmoid(jnp.dot(xb, wgate_ref[:, n * D_MODEL:(n + 1) * D_MODEL], preferred_element_type=F32))
        term = gate * jnp.dot(br_ref[...], wbr_ref[n], preferred_element_type=F32)
        merged = term if merged is None else merged + term
    mix = jnp.dot(merged.astype(BF16), wo_ref[...], preferred_element_type=F32)
    o_ref[...] = _layer_norm(DEEPNORM_ALPHA * x + mix, g_ref[...], b_ref[...])


def _merge_call(x, branches, params, tm):
    n = x.shape[0]
    return _call(_merge_body, (n // tm,), ("arbitrary",), [_token_rows(t, tm) for t in (x,) + tuple(branches)], params,
                 pl.BlockSpec((tm, D_MODEL), lambda i: (i, 0)), jax.ShapeDtypeStruct((n, D_MODEL), F32), [], "merge_ln")


def _gdn_body(x_ref, w_ref, cw_ref, alog_ref, dtb_ref, ng_ref, o_ref,
              ext_ref, q_s, k_s, v_s, g_s, beta_s, gate_s, st_ref, *, tb):
    t_idx = pl.program_id(1)

    @pl.when(t_idx == 0)
    def _():
        ext_ref[0:8, :] = jnp.zeros((8, GDN_QKV), F32)
        st_ref[...] = jnp.zeros_like(st_ref)

    @pl.when(t_idx != 0)
    def _():
        ext_ref[0:8, :] = ext_ref[tb:tb + 8, :]

    hw = GDN_HEADS * GDN_DK
    wave_rows = GDN_CHUNKS_PER_WAVE * CHUNK

    def prep_units(wave):
        r0 = wave * wave_rows
        rows = slice(r0, r0 + wave_rows)
        vals = {}

        def project(c0, c1):
            def run():
                if "xb" not in vals:
                    vals["xb"] = x_ref[rows, :].astype(BF16)
                ext_ref[8 + r0:8 + r0 + wave_rows, c0:c1] = jnp.dot(vals["xb"], w_ref[:, c0:c1],
                                                                      preferred_element_type=F32)
            return run

        def gates():
            xb = vals["xb"]
            gate_s[rows, :] = jnp.dot(xb, w_ref[:, GDN_QKV:GDN_QKV + GDN_HEADS * GDN_DV], preferred_element_type=F32)
            ab = jnp.dot(xb, w_ref[:, GDN_QKV + GDN_HEADS * GDN_DV:GDN_PAD], preferred_element_type=F32)
            g_s[rows, :] = _chunk_cumsum(-jnp.exp(alog_ref[...]) * _softplus(ab + dtb_ref[...]))
            beta_s[rows, :] = _sigmoid(ab)

        def conv(grp, dst):
            def run():
                cols = slice(grp * hw, (grp + 1) * hw)
                acc = ext_ref[8 + r0:8 + r0 + wave_rows, cols] * cw_ref[GDN_CONV - 1:GDN_CONV, cols]
                for d in range(1, GDN_CONV):
                    acc = acc + ext_ref[8 + r0 - d:8 + r0 - d + wave_rows, cols] * cw_ref[GDN_CONV - 1 - d:GDN_CONV - d, cols]
                act = acc * _sigmoid(acc)
                if grp == 2:
                    dst[rows, :] = act
                else:
                    scale = GDN_DK ** -0.5 if grp == 0 else 1.0
                    for h in range(GDN_HEADS):
                        t = act[:, h * GDN_DK:(h + 1) * GDN_DK]
                        dst[rows, h * GDN_DK:(h + 1) * GDN_DK] = t * (lax.rsqrt(jnp.sum(t * t, -1, keepdims=True) + L2_EPS) * scale)
            return run

        return ([project(g * hw, (g + 1) * hw) for g in range(3)] + [gates]
                + [conv(g, dst) for g, dst in enumerate((q_s, k_s, v_s))])

    n2 = 2 * CHUNK
    ng = ng_ref[...]

    causal = _block_causal(n2, strict=False)
    strict = _block_causal(n2, strict=True)
    first = lax.broadcasted_iota(jnp.int32, (GDN_DK, n2), 1) < CHUNK
    pending = []

    def fill():
        if pending:
            pending.pop(0)()

    def pair_rows(src, rows, pr, width):
        return jnp.concatenate([src[rows, h * width:(h + 1) * width] for h in (2 * pr, 2 * pr + 1)], axis=0)

    def pair_col(a, pr, off):
        return jnp.concatenate([a[:, off + h:off + h + 1] for h in (2 * pr, 2 * pr + 1)], axis=0)

    def precompute(chunks):
        jobs = [(j, pr) for j in range(len(chunks)) for pr in range(GDN_HEADS // 2)]
        rows = [slice(c * CHUNK, (c + 1) * CHUNK) for c in chunks]
        gcs = [g_s[rw, :] for rw in rows]
        betas = [beta_s[rw, :] for rw in rows]
        gi = [pair_col(gcs[j], pr, 0) for j, pr in jobs]
        bcol = [pair_col(betas[j], pr, GDN_HEADS) for j, pr in jobs]
        qs = [pair_rows(q_s, rows[j], pr, GDN_DK) for j, pr in jobs]
        ks = [pair_rows(k_s, rows[j], pr, GDN_DK) for j, pr in jobs]
        vs = [pair_rows(v_s, rows[j], pr, GDN_DV) for j, pr in jobs]
        dec = []
        for g in gi:
            gi_b = jnp.broadcast_to(g, (n2, n2))
            diff = gi_b - gi_b.T
            dec.append(jnp.where(causal, jnp.exp(jnp.where(causal, diff, 0.0)), 0.0))
        kb = [k * b for k, b in zip(ks, bcol)]
        gram = [_mm_nt(jnp.concatenate([a, q], axis=0), k) for a, q, k in zip(kb, qs, ks)]
        fill()
        low = [jnp.where(strict, g[0:n2] * d, 0.0) for g, d in zip(gram, dec)]
        att = [g[n2:2 * n2] * d for g, d in zip(gram, dec)]
        t_inv = _unit_lower_inverses(low, n2, between=fill)
        eg = [jnp.exp(g) for g in gi]
        wu = [_mm(t, jnp.concatenate([a * e, v * b], axis=1))
              for t, a, e, v, b in zip(t_inv, kb, eg, vs, bcol)]
        fill()
        aw = [_mm(a, x) for a, x in zip(att, wu)]
        fill()
        g_end = [jnp.concatenate([jnp.broadcast_to(gcs[j][CHUNK - 1:CHUNK, h:h + 1], (CHUNK, 1))
                                  for h in (2 * pr, 2 * pr + 1)], axis=0) for j, pr in jobs]
        k_hat_t = [(k * jnp.exp(ge - g)).T for k, ge, g in zip(ks, g_end, gi)]
        kwu2 = [_mm(jnp.concatenate([jnp.where(first, kt, 0.0), jnp.where(first, 0.0, kt)], axis=0), x)
                for kt, x in zip(k_hat_t, wu)]
        fill()
        per_head = {}
        for idx, (j, pr) in enumerate(jobs):
            for half, h in enumerate((2 * pr, 2 * pr + 1)):
                sl = slice(half * CHUNK, (half + 1) * CHUNK)
                kwu = kwu2[idx][half * GDN_DK:(half + 1) * GDN_DK]
                q_eff = qs[idx][sl] * eg[idx][sl] - aw[idx][sl, 0:GDN_DV]
                g_dec = jnp.exp(gcs[j][CHUNK - 1:CHUNK, h:h + 1])
                per_head[(chunks[j], h)] = (q_eff, aw[idx][sl, GDN_DV:2 * GDN_DV], g_dec, kwu)
        return per_head

    sts = [st_ref[h] for h in range(GDN_HEADS)]

    def chain_step(c, per_head):
        def run():
            rows = slice(c * CHUNK, (c + 1) * CHUNK)
            for h in range(GDN_HEADS):
                q_eff, o_loc, g_dec, kwu = per_head[(c, h)]
                st = sts[h]
                both = _mm(jnp.concatenate([q_eff, kwu[:, 0:GDN_DV]], axis=0), st)
                o = both[0:CHUNK] + o_loc
                sts[h] = g_dec * st - both[CHUNK:CHUNK + GDN_DK] + kwu[:, GDN_DV:2 * GDN_DV]
                o = o * lax.rsqrt(jnp.mean(o * o, -1, keepdims=True) + HEAD_NORM_EPS) * ng
                gate = gate_s[rows, h * GDN_DV:(h + 1) * GDN_DV]
                o_ref[rows, h * GDN_DV:(h + 1) * GDN_DV] = (o * gate * _sigmoid(gate)).astype(o_ref.dtype)
        return run

    n_waves = tb // wave_rows
    for unit in prep_units(0):
        unit()
    for wave in range(n_waves):
        chunks = list(range(wave * GDN_CHUNKS_PER_WAVE, (wave + 1) * GDN_CHUNKS_PER_WAVE))
        if wave + 1 < n_waves:
            pending.extend(prep_units(wave + 1))
        per_head = precompute(chunks)
        while pending:
            fill()
        pending.extend(chain_step(c, per_head) for c in chunks)
    while pending:
        fill()
    for h in range(GDN_HEADS):
        st_ref[h] = sts[h]


def _mixer_call(body, x, params, width, scratch, bsz, seq, tb, name):
    nt = seq // tb
    rows = lambda b, t: (b * nt + t, 0)
    return _call(functools.partial(body, tb=tb), (bsz, nt), ("arbitrary", "arbitrary"),
                 [(x, pl.BlockSpec((tb, D_MODEL), rows))], params,
                 pl.BlockSpec((tb, width), rows), jax.ShapeDtypeStruct((bsz * seq, width), BF16), scratch, name)


def _gdn_call(x, params, bsz, seq, tb):
    wide = pltpu.VMEM((tb, GDN_HEADS * GDN_DK), F32)
    narrow = pltpu.VMEM((tb, LANES), F32)
    scratch = [pltpu.VMEM((tb + 8, GDN_QKV), F32), wide, wide, wide, narrow, narrow, wide,
               pltpu.VMEM((GDN_HEADS, GDN_DK, GDN_DV), F32)]
    return _mixer_call(_gdn_body, x, params, GDN_HEADS * GDN_DV, scratch, bsz, seq, tb, "gated_deltanet")


def _gla_level_arg(s, la, b, ri):
    c = CHUNK
    if s >= 4:
        ref = jnp.concatenate(
            [jnp.broadcast_to(b[g * 2 * s + s - 1:g * 2 * s + s, :], (2 * s, LANES)) for g in range(c // (2 * s))],
            axis=0)
        return -jnp.abs(b - ref)
    if s == 2:
        m = ri & 3
        up = pltpu.roll(la, c - 1, 0)
        dn = pltpu.roll(la, 1, 0)
        return jnp.where(m == 0, up, jnp.where(m == 1, 0.0, jnp.where(m == 2, la, la + dn)))
    return jnp.where((ri & 1) == 1, la, 0.0)


def _gla_scores(qs, ks, las, bs, between):
    c = CHUNK
    ri = lax.broadcasted_iota(jnp.int32, (c, LANES), 0)
    r2, c2 = _iota2((2 * c, c))
    r2 = r2 & (c - 1)
    diag = r2 == c2
    scores = [jnp.where(diag, _mm_nt(_stack_heads(q), k), 0.0) for q, k in zip(qs, ks)]
    between()
    s = c // 2
    while s >= 1:
        row_side = ((ri >> (s.bit_length() - 1)) & 1) == 1
        sib = _sibling_mask(r2, c2, s)
        es = [jnp.exp(_gla_level_arg(s, la, b, ri)) for la, b in zip(las, bs)]
        part = [_mm_nt(_stack_heads(jnp.where(row_side, q * e, 0.0)), jnp.where(row_side, 0.0, k * e))
                for q, k, e in zip(qs, ks, es)]
        between()
        scores = [sc + jnp.where(sib, p, 0.0) for sc, p in zip(scores, part)]
        s //= 2
    return scores


def _gla_body(x_ref, w_ref, w2_ref, gkb_ref, ng_ref, o_ref, h_s, la_s, b_s, st_ref, *, tb):
    t_idx = pl.program_id(1)

    @pl.when(t_idx == 0)
    def _():
        st_ref[...] = jnp.zeros_like(st_ref)

    hk = GLA_HEADS * GLA_DK
    hv = GLA_HEADS * GLA_DV
    ng = ng_ref[...]
    cpi = GLA_CHUNKS_PER_STEP
    sb = cpi * CHUNK
    npair = GLA_HEADS // 2
    jobs = [(j, p) for j in range(cpi) for p in range(npair)]
    row_first = lax.broadcasted_iota(jnp.int32, (LANES, GLA_DV), 0) < GLA_DK

    def prep_units(s):
        rows = slice(s * sb, (s + 1) * sb)
        vals = {}

        def proj(c0, c1):
            def run():
                if "xb" not in vals:
                    vals["xb"] = x_ref[rows, :].astype(BF16)
                h_s[rows, c0:c1] = jnp.dot(vals["xb"], w_ref[:, c0:c1], preferred_element_type=F32)
            return run

        def decay():
            gk_lo = jnp.dot(vals["xb"], w_ref[:, 2 * hk + 2 * hv:GLA_PAD], preferred_element_type=F32)
            z = _mm_f32(gk_lo, w2_ref[...]) + gkb_ref[...]
            la = (jnp.minimum(z, 0.0) - jnp.log(1.0 + jnp.exp(-jnp.abs(z)))) * (1.0 / GLA_GATE_NORM)
            la_s[rows, :] = la
            b_s[rows, :] = _chunk_cumsum(la)

        return [proj(c0, c0 + 2 * LANES) for c0 in range(0, 2 * hk + 2 * hv, 2 * LANES)] + [decay]

    def recurrence(s, pending):
        rows = [slice(s * sb + j * CHUNK, s * sb + (j + 1) * CHUNK) for j in range(cpi)]

        def fill():
            if pending:
                pending.pop(0)()
        qs = [h_s[rows[j], p * LANES:(p + 1) * LANES] * (GLA_DK ** -0.5) for j, p in jobs]
        ks = [h_s[rows[j], hk + p * LANES:hk + (p + 1) * LANES] for j, p in jobs]
        las = [la_s[rows[j], p * LANES:(p + 1) * LANES] for j, p in jobs]
        bs = [b_s[rows[j], p * LANES:(p + 1) * LANES] for j, p in jobs]
        vpair = [h_s[rows[j], 2 * hk + 2 * p * GLA_DV:2 * hk + (2 * p + 2) * GLA_DV] for j, p in jobs]
        scores = _gla_scores(qs, ks, las, bs, fill)
        q_in = [_stack_heads(q * jnp.exp(b)) for q, b in zip(qs, bs)]
        bts = [b.T for b in bs]
        b_end = [bt[:, CHUNK - 1:CHUNK] for bt in bts]
        upd = [_mm(k.T * jnp.exp(be - bt), vp) for k, be, bt, vp in zip(ks, b_end, bts, vpair)]
        intra = [[_mm(sc[half * CHUNK:(half + 1) * CHUNK], vp[:, half * GLA_DV:(half + 1) * GLA_DV])
                  for half in range(2)] for sc, vp in zip(scores, vpair)]
        sts = [st_ref[p] for p in range(npair)]
        for idx, (j, p) in enumerate(jobs):
            st = sts[p]
            inter = _mm(q_in[idx], st)
            sts[p] = st * jnp.exp(b_end[idx]) + jnp.where(row_first, upd[idx][:, 0:GLA_DV], upd[idx][:, GLA_DV:2 * GLA_DV])
            for half in range(2):
                h = 2 * p + half
                o = intra[idx][half] + inter[half * CHUNK:(half + 1) * CHUNK]
                o = o * lax.rsqrt(jnp.mean(o * o, -1, keepdims=True) + HEAD_NORM_EPS) * ng
                gate = h_s[rows[j], 2 * hk + hv + h * GLA_DV:2 * hk + hv + (h + 1) * GLA_DV]
                o_ref[rows[j], h * GLA_DV:(h + 1) * GLA_DV] = (o * gate * _sigmoid(gate)).astype(o_ref.dtype)
        for p in range(npair):
            st_ref[p] = sts[p]
        while pending:
            fill()

    for unit in prep_units(0):
        unit()
    for s in range(tb // sb):
        recurrence(s, prep_units(s + 1) if (s + 1) * sb < tb else [])


def _gla_call(x, params, bsz, seq, tb):
    keys = pltpu.VMEM((tb, GLA_HEADS * GLA_DK), F32)
    scratch = [pltpu.VMEM((tb, GLA_PAD - LANES), F32), keys, keys,
               pltpu.VMEM((GLA_HEADS // 2, 2 * GLA_DK, GLA_DV), F32)]
    return _mixer_call(_gla_body, x, params, GLA_HEADS * GLA_DV, scratch, bsz, seq, tb, "gated_linear_attention")


def _seg_sum(x, ones_bd):
    n = x.shape[0]
    both = jnp.dot(jnp.concatenate(_split_bf16(x, 2), axis=0), ones_bd, preferred_element_type=F32)
    return both[0:n] + both[n:2 * n]


def _rw_body(x_ref, w_ref, mu_ref, w0_ref, w2_ref, a0_ref, a2_ref, g2_ref, kk_ref, ka_ref, rk_ref, gng_ref, gnb_ref,
             o_ref, ext_ref, r_s, k_s, v_s, ld_s, ci_s, an_s, bn_s, g_s, st_ref, *, tb):
    t_idx = pl.program_id(1)

    @pl.when(t_idx == 0)
    def _():
        ext_ref[0:8, :] = jnp.zeros((8, RW_PAD), F32)
        st_ref[...] = jnp.zeros_like(st_ref)

    @pl.when(t_idx != 0)
    def _():
        ext_ref[0:8, :] = ext_ref[tb:tb + 8, :]

    w = RW_WIDTH
    n2 = 2 * CHUNK
    cpi = RW_CHUNKS_PER_STEP
    sb = cpi * CHUNK
    npair = RW_HEADS // 2
    jobs = [(j, p) for j in range(cpi) for p in range(npair)]
    r2, c2 = _iota2((LANES, LANES))
    ones_bd = jnp.where((r2 >> 6) == (c2 >> 6), 1.0, 0.0).astype(BF16)
    strict = _block_causal(n2, strict=True)
    incl = _block_causal(n2, strict=False)
    zero = jnp.zeros((n2, LANES), F32)

    ext_ref[8:tb + 8, :] = jnp.dot(x_ref[...].astype(BF16), w_ref[...], preferred_element_type=F32)

    def shifted(cols):
        cur = ext_ref[8:tb + 8, cols]
        return cur + (ext_ref[7:tb + 7, cols] - cur) * mu_ref[:, cols]

    lo = shifted(slice(3 * w, 3 * w + LANES))
    wv = -_softplus(-(w0_ref[...] + _mm_f32(jnp.tanh(lo), w2_ref[...]))) - 0.5
    ld = -jnp.exp(wv)
    ld_s[...] = ld
    ci_s[...] = _chunk_cumsum(ld)
    a = _sigmoid(a0_ref[...] + _mm(lo, a2_ref[...]))
    glo = shifted(slice(3 * w + LANES, RW_PAD))
    g_s[...] = _mm(_sigmoid(glo), g2_ref[...])
    r_s[...] = shifted(slice(0, w))
    v_s[...] = shifted(slice(2 * w, 3 * w))
    for p in range(npair):
        lanes = slice(p * LANES, (p + 1) * LANES)
        k = shifted(slice(w + p * LANES, w + (p + 1) * LANES))
        t = k * kk_ref[:, lanes]
        kkn = t * lax.rsqrt(_seg_sum(t * t, ones_bd) + L2_EPS)
        an_s[:, lanes] = -kkn
        bn_s[:, lanes] = kkn * a[:, lanes]
        k_s[:, lanes] = k * (1.0 + (a[:, lanes] - 1.0) * ka_ref[:, lanes])

    def step(it, carry):
        rows = [pl.ds(pl.multiple_of(it * sb + j * CHUNK, CHUNK), CHUNK) for j in range(cpi)]

        def load(src):
            return [src[rows[j], p * LANES:(p + 1) * LANES] for j, p in jobs]

        r, kx, v, ld, ci, an, bn = (load(src) for src in (r_s, k_s, v_s, ld_s, ci_s, an_s, bn_s))
        e_in = [jnp.exp(-c) for c in ci]
        e_out = [jnp.exp(c[CHUNK - 1:CHUNK, :] - c) for c in ci]
        a_s = [_stack_heads(a * jnp.exp(c - t)) for a, c, t in zip(an, ci, ld)]
        r_t = [_stack_heads(x * jnp.exp(c)) for x, c in zip(r, ci)]
        b_hat = [_stack_heads(b * e) for b, e in zip(bn, e_out)]
        k_hat = [_stack_heads(k * e) for k, e in zip(kx, e_out)]
        v_st = [_stack_heads(x) for x in v]
        gram = [_mm_nt(jnp.concatenate([a, x], axis=0),
                       jnp.concatenate([_stack_heads(b * e), _stack_heads(k * e)], axis=0))
                for a, x, b, k, e in zip(a_s, r_t, bn, kx, e_in)]
        a_ab = [jnp.where(strict, g[0:n2, 0:n2], 0.0) for g in gram]
        a_ak = [jnp.where(strict, g[0:n2, n2:2 * n2], 0.0) for g in gram]
        a_rbk = [jnp.concatenate([jnp.where(incl, g[n2:2 * n2, 0:n2], 0.0),
                                  jnp.where(incl, g[n2:2 * n2, n2:2 * n2], 0.0)], axis=1) for g in gram]
        t_inv = _unit_lower_inverses([-a for a in a_ab], n2)
        akv = [_mm(a, x) for a, x in zip(a_ak, v_st)]
        x_au = [_mm(t, jnp.concatenate([a, b], axis=1)) for t, a, b in zip(t_inv, a_s, akv)]
        bk_t = [jnp.concatenate([b.T, k.T], axis=1) for b, k in zip(b_hat, k_hat)]
        zz = [_mm(jnp.concatenate([a, t], axis=0), jnp.concatenate([x, jnp.concatenate([zero, u], axis=1)], axis=0))
              for a, t, x, u in zip(a_rbk, bk_t, x_au, v_st)]
        rw_p = [jnp.concatenate([x + y[0:n2, 0:LANES], y[n2:2 * n2, 0:LANES]], axis=0) for x, y in zip(r_t, zz)]
        g_col = [jnp.exp(c.T[:, CHUNK - 1:CHUNK]) for c in ci]
        sts = [st_ref[p] for p in range(npair)]
        ys = []
        for idx, (j, p) in enumerate(jobs):
            st = sts[p]
            both = _mm(rw_p[idx], st)
            ys.append(both[0:n2] + zz[idx][0:n2, LANES:2 * LANES])
            sts[p] = g_col[idx] * st + both[n2:2 * n2] + zz[idx][n2:2 * n2, LANES:2 * LANES]
        for p in range(npair):
            st_ref[p] = sts[p]
        lanes = [slice(p * LANES, (p + 1) * LANES) for _, p in jobs]
        y = [t[0:CHUNK] + t[CHUNK:n2] for t in ys]
        sums = [_seg_sum(jnp.concatenate([a, b * k * rk_ref[:, ln]], axis=0), ones_bd)
                for a, b, k, ln in zip(y, r, kx, lanes)]
        d = [a - t[0:CHUNK] * (1.0 / RW_HEAD) for a, t in zip(y, sums)]
        var = [_seg_sum(a * a, ones_bd) * (1.0 / RW_HEAD) for a in d]
        for idx, (j, p) in enumerate(jobs):
            yn = d[idx] * lax.rsqrt(var[idx] + RW_GN_EPS) * gng_ref[:, lanes[idx]] + gnb_ref[:, lanes[idx]]
            bonus = sums[idx][CHUNK:n2] * v[idx]
            o_ref[rows[j], lanes[idx]] = ((yn + bonus) * g_s[rows[j], lanes[idx]]).astype(o_ref.dtype)
        return carry

    lax.fori_loop(0, tb // sb, step, 0)


def _rw_call(x, params, bsz, seq, tb):
    scratch = ([pltpu.VMEM((tb + 8, RW_PAD), F32)] + [pltpu.VMEM((tb, RW_WIDTH), F32)] * 8
               + [pltpu.VMEM((RW_HEADS // 2, LANES, LANES), F32)])
    return _mixer_call(_rw_body, x, params, RW_WIDTH, scratch, bsz, seq, tb, "rwkv7_time_mix")


def _pad_to(a, size, axis):
    pad = [(0, 0)] * a.ndim
    pad[axis] = (0, size - a.shape[axis])
    return jnp.pad(a, pad)


def _mixer_in_weights(w_in):
    o = 0
    rw = w_in[..., o:o + RW_IN]
    o += RW_IN
    gla = w_in[..., o:o + GLA_IN]
    o += GLA_IN
    gdn = w_in[..., o:o + GDN_IN]
    o += GDN_IN
    gate = w_in[..., o:]
    qkv_w = 2 * GLA_HEADS * GLA_DK + GLA_HEADS * GLA_DV
    gla_p = jnp.concatenate([gla[..., 0:qkv_w], gla[..., qkv_w + GLA_GATE_LORA:], gla[..., qkv_w:qkv_w + GLA_GATE_LORA]], axis=-1)
    gdn_p = jnp.concatenate([gdn[..., 0:GDN_QKV], gdn[..., GDN_QKV + 2 * GDN_HEADS:], gdn[..., GDN_QKV:GDN_QKV + 2 * GDN_HEADS]], axis=-1)
    return (_pad_to(rw, RW_PAD, 2).astype(BF16), _pad_to(gla_p, GLA_PAD, 2).astype(BF16),
            _pad_to(gdn_p, GDN_PAD, 2).astype(BF16), gate.astype(BF16))


def _rows(v):
    return v.reshape(v.shape[:-1] + (1, v.shape[-1]))


def _tiles(n, seq):
    return min(512, n), min(512, seq), min(1024, seq)


def kernel(x, p, ln_g, ln_b, ffn_w1, ffn_w3, ffn_w2, w_in, rw_mu, rw_w0, rw_w2, rw_a0, rw_a2, rw_g2, rw_k_k, rw_k_a, rw_r_k, rw_gn_g, rw_gn_b, gla_gk_w2, gla_gk_b, gla_norm_g, gdn_conv_w, gdn_a_log, gdn_dt_bias, gdn_norm_g, w_branch, w_o, ple_w_gate, ple_w_proj):
    bsz, seq, d = x.shape
    depth = p.shape[0]
    n = bsz * seq
    tm, tb, tb_gla = _tiles(n, seq)
    x = x.reshape(n, d)
    p = p.reshape(depth, n, PLE_DIM)

    w1, w3, w2 = ffn_w1.astype(BF16), ffn_w3.astype(BF16), ffn_w2.astype(BF16)
    w_rw, w_gla, w_gdn, w_gate = _mixer_in_weights(w_in)
    w_br, w_out = w_branch.astype(BF16), w_o.astype(BF16)
    w_pg, w_pp = ple_w_gate.astype(BF16), ple_w_proj.astype(BF16)
    lng, lnb = _rows(ln_g), _rows(ln_b)
    rw_vecs = [_rows(v) for v in (rw_k_k, rw_k_a, rw_r_k.reshape(depth, RW_WIDTH), rw_gn_g, rw_gn_b)]
    rw_mu_p = _rows(_pad_to(rw_mu, RW_PAD, 1))
    rw_w0_r, rw_a0_r = _rows(rw_w0), _rows(rw_a0)
    rw_w2_p = _pad_to(rw_w2, LANES, 1)
    rw_a2_p = jnp.pad(rw_a2, ((0, 0), (RW_DECAY_LORA, 0), (0, 0)))
    rw_g2_p = _pad_to(rw_g2, 2 * LANES, 1)
    gla_w2_p, gla_b_r, gla_ng = _pad_to(gla_gk_w2, LANES, 1), _rows(gla_gk_b), _rows(gla_norm_g)
    gdn_alog, gdn_dtb, gdn_ng = _rows(_pad_to(gdn_a_log, LANES, 1)), _rows(_pad_to(gdn_dt_bias, LANES, 1)), _rows(gdn_norm_g)

    for i in range(depth):
        x = _ffn_call(x, [_param(w1, i, 0), _param(w3, i, 0), _param(w2, i, 0), _param(lng, i, 0), _param(lnb, i, 0)], tm)
        o_rw = _rw_call(x, [_param(w_rw, i), _param(rw_mu_p, i), _param(rw_w0_r, i), _param(rw_w2_p, i),
                            _param(rw_a0_r, i), _param(rw_a2_p, i), _param(rw_g2_p, i)]
                        + [_param(v, i) for v in rw_vecs], bsz, seq, tb)
        o_gla = _gla_call(x, [_param(w_gla, i), _param(gla_w2_p, i), _param(gla_b_r, i), _param(gla_ng, i)], bsz, seq, tb_gla)
        o_gdn = _gdn_call(x, [_param(w_gdn, i), _param(gdn_conv_w, i), _param(gdn_alog, i), _param(gdn_dtb, i),
                              _param(gdn_ng, i)], bsz, seq, tb_gla)
        x = _merge_call(x, (o_rw, o_gla, o_gdn), [_param(w_gate, i), _param(w_br, i), _param(w_out, i),
                                                  _param(lng, i, 1), _param(lnb, i, 1)], tm)
        x = _ffn_ple_call(x, p, i, [_param(w1, i, 1), _param(w3, i, 1), _param(w2, i, 1), _param(lng, i, 2), _param(lnb, i, 2),
                                    _param(w_pg, i), _param(w_pp, i), _param(lng, i, 3), _param(lnb, i, 3)], tm)
    return x.reshape(bsz, seq, d)
```

```python
import functools

import jax
import jax.numpy as jnp
from jax import lax
from jax.experimental import pallas as pl
from jax.experimental.pallas import tpu as pltpu

F32 = jnp.float32
BF16 = jnp.bfloat16

D_MODEL = 1024
DEPTH = 2
PLE_DIM = 256
D_FF = 2816
LN_EPS = 1e-5
DEEPNORM_ALPHA = (2 * DEPTH) ** 0.25
MACARON_W = 0.5

RW_HEADS = 8
RW_HEAD = 64
RW_WIDTH = RW_HEADS * RW_HEAD
RW_DECAY_LORA = 64
RW_AAA_LORA = 64
RW_GATE_LORA = 160
RW_GN_EPS = 1e-5 * RW_HEAD
RW_IN = 3 * RW_WIDTH + RW_DECAY_LORA + RW_AAA_LORA + RW_GATE_LORA

GLA_HEADS = 4
GLA_DK = 64
GLA_DV = 128
GLA_GATE_LORA = 16
GLA_GATE_NORM = 16.0
GLA_IN = 2 * GLA_HEADS * GLA_DK + 2 * GLA_HEADS * GLA_DV + GLA_GATE_LORA

GDN_HEADS = 4
GDN_DK = 128
GDN_DV = 128
GDN_CONV = 4
GDN_QKV = GDN_HEADS * (2 * GDN_DK + GDN_DV)
GDN_IN = GDN_QKV + 2 * GDN_HEADS + GDN_HEADS * GDN_DV

HEAD_NORM_EPS = 1e-6
L2_EPS = 1e-6
N_BRANCH = 3
BRANCH_WIDTH = 512

LANES = 128
SUBLANES = 8
CHUNK = 64
GDN_CHUNKS_PER_WAVE = 8
RW_CHUNKS_PER_STEP = 4
GLA_CHUNKS_PER_STEP = 4
RW_PAD = 1920
GLA_PAD = 1664
GDN_PAD = 2176
MIX_PAD = RW_PAD + GLA_PAD + GDN_PAD
VMEM_LIMIT = 56 * 1024 * 1024


def _param(arr, *prefix):
    tail = arr.shape[len(prefix):]
    zeros = (0,) * len(tail)
    spec = pl.BlockSpec((None,) * len(prefix) + tail, lambda *_: prefix + zeros, pipeline_mode=pl.Buffered(1))
    return arr, spec


def _call(body, grid, sem, streamed, params, out_specs, out_shape, scratch, name):
    ops = list(streamed) + list(params)
    return pl.pallas_call(
        body,
        grid=grid,
        in_specs=[spec for _, spec in ops],
        out_specs=out_specs,
        out_shape=out_shape,
        scratch_shapes=scratch,
        compiler_params=pltpu.CompilerParams(dimension_semantics=sem, vmem_limit_bytes=VMEM_LIMIT),
        name=name,
    )(*[arr for arr, _ in ops])


def _mm(a, b):
    return jnp.dot(a.astype(BF16), b.astype(BF16), preferred_element_type=F32)


def _mm_nt(a, b):
    return lax.dot_general(a.astype(BF16), b.astype(BF16), (((1,), (1,)), ((), ())),
                           preferred_element_type=F32)


def _split_bf16(x, parts):
    out = []
    for _ in range(parts - 1):
        piece = x.astype(BF16)
        out.append(piece)
        x = x - piece.astype(F32)
    out.append(x.astype(BF16))
    return out


def _mm_f32(a, b):
    a_hi, a_lo = _split_bf16(a, 2)
    b_hi, b_lo = _split_bf16(b, 2)
    dot = functools.partial(jnp.dot, preferred_element_type=F32)
    return dot(a_hi, b_hi) + (dot(a_hi, b_lo) + dot(a_lo, b_hi))


def _chunk_cumsum(x):
    tril = _tril_f32(CHUNK).astype(BF16)
    out = []
    for r0 in range(0, x.shape[0], CHUNK):
        pieces = _split_bf16(x[r0:r0 + CHUNK], 3)
        acc = jnp.dot(tril, pieces[2], preferred_element_type=F32)
        acc = acc + jnp.dot(tril, pieces[1], preferred_element_type=F32)
        out.append(acc + jnp.dot(tril, pieces[0], preferred_element_type=F32))
    return jnp.concatenate(out, axis=0)


def _layer_norm(y, g, b):
    m = jnp.mean(y, -1, keepdims=True)
    d = y - m
    var = jnp.mean(d * d, -1, keepdims=True)
    return d * lax.rsqrt(var + LN_EPS) * g + b


def _sigmoid(x):
    return 0.5 * jnp.tanh(0.5 * x) + 0.5


def _softplus(x):
    return jnp.maximum(x, 0.0) + jnp.log(1.0 + jnp.exp(-jnp.abs(x)))


def _iota2(shape):
    return (lax.broadcasted_iota(jnp.int32, shape, 0), lax.broadcasted_iota(jnp.int32, shape, 1))


def _tril_f32(n):
    r, c = _iota2((n, n))
    return jnp.where(r >= c, 1.0, 0.0).astype(F32)


def _sibling_mask(r, c, s):
    k = s.bit_length() - 1
    return ((r >> (k + 1)) == (c >> (k + 1))) & (((r >> k) & 1) == 1) & (((c >> k) & 1) == 0)


def _unit_lower_inverses(lows, n, between=lambda: None):
    r, c = _iota2((n, n))
    eye = jnp.where(r == c, 1.0, 0.0).astype(F32)
    first = _sibling_mask(r, c, 1)
    xs = [eye - jnp.where(first, low, 0.0) for low in lows]
    s = 2
    while s < CHUNK:
        sib = _sibling_mask(r, c, s)
        if s % SUBLANES:
            ts = [_mm(jnp.where(sib, low, 0.0), x) for low, x in zip(lows, xs)]
            between()
            us = [_mm(x, t) for x, t in zip(xs, ts)]
            between()
            xs = [x - u for x, u in zip(xs, us)]
        else:
            blank = jnp.zeros((s, n), F32)

            def odd(a, s=s):
                return jnp.concatenate([a[g * s:(g + 1) * s] for g in range(1, n // s, 2)], axis=0)

            def spread(a, s=s, blank=blank):
                return jnp.concatenate([piece for g in range(n // (2 * s)) for piece in (blank, a[g * s:(g + 1) * s])],
                                       axis=0)

            ts = [_mm(odd(jnp.where(sib, low, 0.0)), x) for low, x in zip(lows, xs)]
            between()
            us = [_mm(odd(x), spread(t)) for x, t in zip(xs, ts)]
            between()
            xs = [x - spread(u) for x, u in zip(xs, us)]
        s *= 2
    return xs


def _stack_heads(z):
    lane = lax.broadcasted_iota(jnp.int32, z.shape, 1)
    first = lane < (LANES // 2)
    return jnp.concatenate([jnp.where(first, z, 0.0), jnp.where(first, 0.0, z)], axis=0)


def _block_causal(n, strict):
    r, c = _iota2((n, n))
    same = (r >> 6) == (c >> 6)
    return same & ((r > c) if strict else (r >= c))


FF_CHUNK = 256


def _swiglu_ln(x, w1_ref, w3_ref, w2_ref, g, b):
    xb = x.astype(BF16)

    acc = None
    for c0 in range(0, D_FF, FF_CHUNK):
        h1 = jnp.dot(xb, w1_ref[:, c0:c0 + FF_CHUNK], preferred_element_type=F32)
        h3 = jnp.dot(xb, w3_ref[:, c0:c0 + FF_CHUNK], preferred_element_type=F32)
        act = (h1 * _sigmoid(h1) * h3).astype(BF16)
        part = jnp.dot(act, w2_ref[c0:c0 + FF_CHUNK, :], preferred_element_type=F32)
        acc = part if acc is None else acc + part
    return _layer_norm(DEEPNORM_ALPHA * x + MACARON_W * acc, g, b)


def _ffn_body(x_ref, w1_ref, w3_ref, w2_ref, g_ref, b_ref, o_ref):
    o_ref[...] = _swiglu_ln(x_ref[...], w1_ref, w3_ref, w2_ref, g_ref[...], b_ref[...])


def _ffn_ple_body(x_ref, p_ref, w1_ref, w3_ref, w2_ref, g_ref, b_ref, wg_ref, wp_ref, g2_ref, b2_ref, o_ref):
    x = _swiglu_ln(x_ref[...], w1_ref, w3_ref, w2_ref, g_ref[...], b_ref[...])
    gate = _sigmoid(jnp.dot(x.astype(BF16), wg_ref[...], preferred_element_type=F32))
    emb = jnp.dot(p_ref[...].astype(BF16), wp_ref[...], preferred_element_type=F32)
    o_ref[...] = _layer_norm(DEEPNORM_ALPHA * x + gate * emb, g2_ref[...], b2_ref[...])


def _token_rows(x, tm):
    return x, pl.BlockSpec((tm, x.shape[1]), lambda i: (i, 0))


def _ffn_call(x, params, tm):
    n = x.shape[0]
    return _call(_ffn_body, (n // tm,), ("arbitrary",), [_token_rows(x, tm)], params,
                 pl.BlockSpec((tm, D_MODEL), lambda i: (i, 0)), jax.ShapeDtypeStruct((n, D_MODEL), F32), [], "ffn_ln")


def _ffn_ple_call(x, p, layer, params, tm):
    n = x.shape[0]
    p_rows = (p, pl.BlockSpec((None, tm, PLE_DIM), lambda i: (layer, i, 0)))
    return _call(_ffn_ple_body, (n // tm,), ("arbitrary",), [_token_rows(x, tm), p_rows], params,
                 pl.BlockSpec((tm, D_MODEL), lambda i: (i, 0)), jax.ShapeDtypeStruct((n, D_MODEL), F32), [], "ffn_ple_ln")


def _merge_body(x_ref, orw_ref, ogla_ref, ogdn_ref, wgate_ref, wbr_ref, wo_ref, g_ref, b_ref, o_ref):
    x = x_ref[...]
    xb = x.astype(BF16)
    merged = None
    for n, br_ref in enumerate((orw_ref, ogla_ref, ogdn_ref)):
        gate = _sigmoid(jnp.dot(xb, wgate_ref[:, n * D_MODEL:(n + 1) * D_MODEL], preferred_element_type=F32))
        term = gate * jnp.dot(br_ref[...], wbr_ref[n], preferred_element_type=F32)
        merged = term if merged is None else merged + term
    mix = jnp.dot(merged.astype(BF16), wo_ref[...], preferred_element_type=F32)
    o_ref[...] = _layer_norm(DEEPNORM_ALPHA * x + mix, g_ref[...], b_ref[...])


def _merge_call(x, branches, params, tm):
    n = x.shape[0]
    return _call(_merge_body, (n // tm,), ("arbitrary",), [_token_rows(t, tm) for t in (x,) + tuple(branches)], params,
                 pl.BlockSpec((tm, D_MODEL), lambda i: (i, 0)), jax.ShapeDtypeStruct((n, D_MODEL), F32), [], "merge_ln")


def _gdn_body(x_ref, w_ref, cw_ref, alog_ref, dtb_ref, ng_ref, o_ref,
              ext_ref, q_s, k_s, v_s, g_s, beta_s, gate_s, st_ref, *, tb):
    t_idx = pl.program_id(1)

    @pl.when(t_idx == 0)
    def _():
        ext_ref[0:8, :] = jnp.zeros((8, GDN_QKV), F32)
        st_ref[...] = jnp.zeros_like(st_ref)

    @pl.when(t_idx != 0)
    def _():
        ext_ref[0:8, :] = ext_ref[tb:tb + 8, :]

    hw = GDN_HEADS * GDN_DK
    wave_rows = GDN_CHUNKS_PER_WAVE * CHUNK

    def prep_units(wave):
        r0 = wave * wave_rows
        rows = slice(r0, r0 + wave_rows)
        vals = {}

        def project(c0, c1):
            def run():
                if "xb" not in vals:
                    vals["xb"] = x_ref[rows, :].astype(BF16)
                ext_ref[8 + r0:8 + r0 + wave_rows, c0:c1] = jnp.dot(vals["xb"], w_ref[:, c0:c1],
                                                                      preferred_element_type=F32)
            return run

        def gates():
            xb = vals["xb"]
            gate_s[rows, :] = jnp.dot(xb, w_ref[:, GDN_QKV:GDN_QKV + GDN_HEADS * GDN_DV], preferred_element_type=F32)
            ab = jnp.dot(xb, w_ref[:, GDN_QKV + GDN_HEADS * GDN_DV:GDN_PAD], preferred_element_type=F32)
            g_s[rows, :] = _chunk_cumsum(-jnp.exp(alog_ref[...]) * _softplus(ab + dtb_ref[...]))
            beta_s[rows, :] = _sigmoid(ab)

        def conv(grp, dst):
            def run():
                cols = slice(grp * hw, (grp + 1) * hw)
                acc = ext_ref[8 + r0:8 + r0 + wave_rows, cols] * cw_ref[GDN_CONV - 1:GDN_CONV, cols]
                for d in range(1, GDN_CONV):
                    acc = acc + ext_ref[8 + r0 - d:8 + r0 - d + wave_rows, cols] * cw_ref[GDN_CONV - 1 - d:GDN_CONV - d, cols]
                act = acc * _sigmoid(acc)
                if grp == 2:
                    dst[rows, :] = act
                else:
                    scale = GDN_DK ** -0.5 if grp == 0 else 1.0
                    for h in range(GDN_HEADS):
                        t = act[:, h * GDN_DK:(h + 1) * GDN_DK]
                        dst[rows, h * GDN_DK:(h + 1) * GDN_DK] = t * (lax.rsqrt(jnp.sum(t * t, -1, keepdims=True) + L2_EPS) * scale)
            return run

        return ([project(g * hw, (g + 1) * hw) for g in range(3)] + [gates]
                + [conv(g, dst) for g, dst in enumerate((q_s, k_s, v_s))])

    n2 = 2 * CHUNK
    ng = ng_ref[...]

    causal = _block_causal(n2, strict=False)
    strict = _block_causal(n2, strict=True)
    first = lax.broadcasted_iota(jnp.int32, (GDN_DK, n2), 1) < CHUNK
    pending = []

    def fill():
        if pending:
            pending.pop(0)()

    def pair_rows(src, rows, pr, width):
        return jnp.concatenate([src[rows, h * width:(h + 1) * width] for h in (2 * pr, 2 * pr + 1)], axis=0)

    def pair_col(a, pr, off):
        return jnp.concatenate([a[:, off + h:off + h + 1] for h in (2 * pr, 2 * pr + 1)], axis=0)

    def precompute(chunks):
        jobs = [(j, pr) for j in range(len(chunks)) for pr in range(GDN_HEADS // 2)]
        rows = [slice(c * CHUNK, (c + 1) * CHUNK) for c in chunks]
        gcs = [g_s[rw, :] for rw in rows]
        betas = [beta_s[rw, :] for rw in rows]
        gi = [pair_col(gcs[j], pr, 0) for j, pr in jobs]
        bcol = [pair_col(betas[j], pr, GDN_HEADS) for j, pr in jobs]
        qs = [pair_rows(q_s, rows[j], pr, GDN_DK) for j, pr in jobs]
        ks = [pair_rows(k_s, rows[j], pr, GDN_DK) for j, pr in jobs]
        vs = [pair_rows(v_s, rows[j], pr, GDN_DV) for j, pr in jobs]
        dec = []
        for g in gi:
            gi_b = jnp.broadcast_to(g, (n2, n2))
            diff = gi_b - gi_b.T
            dec.append(jnp.where(causal, jnp.exp(jnp.where(causal, diff, 0.0)), 0.0))
        kb = [k * b for k, b in zip(ks, bcol)]
        gram = [_mm_nt(jnp.concatenate([a, q], axis=0), k) for a, q, k in zip(kb, qs, ks)]
        fill()
        low = [jnp.where(strict, g[0:n2] * d, 0.0) for g, d in zip(gram, dec)]
        att = [g[n2:2 * n2] * d for g, d in zip(gram, dec)]
        t_inv = _unit_lower_inverses(low, n2, between=fill)
        eg = [jnp.exp(g) for g in gi]
        wu = [_mm(t, jnp.concatenate([a * e, v * b], axis=1))
              for t, a, e, v, b in zip(t_inv, kb, eg, vs, bcol)]
        fill()
        aw = [_mm(a, x) for a, x in zip(att, wu)]
        fill()
        g_end = [jnp.concatenate([jnp.broadcast_to(gcs[j][CHUNK - 1:CHUNK, h:h + 1], (CHUNK, 1))
                                  for h in (2 * pr, 2 * pr + 1)], axis=0) for j, pr in jobs]
        k_hat_t = [(k * jnp.exp(ge - g)).T for k, ge, g in zip(ks, g_end, gi)]
        kwu2 = [_mm(jnp.concatenate([jnp.where(first, kt, 0.0), jnp.where(first, 0.0, kt)], axis=0), x)
                for kt, x in zip(k_hat_t, wu)]
        fill()
        per_head = {}
        for idx, (j, pr) in enumerate(jobs):
            for half, h in enumerate((2 * pr, 2 * pr + 1)):
                sl = slice(half * CHUNK, (half + 1) * CHUNK)
                kwu = kwu2[idx][half * GDN_DK:(half + 1) * GDN_DK]
                q_eff = qs[idx][sl] * eg[idx][sl] - aw[idx][sl, 0:GDN_DV]
                g_dec = jnp.exp(gcs[j][CHUNK - 1:CHUNK, h:h + 1])
                per_head[(chunks[j], h)] = (q_eff, aw[idx][sl, GDN_DV:2 * GDN_DV], g_dec, kwu)
        return per_head

    sts = [st_ref[h] for h in range(GDN_HEADS)]

    def chain_step(c, per_head):
        def run():
            rows = slice(c * CHUNK, (c + 1) * CHUNK)
            for h in range(GDN_HEADS):
                q_eff, o_loc, g_dec, kwu = per_head[(c, h)]
                st = sts[h]
                both = _mm(jnp.concatenate([q_eff, kwu[:, 0:GDN_DV]], axis=0), st)
                o = both[0:CHUNK] + o_loc
                sts[h] = g_dec * st - both[CHUNK:CHUNK + GDN_DK] + kwu[:, GDN_DV:2 * GDN_DV]
                o = o * lax.rsqrt(jnp.mean(o * o, -1, keepdims=True) + HEAD_NORM_EPS) * ng
                gate = gate_s[rows, h * GDN_DV:(h + 1) * GDN_DV]
                o_ref[rows, h * GDN_DV:(h + 1) * GDN_DV] = (o * gate * _sigmoid(gate)).astype(o_ref.dtype)
        return run

    n_waves = tb // wave_rows
    for unit in prep_units(0):
        unit()
    for wave in range(n_waves):
        chunks = list(range(wave * GDN_CHUNKS_PER_WAVE, (wave + 1) * GDN_CHUNKS_PER_WAVE))
        if wave + 1 < n_waves:
            pending.extend(prep_units(wave + 1))
        per_head = precompute(chunks)
        while pending:
            fill()
        pending.extend(chain_step(c, per_head) for c in chunks)
    while pending:
        fill()
    for h in range(GDN_HEADS):
        st_ref[h] = sts[h]


def _mixer_call(body, x, params, width, scratch, bsz, seq, tb, name):
    nt = seq // tb
    rows = lambda b, t: (b * nt + t, 0)
    return _call(functools.partial(body, tb=tb), (bsz, nt), ("arbitrary", "arbitrary"),
                 [(x, pl.BlockSpec((tb, D_MODEL), rows))], params,
                 pl.BlockSpec((tb, width), rows), jax.ShapeDtypeStruct((bsz * seq, width), BF16), scratch, name)


def _gdn_call(x, params, bsz, seq, tb):
    wide = pltpu.VMEM((tb, GDN_HEADS * GDN_DK), F32)
    narrow = pltpu.VMEM((tb, LANES), F32)
    scratch = [pltpu.VMEM((tb + 8, GDN_QKV), F32), wide, wide, wide, narrow, narrow, wide,
               pltpu.VMEM((GDN_HEADS, GDN_DK, GDN_DV), F32)]
    return _mixer_call(_gdn_body, x, params, GDN_HEADS * GDN_DV, scratch, bsz, seq, tb, "gated_deltanet")


def _gla_level_arg(s, la, b, ri):
    c = CHUNK
    if s >= 4:
        ref = jnp.concatenate(
            [jnp.broadcast_to(b[g * 2 * s + s - 1:g * 2 * s + s, :], (2 * s, LANES)) for g in range(c // (2 * s))],
            axis=0)
        return -jnp.abs(b - ref)
    if s == 2:
        m = ri & 3
        up = pltpu.roll(la, c - 1, 0)
        dn = pltpu.roll(la, 1, 0)
        return jnp.where(m == 0, up, jnp.where(m == 1, 0.0, jnp.where(m == 2, la, la + dn)))
    return jnp.where((ri & 1) == 1, la, 0.0)


def _gla_scores(qs, ks, las, bs, between):
    c = CHUNK
    ri = lax.broadcasted_iota(jnp.int32, (c, LANES), 0)
    r2, c2 = _iota2((2 * c, c))
    r2 = r2 & (c - 1)
    diag = r2 == c2
    scores = [jnp.where(diag, _mm_nt(_stack_heads(q), k), 0.0) for q, k in zip(qs, ks)]
    between()
    s = c // 2
    while s >= 1:
        row_side = ((ri >> (s.bit_length() - 1)) & 1) == 1
        sib = _sibling_mask(r2, c2, s)
        es = [jnp.exp(_gla_level_arg(s, la, b, ri)) for la, b in zip(las, bs)]
        part = [_mm_nt(_stack_heads(jnp.where(row_side, q * e, 0.0)), jnp.where(row_side, 0.0, k * e))
                for q, k, e in zip(qs, ks, es)]
        between()
        scores = [sc + jnp.where(sib, p, 0.0) for sc, p in zip(scores, part)]
        s //= 2
    return scores


def _gla_body(x_ref, w_ref, w2_ref, gkb_ref, ng_ref, o_ref, h_s, la_s, b_s, st_ref, *, tb):
    t_idx = pl.program_id(1)

    @pl.when(t_idx == 0)
    def _():
        st_ref[...] = jnp.zeros_like(st_ref)

    hk = GLA_HEADS * GLA_DK
    hv = GLA_HEADS * GLA_DV
    ng = ng_ref[...]
    cpi = GLA_CHUNKS_PER_STEP
    sb = cpi * CHUNK
    npair = GLA_HEADS // 2
    jobs = [(j, p) for j in range(cpi) for p in range(npair)]
    row_first = lax.broadcasted_iota(jnp.int32, (LANES, GLA_DV), 0) < GLA_DK

    def prep_units(s):
        rows = slice(s * sb, (s + 1) * sb)
        vals = {}

        def proj(c0, c1):
            def run():
                if "xb" not in vals:
                    vals["xb"] = x_ref[rows, :].astype(BF16)
                h_s[rows, c0:c1] = jnp.dot(vals["xb"], w_ref[:, c0:c1], preferred_element_type=F32)
            return run

        def decay():
            gk_lo = jnp.dot(vals["xb"], w_ref[:, 2 * hk + 2 * hv:GLA_PAD], preferred_element_type=F32)
            z = _mm_f32(gk_lo, w2_ref[...]) + gkb_ref[...]
            la = (jnp.minimum(z, 0.0) - jnp.log(1.0 + jnp.exp(-jnp.abs(z)))) * (1.0 / GLA_GATE_NORM)
            la_s[rows, :] = la
            b_s[rows, :] = _chunk_cumsum(la)

        return [proj(c0, c0 + 2 * LANES) for c0 in range(0, 2 * hk + 2 * hv, 2 * LANES)] + [decay]

    def recurrence(s, pending):
        rows = [slice(s * sb + j * CHUNK, s * sb + (j + 1) * CHUNK) for j in range(cpi)]

        def fill():
            if pending:
                pending.pop(0)()
        qs = [h_s[rows[j], p * LANES:(p + 1) * LANES] * (GLA_DK ** -0.5) for j, p in jobs]
        ks = [h_s[rows[j], hk + p * LANES:hk + (p + 1) * LANES] for j, p in jobs]
        las = [la_s[rows[j], p * LANES:(p + 1) * LANES] for j, p in jobs]
        bs = [b_s[rows[j], p * LANES:(p + 1) * LANES] for j, p in jobs]
        vpair = [h_s[rows[j], 2 * hk + 2 * p * GLA_DV:2 * hk + (2 * p + 2) * GLA_DV] for j, p in jobs]
        scores = _gla_scores(qs, ks, las, bs, fill)
        q_in = [_stack_heads(q * jnp.exp(b)) for q, b in zip(qs, bs)]
        bts = [b.T for b in bs]
        b_end = [bt[:, CHUNK - 1:CHUNK] for bt in bts]
        upd = [_mm(k.T * jnp.exp(be - bt), vp) for k, be, bt, vp in zip(ks, b_end, bts, vpair)]
        intra = [[_mm(sc[half * CHUNK:(half + 1) * CHUNK], vp[:, half * GLA_DV:(half + 1) * GLA_DV])
                  for half in range(2)] for sc, vp in zip(scores, vpair)]
        sts = [st_ref[p] for p in range(npair)]
        for idx, (j, p) in enumerate(jobs):
            st = sts[p]
            inter = _mm(q_in[idx], st)
            sts[p] = st * jnp.exp(b_end[idx]) + jnp.where(row_first, upd[idx][:, 0:GLA_DV], upd[idx][:, GLA_DV:2 * GLA_DV])
            for half in range(2):
                h = 2 * p + half
                o = intra[idx][half] + inter[half * CHUNK:(half + 1) * CHUNK]
                o = o * lax.rsqrt(jnp.mean(o * o, -1, keepdims=True) + HEAD_NORM_EPS) * ng
                gate = h_s[rows[j], 2 * hk + hv + h * GLA_DV:2 * hk + hv + (h + 1) * GLA_DV]
                o_ref[rows[j], h * GLA_DV:(h + 1) * GLA_DV] = (o * gate * _sigmoid(gate)).astype(o_ref.dtype)
        for p in range(npair):
            st_ref[p] = sts[p]
        while pending:
            fill()

    for unit in prep_units(0):
        unit()
    for s in range(tb // sb):
        recurrence(s, prep_units(s + 1) if (s + 1) * sb < tb else [])


def _gla_call(x, params, bsz, seq, tb):
    keys = pltpu.VMEM((tb, GLA_HEADS * GLA_DK), F32)
    scratch = [pltpu.VMEM((tb, GLA_PAD - LANES), F32), keys, keys,
               pltpu.VMEM((GLA_HEADS // 2, 2 * GLA_DK, GLA_DV), F32)]
    return _mixer_call(_gla_body, x, params, GLA_HEADS * GLA_DV, scratch, bsz, seq, tb, "gated_linear_attention")


def _seg_sum(x, ones_bd):
    n = x.shape[0]
    both = jnp.dot(jnp.concatenate(_split_bf16(x, 2), axis=0), ones_bd, preferred_element_type=F32)
    return both[0:n] + both[n:2 * n]


def _rw_body(x_ref, w_ref, mu_ref, w0_ref, w2_ref, a0_ref, a2_ref, g2_ref, kk_ref, ka_ref, rk_ref, gng_ref, gnb_ref,
             o_ref, ext_ref, r_s, k_s, v_s, ld_s, ci_s, an_s, bn_s, g_s, st_ref, *, tb):
    t_idx = pl.program_id(1)

    @pl.when(t_idx == 0)
    def _():
        ext_ref[0:8, :] = jnp.zeros((8, RW_PAD), F32)
        st_ref[...] = jnp.zeros_like(st_ref)

    @pl.when(t_idx != 0)
    def _():
        ext_ref[0:8, :] = ext_ref[tb:tb + 8, :]

    w = RW_WIDTH
    n2 = 2 * CHUNK
    cpi = RW_CHUNKS_PER_STEP
    sb = cpi * CHUNK
    npair = RW_HEADS // 2
    jobs = [(j, p) for j in range(cpi) for p in range(npair)]
    r2, c2 = _iota2((LANES, LANES))
    ones_bd = jnp.where((r2 >> 6) == (c2 >> 6), 1.0, 0.0).astype(BF16)
    strict = _block_causal(n2, strict=True)
    incl = _block_causal(n2, strict=False)
    zero = jnp.zeros((n2, LANES), F32)

    ext_ref[8:tb + 8, :] = jnp.dot(x_ref[...].astype(BF16), w_ref[...], preferred_element_type=F32)

    def shifted(cols):
        cur = ext_ref[8:tb + 8, cols]
        return cur + (ext_ref[7:tb + 7, cols] - cur) * mu_ref[:, cols]

    lo = shifted(slice(3 * w, 3 * w + LANES))
    wv = -_softplus(-(w0_ref[...] + _mm_f32(jnp.tanh(lo), w2_ref[...]))) - 0.5
    ld = -jnp.exp(wv)
    ld_s[...] = ld
    ci_s[...] = _chunk_cumsum(ld)
    a = _sigmoid(a0_ref[...] + _mm(lo, a2_ref[...]))
    glo = shifted(slice(3 * w + LANES, RW_PAD))
    g_s[...] = _mm(_sigmoid(glo), g2_ref[...])
    r_s[...] = shifted(slice(0, w))
    v_s[...] = shifted(slice(2 * w, 3 * w))
    for p in range(npair):
        lanes = slice(p * LANES, (p + 1) * LANES)
        k = shifted(slice(w + p * LANES, w + (p + 1) * LANES))
        t = k * kk_ref[:, lanes]
        kkn = t * lax.rsqrt(_seg_sum(t * t, ones_bd) + L2_EPS)
        an_s[:, lanes] = -kkn
        bn_s[:, lanes] = kkn * a[:, lanes]
        k_s[:, lanes] = k * (1.0 + (a[:, lanes] - 1.0) * ka_ref[:, lanes])

    def step(it, carry):
        rows = [pl.ds(pl.multiple_of(it * sb + j * CHUNK, CHUNK), CHUNK) for j in range(cpi)]

        def load(src):
            return [src[rows[j], p * LANES:(p + 1) * LANES] for j, p in jobs]

        r, kx, v, ld, ci, an, bn = (load(src) for src in (r_s, k_s, v_s, ld_s, ci_s, an_s, bn_s))
        e_in = [jnp.exp(-c) for c in ci]
        e_out = [jnp.exp(c[CHUNK - 1:CHUNK, :] - c) for c in ci]
        a_s = [_stack_heads(a * jnp.exp(c - t)) for a, c, t in zip(an, ci, ld)]
        r_t = [_stack_heads(x * jnp.exp(c)) for x, c in zip(r, ci)]
        b_hat = [_stack_heads(b * e) for b, e in zip(bn, e_out)]
        k_hat = [_stack_heads(k * e) for k, e in zip(kx, e_out)]
        v_st = [_stack_heads(x) for x in v]
        gram = [_mm_nt(jnp.concatenate([a, x], axis=0),
                       jnp.concatenate([_stack_heads(b * e), _stack_heads(k * e)], axis=0))
                for a, x, b, k, e in zip(a_s, r_t, bn, kx, e_in)]
        a_ab = [jnp.where(strict, g[0:n2, 0:n2], 0.0) for g in gram]
        a_ak = [jnp.where(strict, g[0:n2, n2:2 * n2], 0.0) for g in gram]
        a_rbk = [jnp.concatenate([jnp.where(incl, g[n2:2 * n2, 0:n2], 0.0),
                                  jnp.where(incl, g[n2:2 * n2, n2:2 * n2], 0.0)], axis=1) for g in gram]
        t_inv = _unit_lower_inverses([-a for a in a_ab], n2)
        akv = [_mm(a, x) for a, x in zip(a_ak, v_st)]
        x_au = [_mm(t, jnp.concatenate([a, b], axis=1)) for t, a, b in zip(t_inv, a_s, akv)]
        bk_t = [jnp.concatenate([b.T, k.T], axis=1) for b, k in zip(b_hat, k_hat)]
        zz = [_mm(jnp.concatenate([a, t], axis=0), jnp.concatenate([x, jnp.concatenate([zero, u], axis=1)], axis=0))
              for a, t, x, u in zip(a_rbk, bk_t, x_au, v_st)]
        rw_p = [jnp.concatenate([x + y[0:n2, 0:LANES], y[n2:2 * n2, 0:LANES]], axis=0) for x, y in zip(r_t, zz)]
        g_col = [jnp.exp(c.T[:, CHUNK - 1:CHUNK]) for c in ci]
        sts = [st_ref[p] for p in range(npair)]
        ys = []
        for idx, (j, p) in enumerate(jobs):
            st = sts[p]
            both = _mm(rw_p[idx], st)
            ys.append(both[0:n2] + zz[idx][0:n2, LANES:2 * LANES])
            sts[p] = g_col[idx] * st + both[n2:2 * n2] + zz[idx][n2:2 * n2, LANES:2 * LANES]
        for p in range(npair):
            st_ref[p] = sts[p]
        lanes = [slice(p * LANES, (p + 1) * LANES) for _, p in jobs]
        y = [t[0:CHUNK] + t[CHUNK:n2] for t in ys]
        sums = [_seg_sum(jnp.concatenate([a, b * k * rk_ref[:, ln]], axis=0), ones_bd)
                for a, b, k, ln in zip(y, r, kx, lanes)]
        d = [a - t[0:CHUNK] * (1.0 / RW_HEAD) for a, t in zip(y, sums)]
        var = [_seg_sum(a * a, ones_bd) * (1.0 / RW_HEAD) for a in d]
        for idx, (j, p) in enumerate(jobs):
            yn = d[idx] * lax.rsqrt(var[idx] + RW_GN_EPS) * gng_ref[:, lanes[idx]] + gnb_ref[:, lanes[idx]]
            bonus = sums[idx][CHUNK:n2] * v[idx]
            o_ref[rows[j], lanes[idx]] = ((yn + bonus) * g_s[rows[j], lanes[idx]]).astype(o_ref.dtype)
        return carry

    lax.fori_loop(0, tb // sb, step, 0)


def _rw_call(x, params, bsz, seq, tb):
    scratch = ([pltpu.VMEM((tb + 8, RW_PAD), F32)] + [pltpu.VMEM((tb, RW_WIDTH), F32)] * 8
               + [pltpu.VMEM((RW_HEADS // 2, LANES, LANES), F32)])
    return _mixer_call(_rw_body, x, params, RW_WIDTH, scratch, bsz, seq, tb, "rwkv7_time_mix")


def _pad_to(a, size, axis):
    pad = [(0, 0)] * a.ndim
    pad[axis] = (0, size - a.shape[axis])
    return jnp.pad(a, pad)


def _mixer_in_weights(w_in):
    o = 0
    rw = w_in[..., o:o + RW_IN]
    o += RW_IN
    gla = w_in[..., o:o + GLA_IN]
    o += GLA_IN
    gdn = w_in[..., o:o + GDN_IN]
    o += GDN_IN
    gate = w_in[..., o:]
    qkv_w = 2 * GLA_HEADS * GLA_DK + GLA_HEADS * GLA_DV
    gla_p = jnp.concatenate([gla[..., 0:qkv_w], gla[..., qkv_w + GLA_GATE_LORA:], gla[..., qkv_w:qkv_w + GLA_GATE_LORA]], axis=-1)
    gdn_p = jnp.concatenate([gdn[..., 0:GDN_QKV], gdn[..., GDN_QKV + 2 * GDN_HEADS:], gdn[..., GDN_QKV:GDN_QKV + 2 * GDN_HEADS]], axis=-1)
    return (_pad_to(rw, RW_PAD, 2).astype(BF16), _pad_to(gla_p, GLA_PAD, 2).astype(BF16),
            _pad_to(gdn_p, GDN_PAD, 2).astype(BF16), gate.astype(BF16))


def _rows(v):
    return v.reshape(v.shape[:-1] + (1, v.shape[-1]))


def _tiles(n, seq):
    return min(512, n), min(512, seq), min(1024, seq)


def kernel(x, p, ln_g, ln_b, ffn_w1, ffn_w3, ffn_w2, w_in, rw_mu, rw_w0, rw_w2, rw_a0, rw_a2, rw_g2, rw_k_k, rw_k_a, rw_r_k, rw_gn_g, rw_gn_b, gla_gk_w2, gla_gk_b, gla_norm_g, gdn_conv_w, gdn_a_log, gdn_dt_bias, gdn_norm_g, w_branch, w_o, ple_w_gate, ple_w_proj):
    bsz, seq, d = x.shape
    depth = p.shape[0]
    n = bsz * seq
    tm, tb, tb_gla = _tiles(n, seq)
    x = x.reshape(n, d)
    p = p.reshape(depth, n, PLE_DIM)

    w1, w3, w2 = ffn_w1.astype(BF16), ffn_w3.astype(BF16), ffn_w2.astype(BF16)
    w_rw, w_gla, w_gdn, w_gate = _mixer_in_weights(w_in)
    w_br, w_out = w_branch.astype(BF16), w_o.astype(BF16)
    w_pg, w_pp = ple_w_gate.astype(BF16), ple_w_proj.astype(BF16)
    lng, lnb = _rows(ln_g), _rows(ln_b)
    rw_vecs = [_rows(v) for v in (rw_k_k, rw_k_a, rw_r_k.reshape(depth, RW_WIDTH), rw_gn_g, rw_gn_b)]
    rw_mu_p = _rows(_pad_to(rw_mu, RW_PAD, 1))
    rw_w0_r, rw_a0_r = _rows(rw_w0), _rows(rw_a0)
    rw_w2_p = _pad_to(rw_w2, LANES, 1)
    rw_a2_p = jnp.pad(rw_a2, ((0, 0), (RW_DECAY_LORA, 0), (0, 0)))
    rw_g2_p = _pad_to(rw_g2, 2 * LANES, 1)
    gla_w2_p, gla_b_r, gla_ng = _pad_to(gla_gk_w2, LANES, 1), _rows(gla_gk_b), _rows(gla_norm_g)
    gdn_alog, gdn_dtb, gdn_ng = _rows(_pad_to(gdn_a_log, LANES, 1)), _rows(_pad_to(gdn_dt_bias, LANES, 1)), _rows(gdn_norm_g)

    for i in range(depth):
        x = _ffn_call(x, [_param(w1, i, 0), _param(w3, i, 0), _param(w2, i, 0), _param(lng, i, 0), _param(lnb, i, 0)], tm)
        o_rw = _rw_call(x, [_param(w_rw, i), _param(rw_mu_p, i), _param(rw_w0_r, i), _param(rw_w2_p, i),
                            _param(rw_a0_r, i), _param(rw_a2_p, i), _param(rw_g2_p, i)]
                        + [_param(v, i) for v in rw_vecs], bsz, seq, tb)
        o_gla = _gla_call(x, [_param(w_gla, i), _param(gla_w2_p, i), _param(gla_b_r, i), _param(gla_ng, i)], bsz, seq, tb_gla)
        o_gdn = _gdn_call(x, [_param(w_gdn, i), _param(gdn_conv_w, i), _param(gdn_alog, i), _param(gdn_dtb, i),
                              _param(gdn_ng, i)], bsz, seq, tb_gla)
        x = _merge_call(x, (o_rw, o_gla, o_gdn), [_param(w_gate, i), _param(w_br, i), _param(w_out, i),
                                                  _param(lng, i, 1), _param(lnb, i, 1)], tm)
        x = _ffn_ple_call(x, p, i, [_param(w1, i, 1), _param(w3, i, 1), _param(w2, i, 1), _param(lng, i, 2), _param(lnb, i, 2),
                                    _param(w_pg, i), _param(w_pp, i), _param(lng, i, 3), _param(lnb, i, 3)], tm)
    return x.reshape(bsz, seq, d)
```

```python
import functools

import jax
import jax.numpy as jnp
from jax import lax
from jax.experimental import pallas as pl
from jax.experimental.pallas import tpu as pltpu

F32 = jnp.float32
BF16 = jnp.bfloat16

D_MODEL = 1024
DEPTH = 2
PLE_DIM = 256
D_FF = 2816
LN_EPS = 1e-5
DEEPNORM_ALPHA = (2 * DEPTH) ** 0.25
MACARON_W = 0.5

RW_HEADS = 8
RW_HEAD = 64
RW_WIDTH = RW_HEADS * RW_HEAD
RW_DECAY_LORA = 64
RW_AAA_LORA = 64
RW_GATE_LORA = 160
RW_GN_EPS = 1e-5 * RW_HEAD
RW_IN = 3 * RW_WIDTH + RW_DECAY_LORA + RW_AAA_LORA + RW_GATE_LORA

GLA_HEADS = 4
GLA_DK = 64
GLA_DV = 128
GLA_GATE_LORA = 16
GLA_GATE_NORM = 16.0
GLA_IN = 2 * GLA_HEADS * GLA_DK + 2 * GLA_HEADS * GLA_DV + GLA_GATE_LORA

GDN_HEADS = 4
GDN_DK = 128
GDN_DV = 128
GDN_CONV = 4
GDN_QKV = GDN_HEADS * (2 * GDN_DK + GDN_DV)
GDN_IN = GDN_QKV + 2 * GDN_HEADS + GDN_HEADS * GDN_DV

HEAD_NORM_EPS = 1e-6
L2_EPS = 1e-6
N_BRANCH = 3
BRANCH_WIDTH = 512

LANES = 128
SUBLANES = 8
CHUNK = 64
GDN_CHUNKS_PER_WAVE = 8
RW_CHUNKS_PER_STEP = 4
GLA_CHUNKS_PER_STEP = 4
RW_PAD = 1920
GLA_PAD = 1664
GDN_PAD = 2176
MIX_PAD = RW_PAD + GLA_PAD + GDN_PAD
VMEM_LIMIT = 56 * 1024 * 1024


def _param(arr, *prefix):
    tail = arr.shape[len(prefix):]
    zeros = (0,) * len(tail)
    spec = pl.BlockSpec((None,) * len(prefix) + tail, lambda *_: prefix + zeros, pipeline_mode=pl.Buffered(1))
    return arr, spec


def _call(body, grid, sem, streamed, params, out_specs, out_shape, scratch, name):
    ops = list(streamed) + list(params)
    return pl.pallas_call(
        body,
        grid=grid,
        in_specs=[spec for _, spec in ops],
        out_specs=out_specs,
        out_shape=out_shape,
        scratch_shapes=scratch,
        compiler_params=pltpu.CompilerParams(dimension_semantics=sem, vmem_limit_bytes=VMEM_LIMIT),
        name=name,
    )(*[arr for arr, _ in ops])


def _mm(a, b):
    return jnp.dot(a.astype(BF16), b.astype(BF16), preferred_element_type=F32)


def _mm_nt(a, b):
    return lax.dot_general(a.astype(BF16), b.astype(BF16), (((1,), (1,)), ((), ())),
                           preferred_element_type=F32)


def _split_bf16(x, parts):
    out = []
    for _ in range(parts - 1):
        piece = x.astype(BF16)
        out.append(piece)
        x = x - piece.astype(F32)
    out.append(x.astype(BF16))
    return out


def _mm_f32(a, b):
    a_hi, a_lo = _split_bf16(a, 2)
    b_hi, b_lo = _split_bf16(b, 2)
    dot = functools.partial(jnp.dot, preferred_element_type=F32)
    return dot(a_hi, b_hi) + (dot(a_hi, b_lo) + dot(a_lo, b_hi))


def _chunk_cumsum(x):
    tril = _tril_f32(CHUNK).astype(BF16)
    out = []
    for r0 in range(0, x.shape[0], CHUNK):
        pieces = _split_bf16(x[r0:r0 + CHUNK], 3)
        acc = jnp.dot(tril, pieces[2], preferred_element_type=F32)
        acc = acc + jnp.dot(tril, pieces[1], preferred_element_type=F32)
        out.append(acc + jnp.dot(tril, pieces[0], preferred_element_type=F32))
    return jnp.concatenate(out, axis=0)


def _layer_norm(y, g, b):
    m = jnp.mean(y, -1, keepdims=True)
    d = y - m
    var = jnp.mean(d * d, -1, keepdims=True)
    return d * lax.rsqrt(var + LN_EPS) * g + b


def _sigmoid(x):
    return 0.5 * jnp.tanh(0.5 * x) + 0.5


def _softplus(x):
    return jnp.maximum(x, 0.0) + jnp.log(1.0 + jnp.exp(-jnp.abs(x)))


def _iota2(shape):
    return (lax.broadcasted_iota(jnp.int32, shape, 0), lax.broadcasted_iota(jnp.int32, shape, 1))


def _tril_f32(n):
    r, c = _iota2((n, n))
    return jnp.where(r >= c, 1.0, 0.0).astype(F32)


def _sibling_mask(r, c, s):
    k = s.bit_length() - 1
    return ((r >> (k + 1)) == (c >> (k + 1))) & (((r >> k) & 1) == 1) & (((c >> k) & 1) == 0)


def _unit_lower_inverses(lows, n, between=lambda: None):
    r, c = _iota2((n, n))
    eye = jnp.where(r == c, 1.0, 0.0).astype(F32)
    first = _sibling_mask(r, c, 1)
    xs = [eye - jnp.where(first, low, 0.0) for low in lows]
    s = 2
    while s < CHUNK:
        sib = _sibling_mask(r, c, s)
        if s % SUBLANES:
            ts = [_mm(jnp.where(sib, low, 0.0), x) for low, x in zip(lows, xs)]
            between()
            us = [_mm(x, t) for x, t in zip(xs, ts)]
            between()
            xs = [x - u for x, u in zip(xs, us)]
        else:
            blank = jnp.zeros((s, n), F32)

            def odd(a, s=s):
                return jnp.concatenate([a[g * s:(g + 1) * s] for g in range(1, n // s, 2)], axis=0)

            def spread(a, s=s, blank=blank):
                return jnp.concatenate([piece for g in range(n // (2 * s)) for piece in (blank, a[g * s:(g + 1) * s])],
                                       axis=0)

            ts = [_mm(odd(jnp.where(sib, low, 0.0)), x) for low, x in zip(lows, xs)]
            between()
            us = [_mm(odd(x), spread(t)) for x, t in zip(xs, ts)]
            between()
            xs = [x - spread(u) for x, u in zip(xs, us)]
        s *= 2
    return xs


def _stack_heads(z):
    lane = lax.broadcasted_iota(jnp.int32, z.shape, 1)
    first = lane < (LANES // 2)
    return jnp.concatenate([jnp.where(first, z, 0.0), jnp.where(first, 0.0, z)], axis=0)


def _block_causal(n, strict):
    r, c = _iota2((n, n))
    same = (r >> 6) == (c >> 6)
    return same & ((r > c) if strict else (r >= c))


MXU_TILE = 256
FF_CHUNK = MXU_TILE
DENSE_SUB = 512


def _sub_tiles(ref):
    return [slice(r0, r0 + DENSE_SUB) for r0 in range(0, ref.shape[0], DENSE_SUB)]


def _swiglu_ln(x, w1_ref, w3_ref, w2_ref, g, b):
    xb = x.astype(BF16)
    acc = None
    for c0 in range(0, D_FF, FF_CHUNK):
        h1 = jnp.dot(xb, w1_ref[:, c0:c0 + FF_CHUNK], preferred_element_type=F32)
        h3 = jnp.dot(xb, w3_ref[:, c0:c0 + FF_CHUNK], preferred_element_type=F32)
        act = (h1 * _sigmoid(h1) * h3).astype(BF16)
        part = jnp.dot(act, w2_ref[c0:c0 + FF_CHUNK, :], preferred_element_type=F32)
        acc = part if acc is None else acc + part
    return _layer_norm(DEEPNORM_ALPHA * x + MACARON_W * acc, g, b)


def _ffn_body(x_ref, w1_ref, w3_ref, w2_ref, g_ref, b_ref, o_ref):
    for rows in _sub_tiles(x_ref):
        o_ref[rows, :] = _swiglu_ln(x_ref[rows, :], w1_ref, w3_ref, w2_ref, g_ref[...], b_ref[...])


def _ffn_ple_body(x_ref, p_ref, w1_ref, w3_ref, w2_ref, g_ref, b_ref, wg_ref, wp_ref, g2_ref, b2_ref, o_ref):
    for rows in _sub_tiles(x_ref):
        emb = jnp.dot(p_ref[rows, :].astype(BF16), wp_ref[...], preferred_element_type=F32)
        x = _swiglu_ln(x_ref[rows, :], w1_ref, w3_ref, w2_ref, g_ref[...], b_ref[...])
        gate = _sigmoid(jnp.dot(x.astype(BF16), wg_ref[...], preferred_element_type=F32))
        o_ref[rows, :] = _layer_norm(DEEPNORM_ALPHA * x + gate * emb, g2_ref[...], b2_ref[...])


def _token_rows(x, tm):
    return x, pl.BlockSpec((tm, x.shape[1]), lambda i: (i, 0))


def _ffn_call(x, params, tm):
    n = x.shape[0]
    return _call(_ffn_body, (n // tm,), ("arbitrary",), [_token_rows(x, tm)], params,
                 pl.BlockSpec((tm, D_MODEL), lambda i: (i, 0)), jax.ShapeDtypeStruct((n, D_MODEL), F32), [], "ffn_ln")


def _ffn_ple_call(x, p, layer, params, tm):
    n = x.shape[0]
    p_rows = (p, pl.BlockSpec((None, tm, PLE_DIM), lambda i: (layer, i, 0)))
    return _call(_ffn_ple_body, (n // tm,), ("arbitrary",), [_token_rows(x, tm), p_rows], params,
                 pl.BlockSpec((tm, D_MODEL), lambda i: (i, 0)), jax.ShapeDtypeStruct((n, D_MODEL), F32), [], "ffn_ple_ln")


def _merge_body(x_ref, orw_ref, ogla_ref, ogdn_ref, wgate_ref, wbr_ref, wo_ref, g_ref, b_ref, o_ref):
    for rows in _sub_tiles(x_ref):
        x = x_ref[rows, :]
        xb = x.astype(BF16)
        merged = None
        for n, br_ref in enumerate((orw_ref, ogla_ref, ogdn_ref)):
            gate = _sigmoid(jnp.dot(xb, wgate_ref[:, n * D_MODEL:(n + 1) * D_MODEL], preferred_element_type=F32))
            term = gate * jnp.dot(br_ref[rows, :], wbr_ref[n], preferred_element_type=F32)
            merged = term if merged is None else merged + term
        mix = jnp.dot(merged.astype(BF16), wo_ref[...], preferred_element_type=F32)
        o_ref[rows, :] = _layer_norm(DEEPNORM_ALPHA * x + mix, g_ref[...], b_ref[...])


def _merge_call(x, branches, params, tm):
    n = x.shape[0]
    return _call(_merge_body, (n // tm,), ("arbitrary",), [_token_rows(t, tm) for t in (x,) + tuple(branches)], params,
                 pl.BlockSpec((tm, D_MODEL), lambda i: (i, 0)), jax.ShapeDtypeStruct((n, D_MODEL), F32), [], "merge_ln")


def _gdn_body(x_ref, w_ref, cw_ref, alog_ref, dtb_ref, ng_ref, o_ref,
              ext_ref, q_s, k_s, v_s, g_s, beta_s, gate_s, st_ref, *, tb):
    t_idx = pl.program_id(1)

    @pl.when(t_idx == 0)
    def _():
        ext_ref[0:8, :] = jnp.zeros((8, GDN_QKV), F32)
        st_ref[...] = jnp.zeros_like(st_ref)

    @pl.when(t_idx != 0)
    def _():
        ext_ref[0:8, :] = ext_ref[tb:tb + 8, :]

    hw = GDN_HEADS * GDN_DK
    wave_rows = GDN_CHUNKS_PER_WAVE * CHUNK

    def prep_units(wave):
        r0 = wave * wave_rows
        rows = slice(r0, r0 + wave_rows)
        vals = {}

        def project(c0, c1):
            def run():
                if "xb" not in vals:
                    vals["xb"] = x_ref[rows, :].astype(BF16)
                ext_ref[8 + r0:8 + r0 + wave_rows, c0:c1] = jnp.dot(vals["xb"], w_ref[:, c0:c1],
                                                                      preferred_element_type=F32)
            return run

        def gates():
            xb = vals["xb"]
            gate_s[rows, :] = jnp.dot(xb, w_ref[:, GDN_QKV:GDN_QKV + GDN_HEADS * GDN_DV], preferred_element_type=F32)
            ab = jnp.dot(xb, w_ref[:, GDN_QKV + GDN_HEADS * GDN_DV:GDN_PAD], preferred_element_type=F32)
            g_s[rows, :] = _chunk_cumsum(-jnp.exp(alog_ref[...]) * _softplus(ab + dtb_ref[...]))
            beta_s[rows, :] = _sigmoid(ab)

        def conv(grp, dst):
            def run():
                cols = slice(grp * hw, (grp + 1) * hw)
                acc = ext_ref[8 + r0:8 + r0 + wave_rows, cols] * cw_ref[GDN_CONV - 1:GDN_CONV, cols]
                for d in range(1, GDN_CONV):
                    acc = acc + ext_ref[8 + r0 - d:8 + r0 - d + wave_rows, cols] * cw_ref[GDN_CONV - 1 - d:GDN_CONV - d, cols]
                act = acc * _sigmoid(acc)
                if grp == 2:
                    dst[rows, :] = act
                else:
                    scale = GDN_DK ** -0.5 if grp == 0 else 1.0
                    for h in range(GDN_HEADS):
                        t = act[:, h * GDN_DK:(h + 1) * GDN_DK]
                        dst[rows, h * GDN_DK:(h + 1) * GDN_DK] = t * (lax.rsqrt(jnp.sum(t * t, -1, keepdims=True) + L2_EPS) * scale)
            return run

        return ([project(g * hw, (g + 1) * hw) for g in range(3)] + [gates]
                + [conv(g, dst) for g, dst in enumerate((q_s, k_s, v_s))])

    n2 = 2 * CHUNK
    ng = ng_ref[...]

    causal = _block_causal(n2, strict=False)
    strict = _block_causal(n2, strict=True)
    first = lax.broadcasted_iota(jnp.int32, (GDN_DK, n2), 1) < CHUNK
    pending = []

    def fill():
        if pending:
            pending.pop(0)()

    def pair_rows(src, rows, pr, width):
        return jnp.concatenate([src[rows, h * width:(h + 1) * width] for h in (2 * pr, 2 * pr + 1)], axis=0)

    def pair_col(a, pr, off):
        return jnp.concatenate([a[:, off + h:off + h + 1] for h in (2 * pr, 2 * pr + 1)], axis=0)

    def precompute(chunks):
        jobs = [(j, pr) for j in range(len(chunks)) for pr in range(GDN_HEADS // 2)]
        rows = [slice(c * CHUNK, (c + 1) * CHUNK) for c in chunks]
        gcs = [g_s[rw, :] for rw in rows]
        betas = [beta_s[rw, :] for rw in rows]
        gi = [pair_col(gcs[j], pr, 0) for j, pr in jobs]
        bcol = [pair_col(betas[j], pr, GDN_HEADS) for j, pr in jobs]
        qs = [pair_rows(q_s, rows[j], pr, GDN_DK) for j, pr in jobs]
        ks = [pair_rows(k_s, rows[j], pr, GDN_DK) for j, pr in jobs]
        vs = [pair_rows(v_s, rows[j], pr, GDN_DV) for j, pr in jobs]
        dec = []
        for g in gi:
            gi_b = jnp.broadcast_to(g, (n2, n2))
            diff = gi_b - gi_b.T
            dec.append(jnp.where(causal, jnp.exp(jnp.where(causal, diff, 0.0)), 0.0))
        kb = [k * b for k, b in zip(ks, bcol)]
        gram = [_mm_nt(jnp.concatenate([a, q], axis=0), k) for a, q, k in zip(kb, qs, ks)]
        fill()
        low = [jnp.where(strict, g[0:n2] * d, 0.0) for g, d in zip(gram, dec)]
        att = [g[n2:2 * n2] * d for g, d in zip(gram, dec)]
        t_inv = _unit_lower_inverses(low, n2, between=fill)
        eg = [jnp.exp(g) for g in gi]
        wu = [_mm(t, jnp.concatenate([a * e, v * b], axis=1))
              for t, a, e, v, b in zip(t_inv, kb, eg, vs, bcol)]
        fill()
        aw = [_mm(a, x) for a, x in zip(att, wu)]
        fill()
        g_end = [jnp.concatenate([jnp.broadcast_to(gcs[j][CHUNK - 1:CHUNK, h:h + 1], (CHUNK, 1))
                                  for h in (2 * pr, 2 * pr + 1)], axis=0) for j, pr in jobs]
        k_hat_t = [(k * jnp.exp(ge - g)).T for k, ge, g in zip(ks, g_end, gi)]
        kwu2 = [_mm(jnp.concatenate([jnp.where(first, kt, 0.0), jnp.where(first, 0.0, kt)], axis=0), x)
                for kt, x in zip(k_hat_t, wu)]
        fill()
        per_head = {}
        for idx, (j, pr) in enumerate(jobs):
            for half, h in enumerate((2 * pr, 2 * pr + 1)):
                sl = slice(half * CHUNK, (half + 1) * CHUNK)
                kwu = kwu2[idx][half * GDN_DK:(half + 1) * GDN_DK]
                q_eff = qs[idx][sl] * eg[idx][sl] - aw[idx][sl, 0:GDN_DV]
                g_dec = jnp.exp(gcs[j][CHUNK - 1:CHUNK, h:h + 1])
                per_head[(chunks[j], h)] = (q_eff, aw[idx][sl, GDN_DV:2 * GDN_DV], g_dec, kwu)
        return per_head

    sts = [st_ref[h] for h in range(GDN_HEADS)]

    def chain_step(c, per_head):
        def run():
            rows = slice(c * CHUNK, (c + 1) * CHUNK)
            for h in range(GDN_HEADS):
                q_eff, o_loc, g_dec, kwu = per_head[(c, h)]
                st = sts[h]
                both = _mm(jnp.concatenate([q_eff, kwu[:, 0:GDN_DV]], axis=0), st)
                o = both[0:CHUNK] + o_loc
                sts[h] = g_dec * st - both[CHUNK:CHUNK + GDN_DK] + kwu[:, GDN_DV:2 * GDN_DV]
                o = o * lax.rsqrt(jnp.mean(o * o, -1, keepdims=True) + HEAD_NORM_EPS) * ng
                gate = gate_s[rows, h * GDN_DV:(h + 1) * GDN_DV]
                o_ref[rows, h * GDN_DV:(h + 1) * GDN_DV] = (o * gate * _sigmoid(gate)).astype(o_ref.dtype)
        return run

    n_waves = tb // wave_rows
    for unit in prep_units(0):
        unit()
    for wave in range(n_waves):
        chunks = list(range(wave * GDN_CHUNKS_PER_WAVE, (wave + 1) * GDN_CHUNKS_PER_WAVE))
        if wave + 1 < n_waves:
            pending.extend(prep_units(wave + 1))
        per_head = precompute(chunks)
        while pending:
            fill()
        pending.extend(chain_step(c, per_head) for c in chunks)
    while pending:
        fill()
    for h in range(GDN_HEADS):
        st_ref[h] = sts[h]


def _mixer_call(body, x, params, width, scratch, bsz, seq, tb, name):
    nt = seq // tb
    rows = lambda b, t: (b * nt + t, 0)
    return _call(functools.partial(body, tb=tb), (bsz, nt), ("arbitrary", "arbitrary"),
                 [(x, pl.BlockSpec((tb, D_MODEL), rows))], params,
                 pl.BlockSpec((tb, width), rows), jax.ShapeDtypeStruct((bsz * seq, width), BF16), scratch, name)


def _gdn_call(x, params, bsz, seq, tb):
    wide = pltpu.VMEM((tb, GDN_HEADS * GDN_DK), F32)
    narrow = pltpu.VMEM((tb, LANES), F32)
    scratch = [pltpu.VMEM((tb + 8, GDN_QKV), F32), wide, wide, wide, narrow, narrow, wide,
               pltpu.VMEM((GDN_HEADS, GDN_DK, GDN_DV), F32)]
    return _mixer_call(_gdn_body, x, params, GDN_HEADS * GDN_DV, scratch, bsz, seq, tb, "gated_deltanet")


def _gla_level_arg(s, la, b, ri):
    c = CHUNK
    if s >= 4:
        ref = jnp.concatenate(
            [jnp.broadcast_to(b[g * 2 * s + s - 1:g * 2 * s + s, :], (2 * s, LANES)) for g in range(c // (2 * s))],
            axis=0)
        return -jnp.abs(b - ref)
    if s == 2:
        m = ri & 3
        up = pltpu.roll(la, c - 1, 0)
        dn = pltpu.roll(la, 1, 0)
        return jnp.where(m == 0, up, jnp.where(m == 1, 0.0, jnp.where(m == 2, la, la + dn)))
    return jnp.where((ri & 1) == 1, la, 0.0)


def _gla_scores(qs, ks, las, bs, between):
    c = CHUNK
    ri = lax.broadcasted_iota(jnp.int32, (c, LANES), 0)
    r2, c2 = _iota2((2 * c, c))
    r2 = r2 & (c - 1)
    diag = r2 == c2
    scores = [jnp.where(diag, _mm_nt(_stack_heads(q), k), 0.0) for q, k in zip(qs, ks)]
    between()
    s = c // 2
    while s >= 1:
        row_side = ((ri >> (s.bit_length() - 1)) & 1) == 1
        sib = _sibling_mask(r2, c2, s)
        es = [jnp.exp(_gla_level_arg(s, la, b, ri)) for la, b in zip(las, bs)]
        part = [_mm_nt(_stack_heads(jnp.where(row_side, q * e, 0.0)), jnp.where(row_side, 0.0, k * e))
                for q, k, e in zip(qs, ks, es)]
        between()
        scores = [sc + jnp.where(sib, p, 0.0) for sc, p in zip(scores, part)]
        s //= 2
    return scores


def _gla_body(x_ref, w_ref, w2_ref, gkb_ref, ng_ref, o_ref, h_s, la_s, b_s, st_ref, *, tb):
    t_idx = pl.program_id(1)

    @pl.when(t_idx == 0)
    def _():
        st_ref[...] = jnp.zeros_like(st_ref)

    hk = GLA_HEADS * GLA_DK
    hv = GLA_HEADS * GLA_DV
    ng = ng_ref[...]
    cpi = GLA_CHUNKS_PER_STEP
    sb = cpi * CHUNK
    npair = GLA_HEADS // 2
    jobs = [(j, p) for j in range(cpi) for p in range(npair)]
    row_first = lax.broadcasted_iota(jnp.int32, (LANES, GLA_DV), 0) < GLA_DK

    def prep_units(s):
        rows = slice(s * sb, (s + 1) * sb)
        vals = {}

        def proj(c0, c1):
            def run():
                if "xb" not in vals:
                    vals["xb"] = x_ref[rows, :].astype(BF16)
                h_s[rows, c0:c1] = jnp.dot(vals["xb"], w_ref[:, c0:c1], preferred_element_type=F32)
            return run

        def decay():
            gk_lo = jnp.dot(vals["xb"], w_ref[:, 2 * hk + 2 * hv:GLA_PAD], preferred_element_type=F32)
            z = _mm_f32(gk_lo, w2_ref[...]) + gkb_ref[...]
            la = (jnp.minimum(z, 0.0) - jnp.log(1.0 + jnp.exp(-jnp.abs(z)))) * (1.0 / GLA_GATE_NORM)
            la_s[rows, :] = la
            b_s[rows, :] = _chunk_cumsum(la)

        return [proj(c0, c0 + 2 * LANES) for c0 in range(0, 2 * hk + 2 * hv, 2 * LANES)] + [decay]

    def recurrence(s, pending):
        rows = [slice(s * sb + j * CHUNK, s * sb + (j + 1) * CHUNK) for j in range(cpi)]

        def fill():
            if pending:
                pending.pop(0)()
        qs = [h_s[rows[j], p * LANES:(p + 1) * LANES] * (GLA_DK ** -0.5) for j, p in jobs]
        ks = [h_s[rows[j], hk + p * LANES:hk + (p + 1) * LANES] for j, p in jobs]
        las = [la_s[rows[j], p * LANES:(p + 1) * LANES] for j, p in jobs]
        bs = [b_s[rows[j], p * LANES:(p + 1) * LANES] for j, p in jobs]
        vpair = [h_s[rows[j], 2 * hk + 2 * p * GLA_DV:2 * hk + (2 * p + 2) * GLA_DV] for j, p in jobs]
        scores = _gla_scores(qs, ks, las, bs, fill)
        q_in = [_stack_heads(q * jnp.exp(b)) for q, b in zip(qs, bs)]
        bts = [b.T for b in bs]
        b_end = [bt[:, CHUNK - 1:CHUNK] for bt in bts]
        upd = [_mm(k.T * jnp.exp(be - bt), vp) for k, be, bt, vp in zip(ks, b_end, bts, vpair)]
        intra = [[_mm(sc[half * CHUNK:(half + 1) * CHUNK], vp[:, half * GLA_DV:(half + 1) * GLA_DV])
                  for half in range(2)] for sc, vp in zip(scores, vpair)]
        sts = [st_ref[p] for p in range(npair)]
        for idx, (j, p) in enumerate(jobs):
            st = sts[p]
            inter = _mm(q_in[idx], st)
            sts[p] = st * jnp.exp(b_end[idx]) + jnp.where(row_first, upd[idx][:, 0:GLA_DV], upd[idx][:, GLA_DV:2 * GLA_DV])
            for half in range(2):
                h = 2 * p + half
                o = intra[idx][half] + inter[half * CHUNK:(half + 1) * CHUNK]
                o = o * lax.rsqrt(jnp.mean(o * o, -1, keepdims=True) + HEAD_NORM_EPS) * ng
                gate = h_s[rows[j], 2 * hk + hv + h * GLA_DV:2 * hk + hv + (h + 1) * GLA_DV]
                o_ref[rows[j], h * GLA_DV:(h + 1) * GLA_DV] = (o * gate * _sigmoid(gate)).astype(o_ref.dtype)
        for p in range(npair):
            st_ref[p] = sts[p]
        while pending:
            fill()

    for unit in prep_units(0):
        unit()
    for s in range(tb // sb):
        recurrence(s, prep_units(s + 1) if (s + 1) * sb < tb else [])


def _gla_call(x, params, bsz, seq, tb):
    keys = pltpu.VMEM((tb, GLA_HEADS * GLA_DK), F32)
    scratch = [pltpu.VMEM((tb, GLA_PAD - LANES), F32), keys, keys,
               pltpu.VMEM((GLA_HEADS // 2, 2 * GLA_DK, GLA_DV), F32)]
    return _mixer_call(_gla_body, x, params, GLA_HEADS * GLA_DV, scratch, bsz, seq, tb, "gated_linear_attention")


def _seg_sum_lanes(x):
    first = lax.broadcasted_iota(jnp.int32, x.shape, 1) < RW_HEAD
    lo = jnp.sum(jnp.where(first, x, 0.0), -1, keepdims=True)
    hi = jnp.sum(jnp.where(first, 0.0, x), -1, keepdims=True)
    return jnp.where(first, lo, hi)


def _rw_body(x_ref, w_ref, mu_ref, w0_ref, w2_ref, a0_ref, a2_ref, g2_ref, kk_ref, ka_ref, rk_ref, gng_ref, gnb_ref,
             o_ref, ext_ref, r_s, k_s, v_s, ld_s, ci_s, an_s, bn_s, g_s, st_ref, *, tb):
    t_idx = pl.program_id(1)

    @pl.when(t_idx == 0)
    def _():
        ext_ref[0:8, :] = jnp.zeros((8, RW_PAD), F32)
        st_ref[...] = jnp.zeros_like(st_ref)

    @pl.when(t_idx != 0)
    def _():
        ext_ref[0:8, :] = ext_ref[tb:tb + 8, :]

    w = RW_WIDTH
    n2 = 2 * CHUNK
    cpi = RW_CHUNKS_PER_STEP
    sb = cpi * CHUNK
    npair = RW_HEADS // 2
    jobs = [(j, p) for j in range(cpi) for p in range(npair)]
    strict = _block_causal(n2, strict=True)
    incl = _block_causal(n2, strict=False)
    zero = jnp.zeros((n2, LANES), F32)

    ext_ref[8:tb + 8, :] = jnp.dot(x_ref[...].astype(BF16), w_ref[...], preferred_element_type=F32)

    def shifted(cols):
        cur = ext_ref[8:tb + 8, cols]
        return cur + (ext_ref[7:tb + 7, cols] - cur) * mu_ref[:, cols]

    lo = shifted(slice(3 * w, 3 * w + LANES))
    wv = -_softplus(-(w0_ref[...] + _mm_f32(jnp.tanh(lo), w2_ref[...]))) - 0.5
    ld = -jnp.exp(wv)
    ld_s[...] = ld
    ci_s[...] = _chunk_cumsum(ld)
    a = _sigmoid(a0_ref[...] + _mm(lo, a2_ref[...]))
    glo = shifted(slice(3 * w + LANES, RW_PAD))
    g_s[...] = _mm(_sigmoid(glo), g2_ref[...])
    r_s[...] = shifted(slice(0, w))
    v_s[...] = shifted(slice(2 * w, 3 * w))
    for p in range(npair):
        lanes = slice(p * LANES, (p + 1) * LANES)
        k = shifted(slice(w + p * LANES, w + (p + 1) * LANES))
        t = k * kk_ref[:, lanes]
        kkn = t * lax.rsqrt(_seg_sum_lanes(t * t) + L2_EPS)
        an_s[:, lanes] = -kkn
        bn_s[:, lanes] = kkn * a[:, lanes]
        k_s[:, lanes] = k * (1.0 + (a[:, lanes] - 1.0) * ka_ref[:, lanes])

    def step(it, carry):
        rows = [pl.ds(pl.multiple_of(it * sb + j * CHUNK, CHUNK), CHUNK) for j in range(cpi)]

        def load(src):
            return [src[rows[j], p * LANES:(p + 1) * LANES] for j, p in jobs]

        r, kx, v, ld, ci, an, bn = (load(src) for src in (r_s, k_s, v_s, ld_s, ci_s, an_s, bn_s))
        e_in = [jnp.exp(-c) for c in ci]
        e_out = [jnp.exp(c[CHUNK - 1:CHUNK, :] - c) for c in ci]
        a_s = [_stack_heads(a * jnp.exp(c - t)) for a, c, t in zip(an, ci, ld)]
        r_t = [_stack_heads(x * jnp.exp(c)) for x, c in zip(r, ci)]
        b_hat = [_stack_heads(b * e) for b, e in zip(bn, e_out)]
        k_hat = [_stack_heads(k * e) for k, e in zip(kx, e_out)]
        v_st = [_stack_heads(x) for x in v]
        gram = [_mm_nt(jnp.concatenate([a, x], axis=0),
                       jnp.concatenate([_stack_heads(b * e), _stack_heads(k * e)], axis=0))
                for a, x, b, k, e in zip(a_s, r_t, bn, kx, e_in)]
        a_ab = [jnp.where(strict, g[0:n2, 0:n2], 0.0) for g in gram]
        a_ak = [jnp.where(strict, g[0:n2, n2:2 * n2], 0.0) for g in gram]
        a_rbk = [jnp.concatenate([jnp.where(incl, g[n2:2 * n2, 0:n2], 0.0),
                                  jnp.where(incl, g[n2:2 * n2, n2:2 * n2], 0.0)], axis=1) for g in gram]
        t_inv = _unit_lower_inverses([-a for a in a_ab], n2)
        akv = [_mm(a, x) for a, x in zip(a_ak, v_st)]
        x_au = [_mm(t, jnp.concatenate([a, b], axis=1)) for t, a, b in zip(t_inv, a_s, akv)]
        bk_t = [jnp.concatenate([b.T, k.T], axis=1) for b, k in zip(b_hat, k_hat)]
        zz = [_mm(jnp.concatenate([a, t], axis=0), jnp.concatenate([x, jnp.concatenate([zero, u], axis=1)], axis=0))
              for a, t, x, u in zip(a_rbk, bk_t, x_au, v_st)]
        rw_p = [jnp.concatenate([x + y[0:n2, 0:LANES], y[n2:2 * n2, 0:LANES]], axis=0) for x, y in zip(r_t, zz)]
        g_col = [jnp.exp(c.T[:, CHUNK - 1:CHUNK]) for c in ci]
        sts = [st_ref[p] for p in range(npair)]
        ys = []
        for idx, (j, p) in enumerate(jobs):
            st = sts[p]
            both = _mm(rw_p[idx], st)
            ys.append(both[0:n2] + zz[idx][0:n2, LANES:2 * LANES])
            sts[p] = g_col[idx] * st + both[n2:2 * n2] + zz[idx][n2:2 * n2, LANES:2 * LANES]
        for p in range(npair):
            st_ref[p] = sts[p]
        lanes = [slice(p * LANES, (p + 1) * LANES) for _, p in jobs]
        y = [t[0:CHUNK] + t[CHUNK:n2] for t in ys]
        sums = [_seg_sum_lanes(jnp.concatenate([a, b * k * rk_ref[:, ln]], axis=0))
                for a, b, k, ln in zip(y, r, kx, lanes)]
        d = [a - t[0:CHUNK] * (1.0 / RW_HEAD) for a, t in zip(y, sums)]
        var = [_seg_sum_lanes(a * a) * (1.0 / RW_HEAD) for a in d]
        for idx, (j, p) in enumerate(jobs):
            yn = d[idx] * lax.rsqrt(var[idx] + RW_GN_EPS) * gng_ref[:, lanes[idx]] + gnb_ref[:, lanes[idx]]
            bonus = sums[idx][CHUNK:n2] * v[idx]
            o_ref[rows[j], lanes[idx]] = ((yn + bonus) * g_s[rows[j], lanes[idx]]).astype(o_ref.dtype)
        return carry

    lax.fori_loop(0, tb // sb, step, 0)


def _rw_call(x, params, bsz, seq, tb):
    scratch = ([pltpu.VMEM((tb + 8, RW_PAD), F32)] + [pltpu.VMEM((tb, RW_WIDTH), F32)] * 8
               + [pltpu.VMEM((RW_HEADS // 2, LANES, LANES), F32)])
    return _mixer_call(_rw_body, x, params, RW_WIDTH, scratch, bsz, seq, tb, "rwkv7_time_mix")


def _pad_to(a, size, axis):
    pad = [(0, 0)] * a.ndim
    pad[axis] = (0, size - a.shape[axis])
    return jnp.pad(a, pad)


def _mixer_in_weights(w_in):
    o = 0
    rw = w_in[..., o:o + RW_IN]
    o += RW_IN
    gla = w_in[..., o:o + GLA_IN]
    o += GLA_IN
    gdn = w_in[..., o:o + GDN_IN]
    o += GDN_IN
    gate = w_in[..., o:]
    qkv_w = 2 * GLA_HEADS * GLA_DK + GLA_HEADS * GLA_DV
    gla_p = jnp.concatenate([gla[..., 0:qkv_w], gla[..., qkv_w + GLA_GATE_LORA:], gla[..., qkv_w:qkv_w + GLA_GATE_LORA]], axis=-1)
    gdn_p = jnp.concatenate([gdn[..., 0:GDN_QKV], gdn[..., GDN_QKV + 2 * GDN_HEADS:], gdn[..., GDN_QKV:GDN_QKV + 2 * GDN_HEADS]], axis=-1)
    return (_pad_to(rw, RW_PAD, 2).astype(BF16), _pad_to(gla_p, GLA_PAD, 2).astype(BF16),
            _pad_to(gdn_p, GDN_PAD, 2).astype(BF16), gate.astype(BF16))


def _rows(v):
    return v.reshape(v.shape[:-1] + (1, v.shape[-1]))


def _tiles(n, seq):
    return min(2 * DENSE_SUB, n), min(512, seq), min(1024, seq)


def kernel(x, p, ln_g, ln_b, ffn_w1, ffn_w3, ffn_w2, w_in, rw_mu, rw_w0, rw_w2, rw_a0, rw_a2, rw_g2, rw_k_k, rw_k_a, rw_r_k, rw_gn_g, rw_gn_b, gla_gk_w2, gla_gk_b, gla_norm_g, gdn_conv_w, gdn_a_log, gdn_dt_bias, gdn_norm_g, w_branch, w_o, ple_w_gate, ple_w_proj):
    bsz, seq, d = x.shape
    depth = p.shape[0]
    n = bsz * seq
    tm, tb, tb_gla = _tiles(n, seq)
    x = x.reshape(n, d)
    p = p.reshape(depth, n, PLE_DIM)

    w1, w3, w2 = ffn_w1.astype(BF16), ffn_w3.astype(BF16), ffn_w2.astype(BF16)
    w_rw, w_gla, w_gdn, w_gate = _mixer_in_weights(w_in)
    w_br, w_out = w_branch.astype(BF16), w_o.astype(BF16)
    w_pg, w_pp = ple_w_gate.astype(BF16), ple_w_proj.astype(BF16)
    lng, lnb = _rows(ln_g), _rows(ln_b)
    rw_vecs = [_rows(v) for v in (rw_k_k, rw_k_a, rw_r_k.reshape(depth, RW_WIDTH), rw_gn_g, rw_gn_b)]
    rw_mu_p = _rows(_pad_to(rw_mu, RW_PAD, 1))
    rw_w0_r, rw_a0_r = _rows(rw_w0), _rows(rw_a0)
    rw_w2_p = _pad_to(rw_w2, LANES, 1)
    rw_a2_p = jnp.pad(rw_a2, ((0, 0), (RW_DECAY_LORA, 0), (0, 0)))
    rw_g2_p = _pad_to(rw_g2, 2 * LANES, 1)
    gla_w2_p, gla_b_r, gla_ng = _pad_to(gla_gk_w2, LANES, 1), _rows(gla_gk_b), _rows(gla_norm_g)
    gdn_alog, gdn_dtb, gdn_ng = _rows(_pad_to(gdn_a_log, LANES, 1)), _rows(_pad_to(gdn_dt_bias, LANES, 1)), _rows(gdn_norm_g)

    for i in range(depth):
        x = _ffn_call(x, [_param(w1, i, 0), _param(w3, i, 0), _param(w2, i, 0), _param(lng, i, 0), _param(lnb, i, 0)], tm)
        o_rw = _rw_call(x, [_param(w_rw, i), _param(rw_mu_p, i), _param(rw_w0_r, i), _param(rw_w2_p, i),
                            _param(rw_a0_r, i), _param(rw_a2_p, i), _param(rw_g2_p, i)]
                        + [_param(v, i) for v in rw_vecs], bsz, seq, tb)
        o_gla = _gla_call(x, [_param(w_gla, i), _param(gla_w2_p, i), _param(gla_b_r, i), _param(gla_ng, i)], bsz, seq, tb_gla)
        o_gdn = _gdn_call(x, [_param(w_gdn, i), _param(gdn_conv_w, i), _param(gdn_alog, i), _param(gdn_dtb, i),
                              _param(gdn_ng, i)], bsz, seq, tb_gla)
        x = _merge_call(x, (o_rw, o_gla, o_gdn), [_param(w_gate, i), _param(w_br, i), _param(w_out, i),
                                                  _param(lng, i, 1), _param(lnb, i, 1)], tm)
        x = _ffn_ple_call(x, p, i, [_param(w1, i, 1), _param(w3, i, 1), _param(w2, i, 1), _param(lng, i, 2), _param(lnb, i, 2),
                                    _param(w_pg, i), _param(w_pp, i), _param(lng, i, 3), _param(lnb, i, 3)], tm)
    return x.reshape(bsz, seq, d)
```

```python
import functools

import jax
import jax.numpy as jnp
from jax import lax
from jax.experimental import pallas as pl
from jax.experimental.pallas import tpu as pltpu

F32 = jnp.float32
BF16 = jnp.bfloat16

D_MODEL = 1024
DEPTH = 2
PLE_DIM = 256
D_FF = 2816
LN_EPS = 1e-5
DEEPNORM_ALPHA = (2 * DEPTH) ** 0.25
MACARON_W = 0.5

RW_HEADS = 8
RW_HEAD = 64
RW_WIDTH = RW_HEADS * RW_HEAD
RW_DECAY_LORA = 64
RW_AAA_LORA = 64
RW_GATE_LORA = 160
RW_GN_EPS = 1e-5 * RW_HEAD
RW_IN = 3 * RW_WIDTH + RW_DECAY_LORA + RW_AAA_LORA + RW_GATE_LORA

GLA_HEADS = 4
GLA_DK = 64
GLA_DV = 128
GLA_GATE_LORA = 16
GLA_GATE_NORM = 16.0
GLA_IN = 2 * GLA_HEADS * GLA_DK + 2 * GLA_HEADS * GLA_DV + GLA_GATE_LORA

GDN_HEADS = 4
GDN_DK = 128
GDN_DV = 128
GDN_CONV = 4
GDN_QKV = GDN_HEADS * (2 * GDN_DK + GDN_DV)
GDN_IN = GDN_QKV + 2 * GDN_HEADS + GDN_HEADS * GDN_DV

HEAD_NORM_EPS = 1e-6
L2_EPS = 1e-6
N_BRANCH = 3
BRANCH_WIDTH = 512

LANES = 128
SUBLANES = 8
CHUNK = 64
GDN_CHUNKS_PER_WAVE = 8
RW_CHUNKS_PER_STEP = 4
GLA_CHUNKS_PER_STEP = 4
RW_PAD = 1920
GLA_PAD = 1664
GDN_PAD = 2176
MIX_PAD = RW_PAD + GLA_PAD + GDN_PAD
VMEM_LIMIT = 56 * 1024 * 1024


def _param(arr, *prefix):
    tail = arr.shape[len(prefix):]
    zeros = (0,) * len(tail)
    spec = pl.BlockSpec((None,) * len(prefix) + tail, lambda *_: prefix + zeros, pipeline_mode=pl.Buffered(1))
    return arr, spec


def _call(body, grid, sem, streamed, params, out_specs, out_shape, scratch, name):
    ops = list(streamed) + list(params)
    return pl.pallas_call(
        body,
        grid=grid,
        in_specs=[spec for _, spec in ops],
        out_specs=out_specs,
        out_shape=out_shape,
        scratch_shapes=scratch,
        compiler_params=pltpu.CompilerParams(dimension_semantics=sem, vmem_limit_bytes=VMEM_LIMIT),
        name=name,
    )(*[arr for arr, _ in ops])


def _mm(a, b):
    return jnp.dot(a.astype(BF16), b.astype(BF16), preferred_element_type=F32)


def _mm_nt(a, b):
    return lax.dot_general(a.astype(BF16), b.astype(BF16), (((1,), (1,)), ((), ())),
                           preferred_element_type=F32)


def _split_bf16(x, parts):
    out = []
    for _ in range(parts - 1):
        piece = x.astype(BF16)
        out.append(piece)
        x = x - piece.astype(F32)
    out.append(x.astype(BF16))
    return out


def _mm_f32(a, b):
    a_hi, a_lo = _split_bf16(a, 2)
    b_hi, b_lo = _split_bf16(b, 2)
    dot = functools.partial(jnp.dot, preferred_element_type=F32)
    return dot(a_hi, b_hi) + (dot(a_hi, b_lo) + dot(a_lo, b_hi))


def _chunk_cumsum(x):
    tril = _tril_f32(CHUNK).astype(BF16)
    out = []
    for r0 in range(0, x.shape[0], CHUNK):
        pieces = _split_bf16(x[r0:r0 + CHUNK], 3)
        acc = jnp.dot(tril, pieces[2], preferred_element_type=F32)
        acc = acc + jnp.dot(tril, pieces[1], preferred_element_type=F32)
        out.append(acc + jnp.dot(tril, pieces[0], preferred_element_type=F32))
    return jnp.concatenate(out, axis=0)


def _layer_norm(y, g, b):
    m = jnp.mean(y, -1, keepdims=True)
    d = y - m
    var = jnp.mean(d * d, -1, keepdims=True)
    return d * lax.rsqrt(var + LN_EPS) * g + b


def _sigmoid(x):
    return 0.5 * jnp.tanh(0.5 * x) + 0.5


def _softplus(x):
    return jnp.maximum(x, 0.0) + jnp.log(1.0 + jnp.exp(-jnp.abs(x)))


def _iota2(shape):
    return (lax.broadcasted_iota(jnp.int32, shape, 0), lax.broadcasted_iota(jnp.int32, shape, 1))


def _tril_f32(n):
    r, c = _iota2((n, n))
    return jnp.where(r >= c, 1.0, 0.0).astype(F32)


def _sibling_mask(r, c, s):
    k = s.bit_length() - 1
    return ((r >> (k + 1)) == (c >> (k + 1))) & (((r >> k) & 1) == 1) & (((c >> k) & 1) == 0)


def _unit_lower_inverses(lows, n, between=lambda: None):
    r, c = _iota2((n, n))
    eye = jnp.where(r == c, 1.0, 0.0).astype(F32)
    first = _sibling_mask(r, c, 1)
    xs = [eye - jnp.where(first, low, 0.0) for low in lows]
    s = 2
    while s < CHUNK:
        sib = _sibling_mask(r, c, s)
        if s % SUBLANES:
            ts = [_mm(jnp.where(sib, low, 0.0), x) for low, x in zip(lows, xs)]
            between()
            us = [_mm(x, t) for x, t in zip(xs, ts)]
            between()
            xs = [x - u for x, u in zip(xs, us)]
        else:
            blank = jnp.zeros((s, n), F32)

            def odd(a, s=s):
                return jnp.concatenate([a[g * s:(g + 1) * s] for g in range(1, n // s, 2)], axis=0)

            def spread(a, s=s, blank=blank):
                return jnp.concatenate([piece for g in range(n // (2 * s)) for piece in (blank, a[g * s:(g + 1) * s])],
                                       axis=0)

            ts = [_mm(odd(jnp.where(sib, low, 0.0)), x) for low, x in zip(lows, xs)]
            between()
            us = [_mm(odd(x), spread(t)) for x, t in zip(xs, ts)]
            between()
            xs = [x - spread(u) for x, u in zip(xs, us)]
        s *= 2
    return xs


def _stack_heads(z):
    lane = lax.broadcasted_iota(jnp.int32, z.shape, 1)
    first = lane < (LANES // 2)
    return jnp.concatenate([jnp.where(first, z, 0.0), jnp.where(first, 0.0, z)], axis=0)


def _block_causal(n, strict):
    r, c = _iota2((n, n))
    same = (r >> 6) == (c >> 6)
    return same & ((r > c) if strict else (r >= c))


MXU_TILE = 256
FF_CHUNK = MXU_TILE
DENSE_SUB = 512


def _sub_tiles(ref):
    return [slice(r0, r0 + DENSE_SUB) for r0 in range(0, ref.shape[0], DENSE_SUB)]


def _swiglu_ln(x, w1_ref, w3_ref, w2_ref, g, b):
    xb = x.astype(BF16)
    acc = None
    for c0 in range(0, D_FF, FF_CHUNK):
        h1 = jnp.dot(xb, w1_ref[:, c0:c0 + FF_CHUNK], preferred_element_type=F32)
        h3 = jnp.dot(xb, w3_ref[:, c0:c0 + FF_CHUNK], preferred_element_type=F32)
        act = (h1 * _sigmoid(h1) * h3).astype(BF16)
        part = jnp.dot(act, w2_ref[c0:c0 + FF_CHUNK, :], preferred_element_type=F32)
        acc = part if acc is None else acc + part
    return _layer_norm(DEEPNORM_ALPHA * x + MACARON_W * acc, g, b)


def _ffn_body(x_ref, w1_ref, w3_ref, w2_ref, g_ref, b_ref, o_ref):
    for rows in _sub_tiles(x_ref):
        o_ref[rows, :] = _swiglu_ln(x_ref[rows, :], w1_ref, w3_ref, w2_ref, g_ref[...], b_ref[...])


def _ffn_ple_body(x_ref, p_ref, w1_ref, w3_ref, w2_ref, g_ref, b_ref, wg_ref, wp_ref, g2_ref, b2_ref, o_ref):
    tiles = _sub_tiles(x_ref)
    embs = [jnp.dot(p_ref[rows, :].astype(BF16), wp_ref[...], preferred_element_type=F32) for rows in tiles]
    xs = [_swiglu_ln(x_ref[rows, :], w1_ref, w3_ref, w2_ref, g_ref[...], b_ref[...]) for rows in tiles]
    gates = [_sigmoid(jnp.dot(x.astype(BF16), wg_ref[...], preferred_element_type=F32)) for x in xs]
    for rows, x, gate, emb in zip(tiles, xs, gates, embs):
        o_ref[rows, :] = _layer_norm(DEEPNORM_ALPHA * x + gate * emb, g2_ref[...], b2_ref[...])


def _token_rows(x, tm):
    return x, pl.BlockSpec((tm, x.shape[1]), lambda i: (i, 0))


def _ffn_call(x, params, tm):
    n = x.shape[0]
    return _call(_ffn_body, (n // tm,), ("arbitrary",), [_token_rows(x, tm)], params,
                 pl.BlockSpec((tm, D_MODEL), lambda i: (i, 0)), jax.ShapeDtypeStruct((n, D_MODEL), F32), [], "ffn_ln")


def _ffn_ple_call(x, p, layer, params, tm):
    n = x.shape[0]
    p_rows = (p, pl.BlockSpec((None, tm, PLE_DIM), lambda i: (layer, i, 0)))
    return _call(_ffn_ple_body, (n // tm,), ("arbitrary",), [_token_rows(x, tm), p_rows], params,
                 pl.BlockSpec((tm, D_MODEL), lambda i: (i, 0)), jax.ShapeDtypeStruct((n, D_MODEL), F32), [], "ffn_ple_ln")


def _merge_body(x_ref, orw_ref, ogla_ref, ogdn_ref, wgate_ref, wbr_ref, wo_ref, g_ref, b_ref, o_ref):
    tiles = _sub_tiles(x_ref)
    xs = [x_ref[rows, :] for rows in tiles]
    mergeds = []
    for rows, x in zip(tiles, xs):
        xb = x.astype(BF16)
        merged = None
        for n, br_ref in enumerate((orw_ref, ogla_ref, ogdn_ref)):
            gate = _sigmoid(jnp.dot(xb, wgate_ref[:, n * D_MODEL:(n + 1) * D_MODEL], preferred_element_type=F32))
            term = gate * jnp.dot(br_ref[rows, :], wbr_ref[n], preferred_element_type=F32)
            merged = term if merged is None else merged + term
        mergeds.append(merged.astype(BF16))
    mixes = [jnp.dot(m, wo_ref[...], preferred_element_type=F32) for m in mergeds]
    for rows, x, mix in zip(tiles, xs, mixes):
        o_ref[rows, :] = _layer_norm(DEEPNORM_ALPHA * x + mix, g_ref[...], b_ref[...])


def _merge_call(x, branches, params, tm):
    n = x.shape[0]
    return _call(_merge_body, (n // tm,), ("arbitrary",), [_token_rows(t, tm) for t in (x,) + tuple(branches)], params,
                 pl.BlockSpec((tm, D_MODEL), lambda i: (i, 0)), jax.ShapeDtypeStruct((n, D_MODEL), F32), [], "merge_ln")


def _gdn_body(x_ref, w_ref, cw_ref, alog_ref, dtb_ref, ng_ref, o_ref,
              ext_ref, q_s, k_s, v_s, g_s, beta_s, gate_s, st_ref, *, tb):
    t_idx = pl.program_id(1)

    @pl.when(t_idx == 0)
    def _():
        ext_ref[0:8, :] = jnp.zeros((8, GDN_QKV), F32)
        st_ref[...] = jnp.zeros_like(st_ref)

    @pl.when(t_idx != 0)
    def _():
        ext_ref[0:8, :] = ext_ref[tb:tb + 8, :]

    hw = GDN_HEADS * GDN_DK
    wave_rows = GDN_CHUNKS_PER_WAVE * CHUNK

    def prep_units(wave):
        r0 = wave * wave_rows
        rows = slice(r0, r0 + wave_rows)
        vals = {}

        def project(c0, c1):
            def run():
                if "xb" not in vals:
                    vals["xb"] = x_ref[rows, :].astype(BF16)
                ext_ref[8 + r0:8 + r0 + wave_rows, c0:c1] = jnp.dot(vals["xb"], w_ref[:, c0:c1],
                                                                      preferred_element_type=F32)
            return run

        def gates():
            xb = vals["xb"]
            gate_s[rows, :] = jnp.dot(xb, w_ref[:, GDN_QKV:GDN_QKV + GDN_HEADS * GDN_DV], preferred_element_type=F32)
            ab = jnp.dot(xb, w_ref[:, GDN_QKV + GDN_HEADS * GDN_DV:GDN_PAD], preferred_element_type=F32)
            g_s[rows, :] = _chunk_cumsum(-jnp.exp(alog_ref[...]) * _softplus(ab + dtb_ref[...]))
            beta_s[rows, :] = _sigmoid(ab)

        def conv(grp, dst):
            def run():
                cols = slice(grp * hw, (grp + 1) * hw)
                acc = ext_ref[8 + r0:8 + r0 + wave_rows, cols] * cw_ref[GDN_CONV - 1:GDN_CONV, cols]
                for d in range(1, GDN_CONV):
                    acc = acc + ext_ref[8 + r0 - d:8 + r0 - d + wave_rows, cols] * cw_ref[GDN_CONV - 1 - d:GDN_CONV - d, cols]
                act = acc * _sigmoid(acc)
                if grp == 2:
                    dst[rows, :] = act
                else:
                    scale = GDN_DK ** -0.5 if grp == 0 else 1.0
                    for h in range(GDN_HEADS):
                        t = act[:, h * GDN_DK:(h + 1) * GDN_DK]
                        dst[rows, h * GDN_DK:(h + 1) * GDN_DK] = t * (lax.rsqrt(jnp.sum(t * t, -1, keepdims=True) + L2_EPS) * scale)
            return run

        return ([project(g * hw, (g + 1) * hw) for g in range(3)] + [gates]
                + [conv(g, dst) for g, dst in enumerate((q_s, k_s, v_s))])

    n2 = 2 * CHUNK
    ng = ng_ref[...]

    causal = _block_causal(n2, strict=False)
    strict = _block_causal(n2, strict=True)
    first = lax.broadcasted_iota(jnp.int32, (GDN_DK, n2), 1) < CHUNK
    pending = []

    def fill():
        if pending:
            pending.pop(0)()

    def pair_rows(src, rows, pr, width):
        return jnp.concatenate([src[rows, h * width:(h + 1) * width] for h in (2 * pr, 2 * pr + 1)], axis=0)

    def pair_col(a, pr, off):
        return jnp.concatenate([a[:, off + h:off + h + 1] for h in (2 * pr, 2 * pr + 1)], axis=0)

    def precompute(chunks):
        jobs = [(j, pr) for j in range(len(chunks)) for pr in range(GDN_HEADS // 2)]
        rows = [slice(c * CHUNK, (c + 1) * CHUNK) for c in chunks]
        gcs = [g_s[rw, :] for rw in rows]
        betas = [beta_s[rw, :] for rw in rows]
        gi = [pair_col(gcs[j], pr, 0) for j, pr in jobs]
        bcol = [pair_col(betas[j], pr, GDN_HEADS) for j, pr in jobs]
        qs = [pair_rows(q_s, rows[j], pr, GDN_DK) for j, pr in jobs]
        ks = [pair_rows(k_s, rows[j], pr, GDN_DK) for j, pr in jobs]
        vs = [pair_rows(v_s, rows[j], pr, GDN_DV) for j, pr in jobs]
        dec = []
        for g in gi:
            gi_b = jnp.broadcast_to(g, (n2, n2))
            diff = gi_b - gi_b.T
            dec.append(jnp.where(causal, jnp.exp(jnp.where(causal, diff, 0.0)), 0.0))
        kb = [k * b for k, b in zip(ks, bcol)]
        gram = [_mm_nt(jnp.concatenate([a, q], axis=0), k) for a, q, k in zip(kb, qs, ks)]
        fill()
        low = [jnp.where(strict, g[0:n2] * d, 0.0) for g, d in zip(gram, dec)]
        att = [g[n2:2 * n2] * d for g, d in zip(gram, dec)]
        t_inv = _unit_lower_inverses(low, n2, between=fill)
        eg = [jnp.exp(g) for g in gi]
        wu = [_mm(t, jnp.concatenate([a * e, v * b], axis=1))
              for t, a, e, v, b in zip(t_inv, kb, eg, vs, bcol)]
        fill()
        aw = [_mm(a, x) for a, x in zip(att, wu)]
        fill()
        g_end = [jnp.concatenate([jnp.broadcast_to(gcs[j][CHUNK - 1:CHUNK, h:h + 1], (CHUNK, 1))
                                  for h in (2 * pr, 2 * pr + 1)], axis=0) for j, pr in jobs]
        k_hat_t = [(k * jnp.exp(ge - g)).T for k, ge, g in zip(ks, g_end, gi)]
        kwu2 = [_mm(jnp.concatenate([jnp.where(first, kt, 0.0), jnp.where(first, 0.0, kt)], axis=0), x)
                for kt, x in zip(k_hat_t, wu)]
        fill()
        per_head = {}
        for idx, (j, pr) in enumerate(jobs):
            for half, h in enumerate((2 * pr, 2 * pr + 1)):
                sl = slice(half * CHUNK, (half + 1) * CHUNK)
                kwu = kwu2[idx][half * GDN_DK:(half + 1) * GDN_DK]
                q_eff = qs[idx][sl] * eg[idx][sl] - aw[idx][sl, 0:GDN_DV]
                g_dec = jnp.exp(gcs[j][CHUNK - 1:CHUNK, h:h + 1])
                per_head[(chunks[j], h)] = (q_eff, aw[idx][sl, GDN_DV:2 * GDN_DV], g_dec, kwu)
        return per_head

    sts = [st_ref[h] for h in range(GDN_HEADS)]

    def chain_step(c, per_head):
        def run():
            rows = slice(c * CHUNK, (c + 1) * CHUNK)
            for h in range(GDN_HEADS):
                q_eff, o_loc, g_dec, kwu = per_head[(c, h)]
                st = sts[h]
                both = _mm(jnp.concatenate([q_eff, kwu[:, 0:GDN_DV]], axis=0), st)
                o = both[0:CHUNK] + o_loc
                sts[h] = g_dec * st - both[CHUNK:CHUNK + GDN_DK] + kwu[:, GDN_DV:2 * GDN_DV]
                o = o * lax.rsqrt(jnp.mean(o * o, -1, keepdims=True) + HEAD_NORM_EPS) * ng
                gate = gate_s[rows, h * GDN_DV:(h + 1) * GDN_DV]
                o_ref[rows, h * GDN_DV:(h + 1) * GDN_DV] = (o * gate * _sigmoid(gate)).astype(o_ref.dtype)
        return run

    n_waves = tb // wave_rows
    for unit in prep_units(0):
        unit()
    for wave in range(n_waves):
        chunks = list(range(wave * GDN_CHUNKS_PER_WAVE, (wave + 1) * GDN_CHUNKS_PER_WAVE))
        if wave + 1 < n_waves:
            pending.extend(prep_units(wave + 1))
        per_head = precompute(chunks)
        while pending:
            fill()
        pending.extend(chain_step(c, per_head) for c in chunks)
    while pending:
        fill()
    for h in range(GDN_HEADS):
        st_ref[h] = sts[h]


def _mixer_call(body, x, params, width, scratch, bsz, seq, tb, name):
    nt = seq // tb
    rows = lambda b, t: (b * nt + t, 0)
    return _call(functools.partial(body, tb=tb), (bsz, nt), ("arbitrary", "arbitrary"),
                 [(x, pl.BlockSpec((tb, D_MODEL), rows))], params,
                 pl.BlockSpec((tb, width), rows), jax.ShapeDtypeStruct((bsz * seq, width), BF16), scratch, name)


def _gdn_call(x, params, bsz, seq, tb):
    wide = pltpu.VMEM((tb, GDN_HEADS * GDN_DK), F32)
    narrow = pltpu.VMEM((tb, LANES), F32)
    scratch = [pltpu.VMEM((tb + 8, GDN_QKV), F32), wide, wide, wide, narrow, narrow, wide,
               pltpu.VMEM((GDN_HEADS, GDN_DK, GDN_DV), F32)]
    return _mixer_call(_gdn_body, x, params, GDN_HEADS * GDN_DV, scratch, bsz, seq, tb, "gated_deltanet")


def _gla_level_arg(s, la, b, ri):
    c = CHUNK
    if s >= 4:
        ref = jnp.concatenate(
            [jnp.broadcast_to(b[g * 2 * s + s - 1:g * 2 * s + s, :], (2 * s, LANES)) for g in range(c // (2 * s))],
            axis=0)
        return -jnp.abs(b - ref)
    if s == 2:
        m = ri & 3
        up = pltpu.roll(la, c - 1, 0)
        dn = pltpu.roll(la, 1, 0)
        return jnp.where(m == 0, up, jnp.where(m == 1, 0.0, jnp.where(m == 2, la, la + dn)))
    return jnp.where((ri & 1) == 1, la, 0.0)


def _gla_scores(qs, ks, las, bs, between):
    c = CHUNK
    ri = lax.broadcasted_iota(jnp.int32, (c, LANES), 0)
    r2, c2 = _iota2((2 * c, c))
    r2 = r2 & (c - 1)
    diag = r2 == c2
    scores = [jnp.where(diag, _mm_nt(_stack_heads(q), k), 0.0) for q, k in zip(qs, ks)]
    between()
    s = c // 2
    while s >= 1:
        row_side = ((ri >> (s.bit_length() - 1)) & 1) == 1
        sib = _sibling_mask(r2, c2, s)
        es = [jnp.exp(_gla_level_arg(s, la, b, ri)) for la, b in zip(las, bs)]
        part = [_mm_nt(_stack_heads(jnp.where(row_side, q * e, 0.0)), jnp.where(row_side, 0.0, k * e))
                for q, k, e in zip(qs, ks, es)]
        between()
        scores = [sc + jnp.where(sib, p, 0.0) for sc, p in zip(scores, part)]
        s //= 2
    return scores


def _gla_body(x_ref, w_ref, w2_ref, gkb_ref, ng_ref, o_ref, h_s, la_s, b_s, st_ref, *, tb):
    t_idx = pl.program_id(1)

    @pl.when(t_idx == 0)
    def _():
        st_ref[...] = jnp.zeros_like(st_ref)

    hk = GLA_HEADS * GLA_DK
    hv = GLA_HEADS * GLA_DV
    ng = ng_ref[...]
    cpi = GLA_CHUNKS_PER_STEP
    sb = cpi * CHUNK
    npair = GLA_HEADS // 2
    jobs = [(j, p) for j in range(cpi) for p in range(npair)]
    row_first = lax.broadcasted_iota(jnp.int32, (LANES, GLA_DV), 0) < GLA_DK

    def prep_units(s):
        rows = slice(s * sb, (s + 1) * sb)
        vals = {}

        def proj(c0, c1):
            def run():
                if "xb" not in vals:
                    vals["xb"] = x_ref[rows, :].astype(BF16)
                h_s[rows, c0:c1] = jnp.dot(vals["xb"], w_ref[:, c0:c1], preferred_element_type=F32)
            return run

        def decay():
            gk_lo = jnp.dot(vals["xb"], w_ref[:, 2 * hk + 2 * hv:GLA_PAD], preferred_element_type=F32)
            z = _mm_f32(gk_lo, w2_ref[...]) + gkb_ref[...]
            la = (jnp.minimum(z, 0.0) - jnp.log(1.0 + jnp.exp(-jnp.abs(z)))) * (1.0 / GLA_GATE_NORM)
            la_s[rows, :] = la
            b_s[rows, :] = _chunk_cumsum(la)

        return [proj(c0, c0 + 2 * LANES) for c0 in range(0, 2 * hk + 2 * hv, 2 * LANES)] + [decay]

    def recurrence(s, pending):
        rows = [slice(s * sb + j * CHUNK, s * sb + (j + 1) * CHUNK) for j in range(cpi)]

        def fill():
            if pending:
                pending.pop(0)()
        qs = [h_s[rows[j], p * LANES:(p + 1) * LANES] * (GLA_DK ** -0.5) for j, p in jobs]
        ks = [h_s[rows[j], hk + p * LANES:hk + (p + 1) * LANES] for j, p in jobs]
        las = [la_s[rows[j], p * LANES:(p + 1) * LANES] for j, p in jobs]
        bs = [b_s[rows[j], p * LANES:(p + 1) * LANES] for j, p in jobs]
        vpair = [h_s[rows[j], 2 * hk + 2 * p * GLA_DV:2 * hk + (2 * p + 2) * GLA_DV] for j, p in jobs]
        scores = _gla_scores(qs, ks, las, bs, fill)
        q_in = [_stack_heads(q * jnp.exp(b)) for q, b in zip(qs, bs)]
        bts = [b.T for b in bs]
        b_end = [bt[:, CHUNK - 1:CHUNK] for bt in bts]
        upd = [_mm(k.T * jnp.exp(be - bt), vp) for k, be, bt, vp in zip(ks, b_end, bts, vpair)]
        intra = [[_mm(sc[half * CHUNK:(half + 1) * CHUNK], vp[:, half * GLA_DV:(half + 1) * GLA_DV])
                  for half in range(2)] for sc, vp in zip(scores, vpair)]
        sts = [st_ref[p] for p in range(npair)]
        for idx, (j, p) in enumerate(jobs):
            st = sts[p]
            inter = _mm(q_in[idx], st)
            sts[p] = st * jnp.exp(b_end[idx]) + jnp.where(row_first, upd[idx][:, 0:GLA_DV], upd[idx][:, GLA_DV:2 * GLA_DV])
            for half in range(2):
                h = 2 * p + half
                o = intra[idx][half] + inter[half * CHUNK:(half + 1) * CHUNK]
                o = o * lax.rsqrt(jnp.mean(o * o, -1, keepdims=True) + HEAD_NORM_EPS) * ng
                gate = h_s[rows[j], 2 * hk + hv + h * GLA_DV:2 * hk + hv + (h + 1) * GLA_DV]
                o_ref[rows[j], h * GLA_DV:(h + 1) * GLA_DV] = (o * gate * _sigmoid(gate)).astype(o_ref.dtype)
        for p in range(npair):
            st_ref[p] = sts[p]
        while pending:
            fill()

    for unit in prep_units(0):
        unit()
    for s in range(tb // sb):
        recurrence(s, prep_units(s + 1) if (s + 1) * sb < tb else [])


def _gla_call(x, params, bsz, seq, tb):
    keys = pltpu.VMEM((tb, GLA_HEADS * GLA_DK), F32)
    scratch = [pltpu.VMEM((tb, GLA_PAD - LANES), F32), keys, keys,
               pltpu.VMEM((GLA_HEADS // 2, 2 * GLA_DK, GLA_DV), F32)]
    return _mixer_call(_gla_body, x, params, GLA_HEADS * GLA_DV, scratch, bsz, seq, tb, "gated_linear_attention")


def _seg_sum_lanes(x):
    first = lax.broadcasted_iota(jnp.int32, x.shape, 1) < RW_HEAD
    lo = jnp.sum(jnp.where(first, x, 0.0), -1, keepdims=True)
    hi = jnp.sum(jnp.where(first, 0.0, x), -1, keepdims=True)
    return jnp.where(first, lo, hi)


def _rw_body(x_ref, w_ref, mu_ref, w0_ref, w2_ref, a0_ref, a2_ref, g2_ref, kk_ref, ka_ref, rk_ref, gng_ref, gnb_ref,
             o_ref, ext_ref, r_s, k_s, v_s, ld_s, ci_s, an_s, bn_s, g_s, st_ref, *, tb):
    t_idx = pl.program_id(1)

    @pl.when(t_idx == 0)
    def _():
        ext_ref[0:8, :] = jnp.zeros((8, RW_PAD), F32)
        st_ref[...] = jnp.zeros_like(st_ref)

    @pl.when(t_idx != 0)
    def _():
        ext_ref[0:8, :] = ext_ref[tb:tb + 8, :]

    w = RW_WIDTH
    n2 = 2 * CHUNK
    cpi = RW_CHUNKS_PER_STEP
    sb = cpi * CHUNK
    npair = RW_HEADS // 2
    jobs = [(j, p) for j in range(cpi) for p in range(npair)]
    strict = _block_causal(n2, strict=True)
    incl = _block_causal(n2, strict=False)
    zero = jnp.zeros((n2, LANES), F32)

    ext_ref[8:tb + 8, :] = jnp.dot(x_ref[...].astype(BF16), w_ref[...], preferred_element_type=F32)

    def shifted(cols):
        cur = ext_ref[8:tb + 8, cols]
        return cur + (ext_ref[7:tb + 7, cols] - cur) * mu_ref[:, cols]

    lo = shifted(slice(3 * w, 3 * w + LANES))
    wv = -_softplus(-(w0_ref[...] + _mm_f32(jnp.tanh(lo), w2_ref[...]))) - 0.5
    ld = -jnp.exp(wv)
    ld_s[...] = ld
    ci_s[...] = _chunk_cumsum(ld)
    a = _sigmoid(a0_ref[...] + _mm(lo, a2_ref[...]))
    glo = shifted(slice(3 * w + LANES, RW_PAD))
    g_s[...] = _mm(_sigmoid(glo), g2_ref[...])
    r_s[...] = shifted(slice(0, w))
    v_s[...] = shifted(slice(2 * w, 3 * w))
    for p in range(npair):
        lanes = slice(p * LANES, (p + 1) * LANES)
        k = shifted(slice(w + p * LANES, w + (p + 1) * LANES))
        t = k * kk_ref[:, lanes]
        kkn = t * lax.rsqrt(_seg_sum_lanes(t * t) + L2_EPS)
        an_s[:, lanes] = -kkn
        bn_s[:, lanes] = kkn * a[:, lanes]
        k_s[:, lanes] = k * (1.0 + (a[:, lanes] - 1.0) * ka_ref[:, lanes])

    def step(it, carry):
        rows = [pl.ds(pl.multiple_of(it * sb + j * CHUNK, CHUNK), CHUNK) for j in range(cpi)]

        def load(src):
            return [src[rows[j], p * LANES:(p + 1) * LANES] for j, p in jobs]

        r, kx, v, ld, ci, an, bn = (load(src) for src in (r_s, k_s, v_s, ld_s, ci_s, an_s, bn_s))
        e_in = [jnp.exp(-c) for c in ci]
        e_out = [jnp.exp(c[CHUNK - 1:CHUNK, :] - c) for c in ci]
        a_s = [_stack_heads(a * jnp.exp(c - t)) for a, c, t in zip(an, ci, ld)]
        r_t = [_stack_heads(x * jnp.exp(c)) for x, c in zip(r, ci)]
        b_hat = [_stack_heads(b * e) for b, e in zip(bn, e_out)]
        k_hat = [_stack_heads(k * e) for k, e in zip(kx, e_out)]
        v_st = [_stack_heads(x) for x in v]
        gram = [_mm_nt(jnp.concatenate([a, x], axis=0),
                       jnp.concatenate([_stack_heads(b * e), _stack_heads(k * e)], axis=0))
                for a, x, b, k, e in zip(a_s, r_t, bn, kx, e_in)]
        a_ab = [jnp.where(strict, g[0:n2, 0:n2], 0.0) for g in gram]
        a_ak = [jnp.where(strict, g[0:n2, n2:2 * n2], 0.0) for g in gram]
        a_rbk = [jnp.concatenate([jnp.where(incl, g[n2:2 * n2, 0:n2], 0.0),
                                  jnp.where(incl, g[n2:2 * n2, n2:2 * n2], 0.0)], axis=1) for g in gram]
        t_inv = _unit_lower_inverses([-a for a in a_ab], n2)
        akv = [_mm(a, x) for a, x in zip(a_ak, v_st)]
        x_au = [_mm(t, jnp.concatenate([a, b], axis=1)) for t, a, b in zip(t_inv, a_s, akv)]
        bk_t = [jnp.concatenate([b.T, k.T], axis=1) for b, k in zip(b_hat, k_hat)]
        zz = [_mm(jnp.concatenate([a, t], axis=0), jnp.concatenate([x, jnp.concatenate([zero, u], axis=1)], axis=0))
              for a, t, x, u in zip(a_rbk, bk_t, x_au, v_st)]
        rw_p = [jnp.concatenate([x + y[0:n2, 0:LANES], y[n2:2 * n2, 0:LANES]], axis=0) for x, y in zip(r_t, zz)]
        g_col = [jnp.exp(c.T[:, CHUNK - 1:CHUNK]) for c in ci]
        sts = [st_ref[p] for p in range(npair)]
        ys = []
        for idx, (j, p) in enumerate(jobs):
            st = sts[p]
            both = _mm(rw_p[idx], st)
            ys.append(both[0:n2] + zz[idx][0:n2, LANES:2 * LANES])
            sts[p] = g_col[idx] * st + both[n2:2 * n2] + zz[idx][n2:2 * n2, LANES:2 * LANES]
        for p in range(npair):
            st_ref[p] = sts[p]
        lanes = [slice(p * LANES, (p + 1) * LANES) for _, p in jobs]
        y = [t[0:CHUNK] + t[CHUNK:n2] for t in ys]
        sums = [_seg_sum_lanes(jnp.concatenate([a, b * k * rk_ref[:, ln]], axis=0))
                for a, b, k, ln in zip(y, r, kx, lanes)]
        d = [a - t[0:CHUNK] * (1.0 / RW_HEAD) for a, t in zip(y, sums)]
        var = [_seg_sum_lanes(a * a) * (1.0 / RW_HEAD) for a in d]
        for idx, (j, p) in enumerate(jobs):
            yn = d[idx] * lax.rsqrt(var[idx] + RW_GN_EPS) * gng_ref[:, lanes[idx]] + gnb_ref[:, lanes[idx]]
            bonus = sums[idx][CHUNK:n2] * v[idx]
            o_ref[rows[j], lanes[idx]] = ((yn + bonus) * g_s[rows[j], lanes[idx]]).astype(o_ref.dtype)
        return carry

    lax.fori_loop(0, tb // sb, step, 0)


def _rw_call(x, params, bsz, seq, tb):
    scratch = ([pltpu.VMEM((tb + 8, RW_PAD), F32)] + [pltpu.VMEM((tb, RW_WIDTH), F32)] * 8
               + [pltpu.VMEM((RW_HEADS // 2, LANES, LANES), F32)])
    return _mixer_call(_rw_body, x, params, RW_WIDTH, scratch, bsz, seq, tb, "rwkv7_time_mix")


def _pad_to(a, size, axis):
    pad = [(0, 0)] * a.ndim
    pad[axis] = (0, size - a.shape[axis])
    return jnp.pad(a, pad)


def _mixer_in_weights(w_in):
    o = 0
    rw = w_in[..., o:o + RW_IN]
    o += RW_IN
    gla = w_in[..., o:o + GLA_IN]
    o += GLA_IN
    gdn = w_in[..., o:o + GDN_IN]
    o += GDN_IN
    gate = w_in[..., o:]
    qkv_w = 2 * GLA_HEADS * GLA_DK + GLA_HEADS * GLA_DV
    gla_p = jnp.concatenate([gla[..., 0:qkv_w], gla[..., qkv_w + GLA_GATE_LORA:], gla[..., qkv_w:qkv_w + GLA_GATE_LORA]], axis=-1)
    gdn_p = jnp.concatenate([gdn[..., 0:GDN_QKV], gdn[..., GDN_QKV + 2 * GDN_HEADS:], gdn[..., GDN_QKV:GDN_QKV + 2 * GDN_HEADS]], axis=-1)
    return (_pad_to(rw, RW_PAD, 2).astype(BF16), _pad_to(gla_p, GLA_PAD, 2).astype(BF16),
            _pad_to(gdn_p, GDN_PAD, 2).astype(BF16), gate.astype(BF16))


def _rows(v):
    return v.reshape(v.shape[:-1] + (1, v.shape[-1]))


def _tiles(n, seq):
    return min(2 * DENSE_SUB, n), min(512, seq), min(1024, seq)


def kernel(x, p, ln_g, ln_b, ffn_w1, ffn_w3, ffn_w2, w_in, rw_mu, rw_w0, rw_w2, rw_a0, rw_a2, rw_g2, rw_k_k, rw_k_a, rw_r_k, rw_gn_g, rw_gn_b, gla_gk_w2, gla_gk_b, gla_norm_g, gdn_conv_w, gdn_a_log, gdn_dt_bias, gdn_norm_g, w_branch, w_o, ple_w_gate, ple_w_proj):
    bsz, seq, d = x.shape
    depth = p.shape[0]
    n = bsz * seq
    tm, tb, tb_gla = _tiles(n, seq)
    x = x.reshape(n, d)
    p = p.reshape(depth, n, PLE_DIM)

    w1, w3, w2 = ffn_w1.astype(BF16), ffn_w3.astype(BF16), ffn_w2.astype(BF16)
    w_rw, w_gla, w_gdn, w_gate = _mixer_in_weights(w_in)
    w_br, w_out = w_branch.astype(BF16), w_o.astype(BF16)
    w_pg, w_pp = ple_w_gate.astype(BF16), ple_w_proj.astype(BF16)
    lng, lnb = _rows(ln_g), _rows(ln_b)
    rw_vecs = [_rows(v) for v in (rw_k_k, rw_k_a, rw_r_k.reshape(depth, RW_WIDTH), rw_gn_g, rw_gn_b)]
    rw_mu_p = _rows(_pad_to(rw_mu, RW_PAD, 1))
    rw_w0_r, rw_a0_r = _rows(rw_w0), _rows(rw_a0)
    rw_w2_p = _pad_to(rw_w2, LANES, 1)
    rw_a2_p = jnp.pad(rw_a2, ((0, 0), (RW_DECAY_LORA, 0), (0, 0)))
    rw_g2_p = _pad_to(rw_g2, 2 * LANES, 1)
    gla_w2_p, gla_b_r, gla_ng = _pad_to(gla_gk_w2, LANES, 1), _rows(gla_gk_b), _rows(gla_norm_g)
    gdn_alog, gdn_dtb, gdn_ng = _rows(_pad_to(gdn_a_log, LANES, 1)), _rows(_pad_to(gdn_dt_bias, LANES, 1)), _rows(gdn_norm_g)

    for i in range(depth):
        x = _ffn_call(x, [_param(w1, i, 0), _param(w3, i, 0), _param(w2, i, 0), _param(lng, i, 0), _param(lnb, i, 0)], tm)
        o_rw = _rw_call(x, [_param(w_rw, i), _param(rw_mu_p, i), _param(rw_w0_r, i), _param(rw_w2_p, i),
                            _param(rw_a0_r, i), _param(rw_a2_p, i), _param(rw_g2_p, i)]
                        + [_param(v, i) for v in rw_vecs], bsz, seq, tb)
        o_gla = _gla_call(x, [_param(w_gla, i), _param(gla_w2_p, i), _param(gla_b_r, i), _param(gla_ng, i)], bsz, seq, tb_gla)
        o_gdn = _gdn_call(x, [_param(w_gdn, i), _param(gdn_conv_w, i), _param(gdn_alog, i), _param(gdn_dtb, i),
                              _param(gdn_ng, i)], bsz, seq, tb_gla)
        x = _merge_call(x, (o_rw, o_gla, o_gdn), [_param(w_gate, i), _param(w_br, i), _param(w_out, i),
                                                  _param(lng, i, 1), _param(lnb, i, 1)], tm)
        x = _ffn_ple_call(x, p, i, [_param(w1, i, 1), _param(w3, i, 1), _param(w2, i, 1), _param(lng, i, 2), _param(lnb, i, 2),
                                    _param(w_pg, i), _param(w_pp, i), _param(lng, i, 3), _param(lnb, i, 3)], tm)
    return x.reshape(bsz, seq, d)
```

```python
import functools

import jax
import jax.numpy as jnp
from jax import lax
from jax.experimental import pallas as pl
from jax.experimental.pallas import tpu as pltpu

F32 = jnp.float32
BF16 = jnp.bfloat16

D_MODEL = 1024
DEPTH = 2
PLE_DIM = 256
D_FF = 2816
LN_EPS = 1e-5
DEEPNORM_ALPHA = (2 * DEPTH) ** 0.25
MACARON_W = 0.5

RW_HEADS = 8
RW_HEAD = 64
RW_WIDTH = RW_HEADS * RW_HEAD
RW_DECAY_LORA = 64
RW_AAA_LORA = 64
RW_GATE_LORA = 160
RW_GN_EPS = 1e-5 * RW_HEAD
RW_IN = 3 * RW_WIDTH + RW_DECAY_LORA + RW_AAA_LORA + RW_GATE_LORA

GLA_HEADS = 4
GLA_DK = 64
GLA_DV = 128
GLA_GATE_LORA = 16
GLA_GATE_NORM = 16.0
GLA_IN = 2 * GLA_HEADS * GLA_DK + 2 * GLA_HEADS * GLA_DV + GLA_GATE_LORA

GDN_HEADS = 4
GDN_DK = 128
GDN_DV = 128
GDN_CONV = 4
GDN_QKV = GDN_HEADS * (2 * GDN_DK + GDN_DV)
GDN_IN = GDN_QKV + 2 * GDN_HEADS + GDN_HEADS * GDN_DV

HEAD_NORM_EPS = 1e-6
L2_EPS = 1e-6
N_BRANCH = 3
BRANCH_WIDTH = 512

LANES = 128
SUBLANES = 8
CHUNK = 64
GDN_CHUNKS_PER_WAVE = 8
RW_CHUNKS_PER_STEP = 4
GLA_CHUNKS_PER_STEP = 4
RW_PAD = 1920
GLA_PAD = 1664
GDN_PAD = 2176
MIX_PAD = RW_PAD + GLA_PAD + GDN_PAD
VMEM_LIMIT = 56 * 1024 * 1024


def _param(arr, *prefix):
    tail = arr.shape[len(prefix):]
    zeros = (0,) * len(tail)
    spec = pl.BlockSpec((None,) * len(prefix) + tail, lambda *_: prefix + zeros, pipeline_mode=pl.Buffered(1))
    return arr, spec


def _call(body, grid, sem, streamed, params, out_specs, out_shape, scratch, name):
    ops = list(streamed) + list(params)
    return pl.pallas_call(
        body,
        grid=grid,
        in_specs=[spec for _, spec in ops],
        out_specs=out_specs,
        out_shape=out_shape,
        scratch_shapes=scratch,
        compiler_params=pltpu.CompilerParams(
            dimension_semantics=sem, vmem_limit_bytes=VMEM_LIMIT,
            allow_input_fusion=[False] * len(streamed) + [True] * len(params)),
        name=name,
    )(*[arr for arr, _ in ops])


def _mm(a, b):
    return jnp.dot(a.astype(BF16), b.astype(BF16), preferred_element_type=F32)


def _mm_nt(a, b):
    return lax.dot_general(a.astype(BF16), b.astype(BF16), (((1,), (1,)), ((), ())),
                           preferred_element_type=F32)


def _split_bf16(x, parts):
    out = []
    for _ in range(parts - 1):
        piece = x.astype(BF16)
        out.append(piece)
        x = x - piece.astype(F32)
    out.append(x.astype(BF16))
    return out


def _mm_f32(a, b):
    a_hi, a_lo = _split_bf16(a, 2)
    b_hi, b_lo = _split_bf16(b, 2)
    dot = functools.partial(jnp.dot, preferred_element_type=F32)
    return dot(a_hi, b_hi) + (dot(a_hi, b_lo) + dot(a_lo, b_hi))


def _chunk_cumsum(x):
    tril = _tril_f32(CHUNK).astype(BF16)
    out = []
    for r0 in range(0, x.shape[0], CHUNK):
        pieces = _split_bf16(x[r0:r0 + CHUNK], 3)
        acc = jnp.dot(tril, pieces[2], preferred_element_type=F32)
        acc = acc + jnp.dot(tril, pieces[1], preferred_element_type=F32)
        out.append(acc + jnp.dot(tril, pieces[0], preferred_element_type=F32))
    return jnp.concatenate(out, axis=0)


def _layer_norm(y, g, b):
    m = jnp.mean(y, -1, keepdims=True)
    d = y - m
    var = jnp.mean(d * d, -1, keepdims=True)
    return d * lax.rsqrt(var + LN_EPS) * g + b


def _sigmoid(x):
    return 0.5 * jnp.tanh(0.5 * x) + 0.5


def _softplus(x):
    return jnp.maximum(x, 0.0) + jnp.log(1.0 + jnp.exp(-jnp.abs(x)))


def _iota2(shape):
    return (lax.broadcasted_iota(jnp.int32, shape, 0), lax.broadcasted_iota(jnp.int32, shape, 1))


def _tril_f32(n):
    r, c = _iota2((n, n))
    return jnp.where(r >= c, 1.0, 0.0).astype(F32)


def _sibling_mask(r, c, s):
    k = s.bit_length() - 1
    return ((r >> (k + 1)) == (c >> (k + 1))) & (((r >> k) & 1) == 1) & (((c >> k) & 1) == 0)


def _unit_lower_inverses(lows, n, between=lambda: None):
    r, c = _iota2((n, n))
    eye = jnp.where(r == c, 1.0, 0.0).astype(F32)
    first = _sibling_mask(r, c, 1)
    xs = [eye - jnp.where(first, low, 0.0) for low in lows]
    s = 2
    while s < CHUNK:
        sib = _sibling_mask(r, c, s)
        if s % SUBLANES:
            ts = [_mm(jnp.where(sib, low, 0.0), x) for low, x in zip(lows, xs)]
            between()
            us = [_mm(x, t) for x, t in zip(xs, ts)]
            between()
            xs = [x - u for x, u in zip(xs, us)]
        else:
            blank = jnp.zeros((s, n), F32)

            def odd(a, s=s):
                return jnp.concatenate([a[g * s:(g + 1) * s] for g in range(1, n // s, 2)], axis=0)

            def spread(a, s=s, blank=blank):
                return jnp.concatenate([piece for g in range(n // (2 * s)) for piece in (blank, a[g * s:(g + 1) * s])],
                                       axis=0)

            ts = [_mm(odd(jnp.where(sib, low, 0.0)), x) for low, x in zip(lows, xs)]
            between()
            us = [_mm(odd(x), spread(t)) for x, t in zip(xs, ts)]
            between()
            xs = [x - spread(u) for x, u in zip(xs, us)]
        s *= 2
    return xs


def _stack_heads(z):
    lane = lax.broadcasted_iota(jnp.int32, z.shape, 1)
    first = lane < (LANES // 2)
    return jnp.concatenate([jnp.where(first, z, 0.0), jnp.where(first, 0.0, z)], axis=0)


def _block_causal(n, strict):
    r, c = _iota2((n, n))
    same = (r >> 6) == (c >> 6)
    return same & ((r > c) if strict else (r >= c))


MXU_TILE = 256
FF_CHUNK = MXU_TILE
DENSE_SUB = 512


def _sub_tiles(ref):
    return [slice(r0, r0 + DENSE_SUB) for r0 in range(0, ref.shape[0], DENSE_SUB)]


def _swiglu_ln(x, w1_ref, w3_ref, w2_ref, g, b):
    xb = x.astype(BF16)
    acc = None
    for c0 in range(0, D_FF, FF_CHUNK):
        h1 = jnp.dot(xb, w1_ref[:, c0:c0 + FF_CHUNK], preferred_element_type=F32)
        h3 = jnp.dot(xb, w3_ref[:, c0:c0 + FF_CHUNK], preferred_element_type=F32)
        act = (h1 * _sigmoid(h1) * h3).astype(BF16)
        part = jnp.dot(act, w2_ref[c0:c0 + FF_CHUNK, :], preferred_element_type=F32)
        acc = part if acc is None else acc + part
    return _layer_norm(DEEPNORM_ALPHA * x + MACARON_W * acc, g, b)


def _ffn_body(x_ref, w1_ref, w3_ref, w2_ref, g_ref, b_ref, o_ref):
    for rows in _sub_tiles(x_ref):
        o_ref[rows, :] = _swiglu_ln(x_ref[rows, :], w1_ref, w3_ref, w2_ref, g_ref[...], b_ref[...])


def _ffn_ple_body(x_ref, p_ref, w1_ref, w3_ref, w2_ref, g_ref, b_ref, wg_ref, wp_ref, g2_ref, b2_ref, o_ref):
    tiles = _sub_tiles(x_ref)
    embs = [jnp.dot(p_ref[rows, :].astype(BF16), wp_ref[...], preferred_element_type=F32) for rows in tiles]
    xs = [_swiglu_ln(x_ref[rows, :], w1_ref, w3_ref, w2_ref, g_ref[...], b_ref[...]) for rows in tiles]
    gates = [_sigmoid(jnp.dot(x.astype(BF16), wg_ref[...], preferred_element_type=F32)) for x in xs]
    for rows, x, gate, emb in zip(tiles, xs, gates, embs):
        o_ref[rows, :] = _layer_norm(DEEPNORM_ALPHA * x + gate * emb, g2_ref[...], b2_ref[...])


def _token_rows(x, tm):
    return x, pl.BlockSpec((tm, x.shape[1]), lambda i: (i, 0))


def _ffn_call(x, params, tm):
    n = x.shape[0]
    return _call(_ffn_body, (n // tm,), ("arbitrary",), [_token_rows(x, tm)], params,
                 pl.BlockSpec((tm, D_MODEL), lambda i: (i, 0)), jax.ShapeDtypeStruct((n, D_MODEL), F32), [], "ffn_ln")


def _ffn_ple_call(x, p, layer, params, tm):
    n = x.shape[0]
    p_rows = (p, pl.BlockSpec((None, tm, PLE_DIM), lambda i: (layer, i, 0)))
    return _call(_ffn_ple_body, (n // tm,), ("arbitrary",), [_token_rows(x, tm), p_rows], params,
                 pl.BlockSpec((tm, D_MODEL), lambda i: (i, 0)), jax.ShapeDtypeStruct((n, D_MODEL), F32), [], "ffn_ple_ln")


def _merge_body(x_ref, orw_ref, ogla_ref, ogdn_ref, wgate_ref, wbr_ref, wo_ref, g_ref, b_ref, o_ref):
    tiles = _sub_tiles(x_ref)
    xs = [x_ref[rows, :] for rows in tiles]
    mergeds = []
    for rows, x in zip(tiles, xs):
        xb = x.astype(BF16)
        merged = None
        for n, br_ref in enumerate((orw_ref, ogla_ref, ogdn_ref)):
            gate = _sigmoid(jnp.dot(xb, wgate_ref[:, n * D_MODEL:(n + 1) * D_MODEL], preferred_element_type=F32))
            term = gate * jnp.dot(br_ref[rows, :], wbr_ref[n], preferred_element_type=F32)
            merged = term if merged is None else merged + term
        mergeds.append(merged.astype(BF16))
    mixes = [jnp.dot(m, wo_ref[...], preferred_element_type=F32) for m in mergeds]
    for rows, x, mix in zip(tiles, xs, mixes):
        o_ref[rows, :] = _layer_norm(DEEPNORM_ALPHA * x + mix, g_ref[...], b_ref[...])


def _merge_call(x, branches, params, tm):
    n = x.shape[0]
    return _call(_merge_body, (n // tm,), ("arbitrary",), [_token_rows(t, tm) for t in (x,) + tuple(branches)], params,
                 pl.BlockSpec((tm, D_MODEL), lambda i: (i, 0)), jax.ShapeDtypeStruct((n, D_MODEL), F32), [], "merge_ln")


def _gdn_body(x_ref, w_ref, cw_ref, alog_ref, dtb_ref, ng_ref, o_ref,
              ext_ref, q_s, k_s, v_s, g_s, beta_s, gate_s, st_ref, *, tb):
    t_idx = pl.program_id(1)

    @pl.when(t_idx == 0)
    def _():
        ext_ref[0:8, :] = jnp.zeros((8, GDN_QKV), F32)
        st_ref[...] = jnp.zeros_like(st_ref)

    @pl.when(t_idx != 0)
    def _():
        ext_ref[0:8, :] = ext_ref[tb:tb + 8, :]

    hw = GDN_HEADS * GDN_DK
    wave_rows = GDN_CHUNKS_PER_WAVE * CHUNK

    def prep_units(wave):
        r0 = wave * wave_rows
        rows = slice(r0, r0 + wave_rows)
        vals = {}

        def project(c0, c1):
            def run():
                if "xb" not in vals:
                    vals["xb"] = x_ref[rows, :].astype(BF16)
                ext_ref[8 + r0:8 + r0 + wave_rows, c0:c1] = jnp.dot(vals["xb"], w_ref[:, c0:c1],
                                                                      preferred_element_type=F32)
            return run

        def gates():
            xb = vals["xb"]
            gate_s[rows, :] = jnp.dot(xb, w_ref[:, GDN_QKV:GDN_QKV + GDN_HEADS * GDN_DV], preferred_element_type=F32)
            ab = jnp.dot(xb, w_ref[:, GDN_QKV + GDN_HEADS * GDN_DV:GDN_PAD], preferred_element_type=F32)
            g_s[rows, :] = _chunk_cumsum(-jnp.exp(alog_ref[...]) * _softplus(ab + dtb_ref[...]))
            beta_s[rows, :] = _sigmoid(ab)

        def conv(grp, dst):
            def run():
                cols = slice(grp * hw, (grp + 1) * hw)
                acc = ext_ref[8 + r0:8 + r0 + wave_rows, cols] * cw_ref[GDN_CONV - 1:GDN_CONV, cols]
                for d in range(1, GDN_CONV):
                    acc = acc + ext_ref[8 + r0 - d:8 + r0 - d + wave_rows, cols] * cw_ref[GDN_CONV - 1 - d:GDN_CONV - d, cols]
                act = acc * _sigmoid(acc)
                if grp == 2:
                    dst[rows, :] = act
                else:
                    scale = GDN_DK ** -0.5 if grp == 0 else 1.0
                    for h in range(GDN_HEADS):
                        t = act[:, h * GDN_DK:(h + 1) * GDN_DK]
                        dst[rows, h * GDN_DK:(h + 1) * GDN_DK] = t * (lax.rsqrt(jnp.sum(t * t, -1, keepdims=True) + L2_EPS) * scale)
            return run

        return ([project(g * hw, (g + 1) * hw) for g in range(3)] + [gates]
                + [conv(g, dst) for g, dst in enumerate((q_s, k_s, v_s))])

    n2 = 2 * CHUNK
    ng = ng_ref[...]

    causal = _block_causal(n2, strict=False)
    strict = _block_causal(n2, strict=True)
    first = lax.broadcasted_iota(jnp.int32, (GDN_DK, n2), 1) < CHUNK
    pending = []

    def fill():
        if pending:
            pending.pop(0)()

    def pair_rows(src, rows, pr, width):
        return jnp.concatenate([src[rows, h * width:(h + 1) * width] for h in (2 * pr, 2 * pr + 1)], axis=0)

    def pair_col(a, pr, off):
        return jnp.concatenate([a[:, off + h:off + h + 1] for h in (2 * pr, 2 * pr + 1)], axis=0)

    def precompute(chunks):
        jobs = [(j, pr) for j in range(len(chunks)) for pr in range(GDN_HEADS // 2)]
        rows = [slice(c * CHUNK, (c + 1) * CHUNK) for c in chunks]
        gcs = [g_s[rw, :] for rw in rows]
        betas = [beta_s[rw, :] for rw in rows]
        gi = [pair_col(gcs[j], pr, 0) for j, pr in jobs]
        bcol = [pair_col(betas[j], pr, GDN_HEADS) for j, pr in jobs]
        qs = [pair_rows(q_s, rows[j], pr, GDN_DK) for j, pr in jobs]
        ks = [pair_rows(k_s, rows[j], pr, GDN_DK) for j, pr in jobs]
        vs = [pair_rows(v_s, rows[j], pr, GDN_DV) for j, pr in jobs]
        dec = []
        for g in gi:
            gi_b = jnp.broadcast_to(g, (n2, n2))
            diff = gi_b - gi_b.T
            dec.append(jnp.where(causal, jnp.exp(jnp.where(causal, diff, 0.0)), 0.0))
        kb = [k * b for k, b in zip(ks, bcol)]
        gram = [_mm_nt(jnp.concatenate([a, q], axis=0), k) for a, q, k in zip(kb, qs, ks)]
        fill()
        low = [jnp.where(strict, g[0:n2] * d, 0.0) for g, d in zip(gram, dec)]
        att = [g[n2:2 * n2] * d for g, d in zip(gram, dec)]
        t_inv = _unit_lower_inverses(low, n2, between=fill)
        eg = [jnp.exp(g) for g in gi]
        wu = [_mm(t, jnp.concatenate([a * e, v * b], axis=1))
              for t, a, e, v, b in zip(t_inv, kb, eg, vs, bcol)]
        fill()
        aw = [_mm(a, x) for a, x in zip(att, wu)]
        fill()
        g_end = [jnp.concatenate([jnp.broadcast_to(gcs[j][CHUNK - 1:CHUNK, h:h + 1], (CHUNK, 1))
                                  for h in (2 * pr, 2 * pr + 1)], axis=0) for j, pr in jobs]
        k_hat_t = [(k * jnp.exp(ge - g)).T for k, ge, g in zip(ks, g_end, gi)]
        kwu2 = [_mm(jnp.concatenate([jnp.where(first, kt, 0.0), jnp.where(first, 0.0, kt)], axis=0), x)
                for kt, x in zip(k_hat_t, wu)]
        fill()
        per_head = {}
        for idx, (j, pr) in enumerate(jobs):
            for half, h in enumerate((2 * pr, 2 * pr + 1)):
                sl = slice(half * CHUNK, (half + 1) * CHUNK)
                kwu = kwu2[idx][half * GDN_DK:(half + 1) * GDN_DK]
                q_eff = qs[idx][sl] * eg[idx][sl] - aw[idx][sl, 0:GDN_DV]
                g_dec = jnp.exp(gcs[j][CHUNK - 1:CHUNK, h:h + 1])
                per_head[(chunks[j], h)] = (q_eff, aw[idx][sl, GDN_DV:2 * GDN_DV], g_dec, kwu)
        return per_head

    sts = [st_ref[h] for h in range(GDN_HEADS)]

    def chain_step(c, per_head):
        def run():
            rows = slice(c * CHUNK, (c + 1) * CHUNK)
            for h in range(GDN_HEADS):
                q_eff, o_loc, g_dec, kwu = per_head[(c, h)]
                st = sts[h]
                both = _mm(jnp.concatenate([q_eff, kwu[:, 0:GDN_DV]], axis=0), st)
                o = both[0:CHUNK] + o_loc
                sts[h] = g_dec * st - both[CHUNK:CHUNK + GDN_DK] + kwu[:, GDN_DV:2 * GDN_DV]
                o = o * lax.rsqrt(jnp.mean(o * o, -1, keepdims=True) + HEAD_NORM_EPS) * ng
                gate = gate_s[rows, h * GDN_DV:(h + 1) * GDN_DV]
                o_ref[rows, h * GDN_DV:(h + 1) * GDN_DV] = (o * gate * _sigmoid(gate)).astype(o_ref.dtype)
        return run

    n_waves = tb // wave_rows
    for unit in prep_units(0):
        unit()
    for wave in range(n_waves):
        chunks = list(range(wave * GDN_CHUNKS_PER_WAVE, (wave + 1) * GDN_CHUNKS_PER_WAVE))
        if wave + 1 < n_waves:
            pending.extend(prep_units(wave + 1))
        per_head = precompute(chunks)
        while pending:
            fill()
        pending.extend(chain_step(c, per_head) for c in chunks)
    while pending:
        fill()
    for h in range(GDN_HEADS):
        st_ref[h] = sts[h]


def _mixer_call(body, x, params, width, scratch, bsz, seq, tb, name):
    nt = seq // tb
    rows = lambda b, t: (b * nt + t, 0)
    return _call(functools.partial(body, tb=tb), (bsz, nt), ("arbitrary", "arbitrary"),
                 [(x, pl.BlockSpec((tb, D_MODEL), rows))], params,
                 pl.BlockSpec((tb, width), rows), jax.ShapeDtypeStruct((bsz * seq, width), BF16), scratch, name)


def _gdn_call(x, params, bsz, seq, tb):
    wide = pltpu.VMEM((tb, GDN_HEADS * GDN_DK), F32)
    narrow = pltpu.VMEM((tb, LANES), F32)
    scratch = [pltpu.VMEM((tb + 8, GDN_QKV), F32), wide, wide, wide, narrow, narrow, wide,
               pltpu.VMEM((GDN_HEADS, GDN_DK, GDN_DV), F32)]
    return _mixer_call(_gdn_body, x, params, GDN_HEADS * GDN_DV, scratch, bsz, seq, tb, "gated_deltanet")


def _gla_level_arg(s, la, b, ri):
    c = CHUNK
    if s >= 4:
        ref = jnp.concatenate(
            [jnp.broadcast_to(b[g * 2 * s + s - 1:g * 2 * s + s, :], (2 * s, LANES)) for g in range(c // (2 * s))],
            axis=0)
        return -jnp.abs(b - ref)
    if s == 2:
        m = ri & 3
        up = pltpu.roll(la, c - 1, 0)
        dn = pltpu.roll(la, 1, 0)
        return jnp.where(m == 0, up, jnp.where(m == 1, 0.0, jnp.where(m == 2, la, la + dn)))
    return jnp.where((ri & 1) == 1, la, 0.0)


def _gla_scores(qs, ks, las, bs, between):
    c = CHUNK
    ri = lax.broadcasted_iota(jnp.int32, (c, LANES), 0)
    r2, c2 = _iota2((2 * c, c))
    r2 = r2 & (c - 1)
    diag = r2 == c2
    scores = [jnp.where(diag, _mm_nt(_stack_heads(q), k), 0.0) for q, k in zip(qs, ks)]
    between()
    s = c // 2
    while s >= 1:
        row_side = ((ri >> (s.bit_length() - 1)) & 1) == 1
        sib = _sibling_mask(r2, c2, s)
        es = [jnp.exp(_gla_level_arg(s, la, b, ri)) for la, b in zip(las, bs)]
        part = [_mm_nt(_stack_heads(jnp.where(row_side, q * e, 0.0)), jnp.where(row_side, 0.0, k * e))
                for q, k, e in zip(qs, ks, es)]
        between()
        scores = [sc + jnp.where(sib, p, 0.0) for sc, p in zip(scores, part)]
        s //= 2
    return scores


def _gla_body(x_ref, w_ref, w2_ref, gkb_ref, ng_ref, o_ref, h_s, la_s, b_s, st_ref, *, tb):
    t_idx = pl.program_id(1)

    @pl.when(t_idx == 0)
    def _():
        st_ref[...] = jnp.zeros_like(st_ref)

    hk = GLA_HEADS * GLA_DK
    hv = GLA_HEADS * GLA_DV
    ng = ng_ref[...]
    cpi = GLA_CHUNKS_PER_STEP
    sb = cpi * CHUNK
    npair = GLA_HEADS // 2
    jobs = [(j, p) for j in range(cpi) for p in range(npair)]
    row_first = lax.broadcasted_iota(jnp.int32, (LANES, GLA_DV), 0) < GLA_DK

    def prep_units(s):
        rows = slice(s * sb, (s + 1) * sb)
        vals = {}

        def proj(c0, c1):
            def run():
                if "xb" not in vals:
                    vals["xb"] = x_ref[rows, :].astype(BF16)
                h_s[rows, c0:c1] = jnp.dot(vals["xb"], w_ref[:, c0:c1], preferred_element_type=F32)
            return run

        def decay():
            gk_lo = jnp.dot(vals["xb"], w_ref[:, 2 * hk + 2 * hv:GLA_PAD], preferred_element_type=F32)
            z = _mm_f32(gk_lo, w2_ref[...]) + gkb_ref[...]
            la = (jnp.minimum(z, 0.0) - jnp.log(1.0 + jnp.exp(-jnp.abs(z)))) * (1.0 / GLA_GATE_NORM)
            la_s[rows, :] = la
            b_s[rows, :] = _chunk_cumsum(la)

        return [proj(c0, c0 + 2 * LANES) for c0 in range(0, 2 * hk + 2 * hv, 2 * LANES)] + [decay]

    def recurrence(s, pending):
        rows = [slice(s * sb + j * CHUNK, s * sb + (j + 1) * CHUNK) for j in range(cpi)]

        def fill():
            if pending:
                pending.pop(0)()
        qs = [h_s[rows[j], p * LANES:(p + 1) * LANES] * (GLA_DK ** -0.5) for j, p in jobs]
        ks = [h_s[rows[j], hk + p * LANES:hk + (p + 1) * LANES] for j, p in jobs]
        las = [la_s[rows[j], p * LANES:(p + 1) * LANES] for j, p in jobs]
        bs = [b_s[rows[j], p * LANES:(p + 1) * LANES] for j, p in jobs]
        vpair = [h_s[rows[j], 2 * hk + 2 * p * GLA_DV:2 * hk + (2 * p + 2) * GLA_DV] for j, p in jobs]
        scores = _gla_scores(qs, ks, las, bs, fill)
        q_in = [_stack_heads(q * jnp.exp(b)) for q, b in zip(qs, bs)]
        bts = [b.T for b in bs]
        b_end = [bt[:, CHUNK - 1:CHUNK] for bt in bts]
        upd = [_mm(k.T * jnp.exp(be - bt), vp) for k, be, bt, vp in zip(ks, b_end, bts, vpair)]
        intra = [[_mm(sc[half * CHUNK:(half + 1) * CHUNK], vp[:, half * GLA_DV:(half + 1) * GLA_DV])
                  for half in range(2)] for sc, vp in zip(scores, vpair)]
        sts = [st_ref[p] for p in range(npair)]
        for idx, (j, p) in enumerate(jobs):
            st = sts[p]
            inter = _mm(q_in[idx], st)
            sts[p] = st * jnp.exp(b_end[idx]) + jnp.where(row_first, upd[idx][:, 0:GLA_DV], upd[idx][:, GLA_DV:2 * GLA_DV])
            for half in range(2):
                h = 2 * p + half
                o = intra[idx][half] + inter[half * CHUNK:(half + 1) * CHUNK]
                o = o * lax.rsqrt(jnp.mean(o * o, -1, keepdims=True) + HEAD_NORM_EPS) * ng
                gate = h_s[rows[j], 2 * hk + hv + h * GLA_DV:2 * hk + hv + (h + 1) * GLA_DV]
                o_ref[rows[j], h * GLA_DV:(h + 1) * GLA_DV] = (o * gate * _sigmoid(gate)).astype(o_ref.dtype)
        for p in range(npair):
            st_ref[p] = sts[p]
        while pending:
            fill()

    for unit in prep_units(0):
        unit()
    for s in range(tb // sb):
        recurrence(s, prep_units(s + 1) if (s + 1) * sb < tb else [])


def _gla_call(x, params, bsz, seq, tb):
    keys = pltpu.VMEM((tb, GLA_HEADS * GLA_DK), F32)
    scratch = [pltpu.VMEM((tb, GLA_PAD - LANES), F32), keys, keys,
               pltpu.VMEM((GLA_HEADS // 2, 2 * GLA_DK, GLA_DV), F32)]
    return _mixer_call(_gla_body, x, params, GLA_HEADS * GLA_DV, scratch, bsz, seq, tb, "gated_linear_attention")


def _seg_sum_lanes(x):
    first = lax.broadcasted_iota(jnp.int32, x.shape, 1) < RW_HEAD
    lo = jnp.sum(jnp.where(first, x, 0.0), -1, keepdims=True)
    hi = jnp.sum(jnp.where(first, 0.0, x), -1, keepdims=True)
    return jnp.where(first, lo, hi)


def _rw_body(x_ref, w_ref, mu_ref, w0_ref, w2_ref, a0_ref, a2_ref, g2_ref, kk_ref, ka_ref, rk_ref, gng_ref, gnb_ref,
             o_ref, ext_ref, r_s, k_s, v_s, ld_s, ci_s, an_s, bn_s, g_s, st_ref, *, tb):
    t_idx = pl.program_id(1)

    @pl.when(t_idx == 0)
    def _():
        ext_ref[0:8, :] = jnp.zeros((8, RW_PAD), F32)
        st_ref[...] = jnp.zeros_like(st_ref)

    @pl.when(t_idx != 0)
    def _():
        ext_ref[0:8, :] = ext_ref[tb:tb + 8, :]

    w = RW_WIDTH
    n2 = 2 * CHUNK
    cpi = RW_CHUNKS_PER_STEP
    sb = cpi * CHUNK
    npair = RW_HEADS // 2
    jobs = [(j, p) for j in range(cpi) for p in range(npair)]
    strict = _block_causal(n2, strict=True)
    incl = _block_causal(n2, strict=False)
    zero = jnp.zeros((n2, LANES), F32)

    ext_ref[8:tb + 8, :] = jnp.dot(x_ref[...].astype(BF16), w_ref[...], preferred_element_type=F32)

    def shifted(cols):
        cur = ext_ref[8:tb + 8, cols]
        return cur + (ext_ref[7:tb + 7, cols] - cur) * mu_ref[:, cols]

    lo = shifted(slice(3 * w, 3 * w + LANES))
    wv = -_softplus(-(w0_ref[...] + _mm_f32(jnp.tanh(lo), w2_ref[...]))) - 0.5
    ld = -jnp.exp(wv)
    ld_s[...] = ld
    ci_s[...] = _chunk_cumsum(ld)
    a = _sigmoid(a0_ref[...] + _mm(lo, a2_ref[...]))
    glo = shifted(slice(3 * w + LANES, RW_PAD))
    g_s[...] = _mm(_sigmoid(glo), g2_ref[...])
    r_s[...] = shifted(slice(0, w))
    v_s[...] = shifted(slice(2 * w, 3 * w))
    for p in range(npair):
        lanes = slice(p * LANES, (p + 1) * LANES)
        k = shifted(slice(w + p * LANES, w + (p + 1) * LANES))
        t = k * kk_ref[:, lanes]
        kkn = t * lax.rsqrt(_seg_sum_lanes(t * t) + L2_EPS)
        an_s[:, lanes] = -kkn
        bn_s[:, lanes] = kkn * a[:, lanes]
        k_s[:, lanes] = k * (1.0 + (a[:, lanes] - 1.0) * ka_ref[:, lanes])

    def step(it, carry):
        rows = [pl.ds(pl.multiple_of(it * sb + j * CHUNK, CHUNK), CHUNK) for j in range(cpi)]

        def load(src):
            return [src[rows[j], p * LANES:(p + 1) * LANES] for j, p in jobs]

        r, kx, v, ld, ci, an, bn = (load(src) for src in (r_s, k_s, v_s, ld_s, ci_s, an_s, bn_s))
        e_in = [jnp.exp(-c) for c in ci]
        e_out = [jnp.exp(c[CHUNK - 1:CHUNK, :] - c) for c in ci]
        a_s = [_stack_heads(a * jnp.exp(c - t)) for a, c, t in zip(an, ci, ld)]
        r_t = [_stack_heads(x * jnp.exp(c)) for x, c in zip(r, ci)]
        b_hat = [_stack_heads(b * e) for b, e in zip(bn, e_out)]
        k_hat = [_stack_heads(k * e) for k, e in zip(kx, e_out)]
        v_st = [_stack_heads(x) for x in v]
        gram = [_mm_nt(jnp.concatenate([a, x], axis=0),
                       jnp.concatenate([_stack_heads(b * e), _stack_heads(k * e)], axis=0))
                for a, x, b, k, e in zip(a_s, r_t, bn, kx, e_in)]
        a_ab = [jnp.where(strict, g[0:n2, 0:n2], 0.0) for g in gram]
        a_ak = [jnp.where(strict, g[0:n2, n2:2 * n2], 0.0) for g in gram]
        a_rbk = [jnp.concatenate([jnp.where(incl, g[n2:2 * n2, 0:n2], 0.0),
                                  jnp.where(incl, g[n2:2 * n2, n2:2 * n2], 0.0)], axis=1) for g in gram]
        t_inv = _unit_lower_inverses([-a for a in a_ab], n2)
        akv = [_mm(a, x) for a, x in zip(a_ak, v_st)]
        x_au = [_mm(t, jnp.concatenate([a, b], axis=1)) for t, a, b in zip(t_inv, a_s, akv)]
        bk_t = [jnp.concatenate([b.T, k.T], axis=1) for b, k in zip(b_hat, k_hat)]
        zz = [_mm(jnp.concatenate([a, t], axis=0), jnp.concatenate([x, jnp.concatenate([zero, u], axis=1)], axis=0))
              for a, t, x, u in zip(a_rbk, bk_t, x_au, v_st)]
        rw_p = [jnp.concatenate([x + y[0:n2, 0:LANES], y[n2:2 * n2, 0:LANES]], axis=0) for x, y in zip(r_t, zz)]
        g_col = [jnp.exp(c.T[:, CHUNK - 1:CHUNK]) for c in ci]
        sts = [st_ref[p] for p in range(npair)]
        ys = []
        for idx, (j, p) in enumerate(jobs):
            st = sts[p]
            both = _mm(rw_p[idx], st)
            ys.append(both[0:n2] + zz[idx][0:n2, LANES:2 * LANES])
            sts[p] = g_col[idx] * st + both[n2:2 * n2] + zz[idx][n2:2 * n2, LANES:2 * LANES]
        for p in range(npair):
            st_ref[p] = sts[p]
        lanes = [slice(p * LANES, (p + 1) * LANES) for _, p in jobs]
        y = [t[0:CHUNK] + t[CHUNK:n2] for t in ys]
        sums = [_seg_sum_lanes(jnp.concatenate([a, b * k * rk_ref[:, ln]], axis=0))
                for a, b, k, ln in zip(y, r, kx, lanes)]
        d = [a - t[0:CHUNK] * (1.0 / RW_HEAD) for a, t in zip(y, sums)]
        var = [_seg_sum_lanes(a * a) * (1.0 / RW_HEAD) for a in d]
        for idx, (j, p) in enumerate(jobs):
            yn = d[idx] * lax.rsqrt(var[idx] + RW_GN_EPS) * gng_ref[:, lanes[idx]] + gnb_ref[:, lanes[idx]]
            bonus = sums[idx][CHUNK:n2] * v[idx]
            o_ref[rows[j], lanes[idx]] = ((yn + bonus) * g_s[rows[j], lanes[idx]]).astype(o_ref.dtype)
        return carry

    lax.fori_loop(0, tb // sb, step, 0)


def _rw_call(x, params, bsz, seq, tb):
    scratch = ([pltpu.VMEM((tb + 8, RW_PAD), F32)] + [pltpu.VMEM((tb, RW_WIDTH), F32)] * 8
               + [pltpu.VMEM((RW_HEADS // 2, LANES, LANES), F32)])
    return _mixer_call(_rw_body, x, params, RW_WIDTH, scratch, bsz, seq, tb, "rwkv7_time_mix")


def _pad_to(a, size, axis):
    pad = [(0, 0)] * a.ndim
    pad[axis] = (0, size - a.shape[axis])
    return jnp.pad(a, pad)


def _mixer_in_weights(w_in):
    o = 0
    rw = w_in[..., o:o + RW_IN]
    o += RW_IN
    gla = w_in[..., o:o + GLA_IN]
    o += GLA_IN
    gdn = w_in[..., o:o + GDN_IN]
    o += GDN_IN
    gate = w_in[..., o:]
    qkv_w = 2 * GLA_HEADS * GLA_DK + GLA_HEADS * GLA_DV
    gla_p = jnp.concatenate([gla[..., 0:qkv_w], gla[..., qkv_w + GLA_GATE_LORA:], gla[..., qkv_w:qkv_w + GLA_GATE_LORA]], axis=-1)
    gdn_p = jnp.concatenate([gdn[..., 0:GDN_QKV], gdn[..., GDN_QKV + 2 * GDN_HEADS:], gdn[..., GDN_QKV:GDN_QKV + 2 * GDN_HEADS]], axis=-1)
    return (_pad_to(rw, RW_PAD, 2).astype(BF16), _pad_to(gla_p, GLA_PAD, 2).astype(BF16),
            _pad_to(gdn_p, GDN_PAD, 2).astype(BF16), gate.astype(BF16))


def _rows(v):
    return v.reshape(v.shape[:-1] + (1, v.shape[-1]))


def _tiles(n, seq):
    return min(2 * DENSE_SUB, n), min(512, seq), min(1024, seq)


def kernel(x, p, ln_g, ln_b, ffn_w1, ffn_w3, ffn_w2, w_in, rw_mu, rw_w0, rw_w2, rw_a0, rw_a2, rw_g2, rw_k_k, rw_k_a, rw_r_k, rw_gn_g, rw_gn_b, gla_gk_w2, gla_gk_b, gla_norm_g, gdn_conv_w, gdn_a_log, gdn_dt_bias, gdn_norm_g, w_branch, w_o, ple_w_gate, ple_w_proj):
    bsz, seq, d = x.shape
    depth = p.shape[0]
    n = bsz * seq
    tm, tb, tb_gla = _tiles(n, seq)
    x = x.reshape(n, d)
    p = p.reshape(depth, n, PLE_DIM)

    w1, w3, w2 = ffn_w1.astype(BF16), ffn_w3.astype(BF16), ffn_w2.astype(BF16)
    w_rw, w_gla, w_gdn, w_gate = _mixer_in_weights(w_in)
    w_br, w_out = w_branch.astype(BF16), w_o.astype(BF16)
    w_pg, w_pp = ple_w_gate.astype(BF16), ple_w_proj.astype(BF16)
    lng, lnb = _rows(ln_g), _rows(ln_b)
    rw_vecs = [_rows(v) for v in (rw_k_k, rw_k_a, rw_r_k.reshape(depth, RW_WIDTH), rw_gn_g, rw_gn_b)]
    rw_mu_p = _rows(_pad_to(rw_mu, RW_PAD, 1))
    rw_w0_r, rw_a0_r = _rows(rw_w0), _rows(rw_a0)
    rw_w2_p = _pad_to(rw_w2, LANES, 1)
    rw_a2_p = jnp.pad(rw_a2, ((0, 0), (RW_DECAY_LORA, 0), (0, 0)))
    rw_g2_p = _pad_to(rw_g2, 2 * LANES, 1)
    gla_w2_p, gla_b_r, gla_ng = _pad_to(gla_gk_w2, LANES, 1), _rows(gla_gk_b), _rows(gla_norm_g)
    gdn_alog, gdn_dtb, gdn_ng = _rows(_pad_to(gdn_a_log, LANES, 1)), _rows(_pad_to(gdn_dt_bias, LANES, 1)), _rows(gdn_norm_g)

    for i in range(depth):
        x = _ffn_call(x, [_param(w1, i, 0), _param(w3, i, 0), _param(w2, i, 0), _param(lng, i, 0), _param(lnb, i, 0)], tm)
        o_rw = _rw_call(x, [_param(w_rw, i), _param(rw_mu_p, i), _param(rw_w0_r, i), _param(rw_w2_p, i),
                            _param(rw_a0_r, i), _param(rw_a2_p, i), _param(rw_g2_p, i)]
                        + [_param(v, i) for v in rw_vecs], bsz, seq, tb)
        o_gla = _gla_call(x, [_param(w_gla, i), _param(gla_w2_p, i), _param(gla_b_r, i), _param(gla_ng, i)], bsz, seq, tb_gla)
        o_gdn = _gdn_call(x, [_param(w_gdn, i), _param(gdn_conv_w, i), _param(gdn_alog, i), _param(gdn_dtb, i),
                              _param(gdn_ng, i)], bsz, seq, tb_gla)
        x = _merge_call(x, (o_rw, o_gla, o_gdn), [_param(w_gate, i), _param(w_br, i), _param(w_out, i),
                                                  _param(lng, i, 1), _param(lnb, i, 1)], tm)
        x = _ffn_ple_call(x, p, i, [_param(w1, i, 1), _param(w3, i, 1), _param(w2, i, 1), _param(lng, i, 2), _param(lnb, i, 2),
                                    _param(w_pg, i), _param(w_pp, i), _param(lng, i, 3), _param(lnb, i, 3)], tm)
    return x.reshape(bsz, seq, d)
```
